```python
import math
import jax
import jax.numpy as jnp
from jax import lax
import numpy as np

D_MODEL = 1024
BATCH = 2
SEQ = 8192
DEPTH = 2

N_HEADS = 8
HEAD_DIM = 64
ATTN_WIDTH = N_HEADS * HEAD_DIM
MOBA_BLOCK = 256
MOBA_TOPK = 3
QUERY_CHUNK = 64
ROPE_THETA = 10000.0
S5_GROUP = 16
S5_GROUPS = 32
S5_WIDTH = S5_GROUP * S5_GROUPS
S5_STATE = 64
DT_MIN = 0.001
DT_MAX = 0.1
IN_COLS = 3 * ATTN_WIDTH + S5_WIDTH + 2 * D_MODEL
N_EXPERTS = 256
TOP_K = 8
N_EXPERT_GROUPS = 8
TOPK_GROUPS = 4
EXPERT_HIDDEN = 256
SHARED_HIDDEN = 256
ROUTED_SCALE = 2.5
MOE_ROW_BLOCK = 64
DN_ALPHA = (2 * DEPTH) ** 0.25
DN_BETA = (8 * DEPTH) ** -0.25
LN_EPS = 1e-5
NEG_INF = -1e30

kernel_name = 'hybrid_moba_s5_moe_deepnorm'


def layer_norm(x, g, b):
    xf = x.astype(jnp.float32)
    mu = jnp.mean(xf, axis=-1, keepdims=True)
    xc = xf - mu
    var = jnp.mean(xc * xc, axis=-1, keepdims=True)
    return (xc * lax.rsqrt(var + LN_EPS) * g.astype(jnp.float32) + b.astype(jnp.float32)).astype(x.dtype)


def rope_tables(positions):
    inv_freq = ROPE_THETA ** (-jnp.arange(0, HEAD_DIM, 2, dtype=jnp.float32) / HEAD_DIM)
    ang = positions.astype(jnp.float32)[..., None] * inv_freq
    return jnp.cos(ang)[:, :, None, :], jnp.sin(ang)[:, :, None, :]


def apply_rope(t, cos, sin):
    half = t.shape[-1] // 2
    t1 = t[..., :half].astype(jnp.float32)
    t2 = t[..., half:].astype(jnp.float32)
    return jnp.concatenate([t1 * cos - t2 * sin, t2 * cos + t1 * sin], axis=-1).astype(t.dtype)


def moba_attention(q, k, v):
    bsz, seq, n_heads, hd = q.shape
    n_blocks = -(-seq // MOBA_BLOCK)
    pad = n_blocks * MOBA_BLOCK - seq
    qh = q.transpose(0, 2, 1, 3)
    kh = jnp.pad(k.transpose(0, 2, 1, 3), ((0, 0), (0, 0), (0, pad), (0, 0)))
    vh = jnp.pad(v.transpose(0, 2, 1, 3), ((0, 0), (0, 0), (0, pad), (0, 0)))
    kb = kh.reshape(bsz, n_heads, n_blocks, MOBA_BLOCK, hd)
    vb = vh.reshape(bsz, n_heads, n_blocks, MOBA_BLOCK, hd)
    k_mean = jnp.mean(kb.astype(jnp.float32), axis=3)
    n_sel = min(MOBA_TOPK, n_blocks)
    scale = hd ** -0.5
    b_ix = jnp.arange(bsz)[:, None, None, None]
    h_ix = jnp.arange(n_heads)[None, :, None, None]
    blk_ids = jnp.arange(n_blocks)

    def chunk(c):
        q0 = c * QUERY_CHUNK
        qblk = q0 // MOBA_BLOCK
        qc = lax.dynamic_slice_in_dim(qh, q0, QUERY_CHUNK, axis=2).astype(jnp.float32)
        gate = jnp.einsum('bhqd,bhnd->bhqn', qc, k_mean)
        gate = jnp.where(blk_ids < qblk, gate, NEG_INF)
        _, sel = lax.top_k(gate, n_sel)
        sel_valid = sel < qblk
        kg = kb[b_ix, h_ix, sel].astype(jnp.float32)
        vg = vb[b_ix, h_ix, sel].astype(jnp.float32)
        s_sel = jnp.einsum('bhqd,bhqjtd->bhqjt', qc, kg) * scale
        s_sel = jnp.where(sel_valid[..., None], s_sel, NEG_INF)
        s_sel = s_sel.reshape(bsz, n_heads, QUERY_CHUNK, n_sel * MOBA_BLOCK)
        k_own = lax.dynamic_slice_in_dim(kh, qblk * MOBA_BLOCK, MOBA_BLOCK, axis=2).astype(jnp.float32)
        v_own = lax.dynamic_slice_in_dim(vh, qblk * MOBA_BLOCK, MOBA_BLOCK, axis=2).astype(jnp.float32)
        s_own = jnp.einsum('bhqd,bhtd->bhqt', qc, k_own) * scale
        q_pos = q0 + jnp.arange(QUERY_CHUNK)
        k_pos = qblk * MOBA_BLOCK + jnp.arange(MOBA_BLOCK)
        s_own = jnp.where(k_pos[None, :] <= q_pos[:, None], s_own, NEG_INF)
        p = jax.nn.softmax(jnp.concatenate([s_sel, s_own], axis=-1), axis=-1)
        p_sel = p[..., :n_sel * MOBA_BLOCK].reshape(bsz, n_heads, QUERY_CHUNK, n_sel, MOBA_BLOCK)
        p_own = p[..., n_sel * MOBA_BLOCK:]
        o = jnp.einsum('bhqjt,bhqjtd->bhqd', p_sel, vg) + jnp.einsum('bhqt,bhtd->bhqd', p_own, v_own)
        return o.astype(q.dtype)

    out = lax.map(chunk, jnp.arange(seq // QUERY_CHUNK))
    return out.transpose(1, 0, 3, 2, 4).reshape(bsz, seq, n_heads * hd)


def s5_branch(u, lam_re, lam_im, log_dt, b_re, b_im, c_re, c_im, d_skip, w_glu, b_glu):
    bsz, seq, width = u.shape
    uf = u.astype(jnp.float32)
    ug = uf.reshape(bsz, seq, S5_GROUPS, S5_GROUP)
    lr = lam_re.astype(jnp.float32)
    li = lam_im.astype(jnp.float32)
    dt = jnp.exp(log_dt.astype(jnp.float32))[:, None]
    mag = jnp.exp(lr * dt)
    a_re = mag * jnp.cos(li * dt)
    a_im = mag * jnp.sin(li * dt)
    den = lr * lr + li * li
    nr = a_re - 1.0
    coef_re = (nr * lr + a_im * li) / den
    coef_im = (a_im * lr - nr * li) / den
    br = b_re.astype(jnp.float32)
    bi = b_im.astype(jnp.float32)
    bbar_re = coef_re[..., None] * br - coef_im[..., None] * bi
    bbar_im = coef_re[..., None] * bi + coef_im[..., None] * br
    bu_re = jnp.einsum('bsgc,gnc->bsgn', ug, bbar_re)
    bu_im = jnp.einsum('bsgc,gnc->bsgn', ug, bbar_im)
    at_re = jnp.broadcast_to(a_re, bu_re.shape)
    at_im = jnp.broadcast_to(a_im, bu_im.shape)

    def combine(e1, e2):
        a1r, a1i, b1r, b1i = e1
        a2r, a2i, b2r, b2i = e2
        return (a2r * a1r - a2i * a1i,
                a2r * a1i + a2i * a1r,
                a2r * b1r - a2i * b1i + b2r,
                a2r * b1i + a2i * b1r + b2i)

    _, _, h_re, h_im = lax.associative_scan(combine, (at_re, at_im, bu_re, bu_im), axis=1)
    y = (jnp.einsum('bsgn,gcn->bsgc', h_re, c_re.astype(jnp.float32))
         - jnp.einsum('bsgn,gcn->bsgc', h_im, c_im.astype(jnp.float32)))
    y = y.reshape(bsz, seq, width) + d_skip.astype(jnp.float32) * uf
    z = jax.nn.gelu(y)
    out = z * jax.nn.sigmoid(z @ w_glu.astype(jnp.float32) + b_glu.astype(jnp.float32))
    return out.astype(u.dtype)


def routed_experts(xt, top_idx, gate_w, w_gate_e, w_up_e, w_down_e):
    n_tok, d = xt.shape
    n_exp = w_gate_e.shape[0]
    m = MOE_ROW_BLOCK
    n_assign = n_tok * TOP_K
    flat_e = top_idx.reshape(n_assign)
    flat_tok = jnp.repeat(jnp.arange(n_tok, dtype=jnp.int32), TOP_K)
    flat_w = gate_w.reshape(n_assign)
    order = jnp.argsort(flat_e)
    e_s = flat_e[order]
    tok_s = flat_tok[order]
    w_s = flat_w[order]
    counts = jnp.bincount(flat_e, length=n_exp)
    starts = jnp.cumsum(counts) - counts
    padded = (counts + m - 1) // m * m
    padded_ends = jnp.cumsum(padded)
    padded_starts = padded_ends - padded
    dest = padded_starts[e_s] + jnp.arange(n_assign) - starts[e_s]
    n_row_blocks = -(-(n_assign + n_exp * (m - 1)) // m)
    n_rows = n_row_blocks * m
    row_tok = jnp.full((n_rows,), n_tok, jnp.int32).at[dest].set(tok_s)
    row_w = jnp.zeros((n_rows,), jnp.float32).at[dest].set(w_s)
    block_e = jnp.minimum(jnp.searchsorted(padded_ends, jnp.arange(n_row_blocks) * m, side='right'), n_exp - 1)
    x_pad = jnp.concatenate([xt, jnp.zeros((1, d), xt.dtype)], axis=0)

    def block(args):
        e, toks, wts = args
        xb = x_pad[toks]
        h = jax.nn.silu(xb @ w_gate_e[e]) * (xb @ w_up_e[e])
        return (h @ w_down_e[e]) * wts[:, None]

    y = lax.map(block, (block_e, row_tok.reshape(n_row_blocks, m), row_w.reshape(n_row_blocks, m)))
    return jax.ops.segment_sum(y.reshape(n_rows, d), row_tok, num_segments=n_tok + 1)[:n_tok]


def moe_ffn(x, w_router, router_bias, w_gate_e, w_up_e, w_down_e, ws_gate, ws_up, ws_down):
    bsz, seq, d = x.shape
    n_tok = bsz * seq
    xt = x.reshape(n_tok, d)
    scores = jax.nn.sigmoid(xt.astype(jnp.float32) @ w_router.astype(jnp.float32))
    biased = scores + router_bias.astype(jnp.float32)
    per_group = N_EXPERTS // N_EXPERT_GROUPS
    group_score = lax.top_k(biased.reshape(n_tok, N_EXPERT_GROUPS, per_group), 2)[0].sum(-1)
    _, top_groups = lax.top_k(group_score, TOPK_GROUPS)
    group_mask = jax.nn.one_hot(top_groups, N_EXPERT_GROUPS, dtype=jnp.float32).sum(-2) > 0
    expert_mask = jnp.repeat(group_mask, per_group, axis=-1)
    _, top_idx = lax.top_k(jnp.where(expert_mask, biased, NEG_INF), TOP_K)
    w = jnp.take_along_axis(scores, top_idx, axis=-1)
    w = w / jnp.sum(w, axis=-1, keepdims=True) * ROUTED_SCALE
    routed = routed_experts(xt, top_idx, w, w_gate_e, w_up_e, w_down_e)
    shared = (jax.nn.silu(xt @ ws_gate) * (xt @ ws_up)) @ ws_down
    return (routed + shared).astype(x.dtype).reshape(bsz, seq, d)


def setup_inputs(seed: int = 0) -> dict:
    key = jax.random.key(seed)
    ks = jax.random.split(key, 32)
    f32 = jnp.float32

    def nrm(k, shape, fan_in, scale=1.0):
        return jax.random.normal(k, shape, f32) * (scale * fan_in ** -0.5)

    x = jax.random.normal(ks[0], (BATCH, SEQ, D_MODEL), f32)
    positions = jnp.broadcast_to(jnp.arange(SEQ, dtype=jnp.int32), (BATCH, SEQ))
    w_in = nrm(ks[1], (DEPTH, D_MODEL, IN_COLS), D_MODEL)
    attn_up = nrm(ks[2], (DEPTH, ATTN_WIDTH, D_MODEL), ATTN_WIDTH)
    lam_re = -0.5 + 0.01 * jax.random.normal(ks[3], (DEPTH, S5_GROUPS, S5_STATE), f32)
    lam_im = (math.pi * jnp.arange(S5_STATE, dtype=f32)
              + 0.01 * jax.random.normal(ks[4], (DEPTH, S5_GROUPS, S5_STATE), f32))
    log_dt = jax.random.uniform(ks[5], (DEPTH, S5_GROUPS), f32, math.log(DT_MIN), math.log(DT_MAX))
    b_re = nrm(ks[6], (DEPTH, S5_GROUPS, S5_STATE, S5_GROUP), 2 * S5_GROUP)
    b_im = nrm(ks[7], (DEPTH, S5_GROUPS, S5_STATE, S5_GROUP), 2 * S5_GROUP)
    c_re = nrm(ks[8], (DEPTH, S5_GROUPS, S5_GROUP, S5_STATE), 2 * S5_STATE)
    c_im = nrm(ks[9], (DEPTH, S5_GROUPS, S5_GROUP, S5_STATE), 2 * S5_STATE)
    d_skip = jax.random.normal(ks[10], (DEPTH, S5_WIDTH), f32)
    w_glu = nrm(ks[11], (DEPTH, S5_WIDTH, S5_WIDTH), S5_WIDTH)
    b_glu = 0.01 * jax.random.normal(ks[12], (DEPTH, S5_WIDTH), f32)
    s5_up = nrm(ks[13], (DEPTH, S5_WIDTH, D_MODEL), S5_WIDTH)
    w_o = nrm(ks[14], (DEPTH, D_MODEL, D_MODEL), D_MODEL, DN_BETA)
    ln1_g = 1.0 + 0.01 * jax.random.normal(ks[15], (DEPTH, D_MODEL), f32)
    ln1_b = 0.01 * jax.random.normal(ks[16], (DEPTH, D_MODEL), f32)
    w_router = nrm(ks[17], (DEPTH, D_MODEL, N_EXPERTS), D_MODEL)
    router_bias = 0.01 * jax.random.normal(ks[18], (DEPTH, N_EXPERTS), f32)
    w_gate_e = nrm(ks[19], (DEPTH, N_EXPERTS, D_MODEL, EXPERT_HIDDEN), D_MODEL)
    w_up_e = nrm(ks[20], (DEPTH, N_EXPERTS, D_MODEL, EXPERT_HIDDEN), D_MODEL)
    w_down_e = nrm(ks[21], (DEPTH, N_EXPERTS, EXPERT_HIDDEN, D_MODEL), EXPERT_HIDDEN, DN_BETA)
    ws_gate = nrm(ks[22], (DEPTH, D_MODEL, SHARED_HIDDEN), D_MODEL)
    ws_up = nrm(ks[23], (DEPTH, D_MODEL, SHARED_HIDDEN), D_MODEL)
    ws_down = nrm(ks[24], (DEPTH, SHARED_HIDDEN, D_MODEL), SHARED_HIDDEN, DN_BETA)
    ln2_g = 1.0 + 0.01 * jax.random.normal(ks[25], (DEPTH, D_MODEL), f32)
    ln2_b = 0.01 * jax.random.normal(ks[26], (DEPTH, D_MODEL), f32)
    return {'x': x, 'positions': positions, 'w_in': w_in, 'attn_up': attn_up,
            'lam_re': lam_re, 'lam_im': lam_im, 'log_dt': log_dt, 'b_re': b_re, 'b_im': b_im,
            'c_re': c_re, 'c_im': c_im, 'd_skip': d_skip, 'w_glu': w_glu, 'b_glu': b_glu,
            's5_up': s5_up, 'w_o': w_o, 'ln1_g': ln1_g, 'ln1_b': ln1_b,
            'w_router': w_router, 'router_bias': router_bias, 'w_gate_e': w_gate_e,
            'w_up_e': w_up_e, 'w_down_e': w_down_e, 'ws_gate': ws_gate, 'ws_up': ws_up,
            'ws_down': ws_down, 'ln2_g': ln2_g, 'ln2_b': ln2_b}


def reference(x, positions, w_in, attn_up, lam_re, lam_im, log_dt, b_re, b_im, c_re, c_im,
              d_skip, w_glu, b_glu, s5_up, w_o, ln1_g, ln1_b, w_router, router_bias,
              w_gate_e, w_up_e, w_down_e, ws_gate, ws_up, ws_down, ln2_g, ln2_b):
    bsz, seq, _ = x.shape
    cos, sin = rope_tables(positions)
    split_at = [ATTN_WIDTH, 2 * ATTN_WIDTH, 3 * ATTN_WIDTH,
                3 * ATTN_WIDTH + S5_WIDTH, 3 * ATTN_WIDTH + S5_WIDTH + D_MODEL]
    for l in range(DEPTH):
        proj = x @ w_in[l]
        q, k, v, u, g_a, g_s = jnp.split(proj, split_at, axis=-1)
        q = apply_rope(q.reshape(bsz, seq, N_HEADS, HEAD_DIM), cos, sin)
        k = apply_rope(k.reshape(bsz, seq, N_HEADS, HEAD_DIM), cos, sin)
        v = v.reshape(bsz, seq, N_HEADS, HEAD_DIM)
        a = moba_attention(q, k, v) @ attn_up[l]
        s = s5_branch(u, lam_re[l], lam_im[l], log_dt[l], b_re[l], b_im[l], c_re[l], c_im[l],
                      d_skip[l], w_glu[l], b_glu[l]) @ s5_up[l]
        mixed = jax.nn.sigmoid(g_a) * a + jax.nn.sigmoid(g_s) * s
        x = layer_norm(DN_ALPHA * x + mixed @ w_o[l], ln1_g[l], ln1_b[l])
        f = moe_ffn(x, w_router[l], router_bias[l], w_gate_e[l], w_up_e[l], w_down_e[l],
                    ws_gate[l], ws_up[l], ws_down[l])
        x = layer_norm(DN_ALPHA * x + f, ln2_g[l], ln2_b[l])
    return x
```

```python
import functools
import math

import jax
import jax.numpy as jnp
from jax import lax
from jax.experimental import pallas as pl
from jax.experimental.pallas import tpu as pltpu

F32 = jnp.float32
BF16 = jnp.bfloat16
I32 = jnp.int32

N_HEADS = 8
HEAD_DIM = 64
ATTN_WIDTH = N_HEADS * HEAD_DIM
MOBA_BLOCK = 256
MOBA_TOPK = 3
ROPE_THETA = 10000.0
S5_GROUP = 16
S5_GROUPS = 32
S5_WIDTH = S5_GROUP * S5_GROUPS
S5_STATE = 64
S5_CHUNK = 16
N_EXPERTS = 256
TOP_K = 8
N_EXPERT_GROUPS = 8
TOPK_GROUPS = 4
GROUP_SIZE = N_EXPERTS // N_EXPERT_GROUPS
ROUTED_SCALE = 2.5
LN_EPS = 1e-5
NEG_INF = -1e30
REMOVED = -3e38

LANES = 128
MOE_ROWS = 256
VMEM_LIMIT = 56 * 1024 * 1024


def _cparams(sem):
    return pltpu.CompilerParams(dimension_semantics=sem, vmem_limit_bytes=VMEM_LIMIT)


def _sigmoid(x):
    return 1.0 / (1.0 + jnp.exp(-x))


def _layer_norm(r, g, b):
    mu = jnp.mean(r, axis=-1, keepdims=True)
    rc = r - mu
    var = jnp.mean(rc * rc, axis=-1, keepdims=True)
    return rc * lax.rsqrt(var + LN_EPS) * g + b


def _dot(a, b):
    return jnp.dot(a, b, preferred_element_type=F32)


def _dot_nt(a, b):
    return lax.dot_general(a, b, (((1,), (1,)), ((), ())), preferred_element_type=F32)


def _proj_kernel(x_ref, w_ref, cos_ref, sin_ref, q_ref, k_ref, v_ref, u_ref, ga_ref, gs_ref):
    xb = x_ref[...].astype(BF16)
    cos = cos_ref[...]
    sin = sin_ref[...]
    lane = lax.broadcasted_iota(I32, cos.shape, 1)
    first_half = (lane % HEAD_DIM) < (HEAD_DIM // 2)

    def rope(t):
        rot = jnp.where(first_half, pltpu.roll(t, LANES - HEAD_DIM // 2, axis=1),
                        pltpu.roll(t, HEAD_DIM // 2, axis=1))
        return t * cos + rot * sin

    aw = ATTN_WIDTH
    for c in range(aw // LANES):
        sl = slice(c * LANES, (c + 1) * LANES)
        qc = _dot(xb, w_ref[:, c * LANES:(c + 1) * LANES])
        q_ref[:, sl] = (rope(qc) * (HEAD_DIM ** -0.5)).astype(BF16)
        kc = _dot(xb, w_ref[:, aw + c * LANES:aw + (c + 1) * LANES])
        k_ref[:, sl] = rope(kc).astype(BF16)
    v_ref[...] = _dot(xb, w_ref[:, 2 * aw:3 * aw]).astype(BF16)
    u_ref[...] = _dot(xb, w_ref[:, 3 * aw:3 * aw + S5_WIDTH])
    d = ga_ref.shape[1]
    off = 3 * aw + S5_WIDTH
    ga_ref[...] = _dot(xb, w_ref[:, off:off + d])
    gs_ref[...] = _dot(xb, w_ref[:, off + d:off + 2 * d])


def _proj(x2, w_bf, cos_t, sin_t, tm=512):
    t, d = x2.shape
    n = w_bf.shape[1]
    aw = ATTN_WIDTH
    row = lambda i: (i, 0)
    return pl.pallas_call(
        _proj_kernel,
        grid=(t // tm,),
        in_specs=[pl.BlockSpec((tm, d), row),
                  pl.BlockSpec((d, n), lambda i: (0, 0)),
                  pl.BlockSpec((tm, LANES), row),
                  pl.BlockSpec((tm, LANES), row)],
        out_specs=[pl.BlockSpec((tm, aw), row), pl.BlockSpec((tm, aw), row), pl.BlockSpec((tm, aw), row),
                   pl.BlockSpec((tm, S5_WIDTH), row), pl.BlockSpec((tm, d), row), pl.BlockSpec((tm, d), row)],
        out_shape=[jax.ShapeDtypeStruct((t, aw), BF16), jax.ShapeDtypeStruct((t, aw), BF16),
                   jax.ShapeDtypeStruct((t, aw), BF16), jax.ShapeDtypeStruct((t, S5_WIDTH), F32),
                   jax.ShapeDtypeStruct((t, d), F32), jax.ShapeDtypeStruct((t, d), F32)],
        compiler_params=_cparams(("parallel",)),
        name="proj_rope",
    )(x2, w_bf, cos_t, sin_t)


def _attn_kernel(q_ref, k_ref, v_ref, o_ref, kmean_ref):
    i = pl.program_id(2)
    blk = MOBA_BLOCK
    n_blocks = k_ref.shape[0] // blk

    @pl.when(i == 0)
    def _():
        kmean_ref[...] = jnp.zeros(kmean_ref.shape, kmean_ref.dtype)
        for j in range(n_blocks):
            kj = k_ref[j * blk:(j + 1) * blk, :].astype(F32)
            kmean_ref[j:j + 1, :] = jnp.sum(kj, axis=0, keepdims=True) * (1.0 / blk)

    q = q_ref[...]
    lane = lax.broadcasted_iota(I32, (blk, LANES), 1)
    zero_b = jnp.zeros((blk, LANES), BF16)
    kmean = kmean_ref[...].astype(BF16)
    k_own = k_ref[pl.ds(pl.multiple_of(i * blk, blk), blk), :]
    v_own = v_ref[pl.ds(pl.multiple_of(i * blk, blk), blk), :]
    row = lax.broadcasted_iota(I32, (blk, blk), 0)
    col = lax.broadcasted_iota(I32, (blk, blk), 1)
    causal = col <= row

    q_aug, head_mask, m0, l0, acc0 = [], [], [], [], []
    for h in range(2):
        in_head = (lane >= h * HEAD_DIM) & (lane < (h + 1) * HEAD_DIM)
        head_mask.append(in_head)
        qh = jnp.where(in_head, q, zero_b)
        g = jnp.where(lane < i, _dot_nt(qh, kmean), NEG_INF)
        selected = jnp.zeros((blk, LANES), jnp.bool_)
        for _ in range(MOBA_TOPK):
            mx = jnp.max(g, axis=1, keepdims=True)
            first = jnp.min(jnp.where(g == mx, lane, LANES), axis=1, keepdims=True)
            pick = lane == first
            selected = selected | (pick & (lane < i))
            g = jnp.where(pick, REMOVED, g)
        sel_neg = jnp.where(selected, 0.0, NEG_INF).astype(BF16)
        q_aug.append(jnp.concatenate([qh, sel_neg], axis=1))
        s = jnp.where(causal, _dot_nt(qh, k_own), NEG_INF)
        mx = jnp.max(s, axis=1, keepdims=True)
        p = jnp.exp(s - mx)
        m0.append(mx)
        l0.append(jnp.sum(p, axis=1, keepdims=True))
        acc0.append(_dot(p.astype(BF16), v_own))

    def body(j, carry):
        out = []
        kj = k_ref[pl.ds(pl.multiple_of(j * blk, blk), blk), :]
        vj = v_ref[pl.ds(pl.multiple_of(j * blk, blk), blk), :]
        k_aug = jnp.concatenate([kj, (lane == j).astype(BF16)], axis=1)
        for h in range(2):
            m, l, acc = carry[h]
            s = _dot_nt(q_aug[h], k_aug)
            m_new = jnp.maximum(m, jnp.max(s, axis=1, keepdims=True))
            p = jnp.exp(s - m_new)
            alpha = jnp.exp(m - m_new)
            l = alpha * l + jnp.sum(p, axis=1, keepdims=True)
            acc = alpha * acc + _dot(p.astype(BF16), vj)
            out.append((m_new, l, acc))
        return tuple(out)

    res = lax.fori_loop(0, i, body, tuple((m0[h], l0[h], acc0[h]) for h in range(2)))
    o = jnp.zeros((blk, LANES), F32)
    for h in range(2):
        _, l, acc = res[h]
        o = jnp.where(head_mask[h], acc / l, o)
    o_ref[...] = o.astype(BF16)


def _attention(q, k, v, bsz, seq):
    t, aw = q.shape
    blk = MOBA_BLOCK
    nb = seq // blk
    assert nb <= LANES
    qmap = lambda b, p, i: (b * nb + i, p)
    kvmap = lambda b, p, i: (b, p)
    return pl.pallas_call(
        _attn_kernel,
        grid=(bsz, aw // LANES, nb),
        in_specs=[pl.BlockSpec((blk, LANES), qmap),
                  pl.BlockSpec((seq, LANES), kvmap),
                  pl.BlockSpec((seq, LANES), kvmap)],
        out_specs=pl.BlockSpec((blk, LANES), qmap),
        out_shape=jax.ShapeDtypeStruct((t, aw), BF16),
        scratch_shapes=[pltpu.VMEM((LANES, LANES), F32)],
        compiler_params=_cparams(("parallel", "parallel", "arbitrary")),
        name="moba_attention",
    )(q, k, v)


def _s5_tables(lam_re, lam_im, log_dt, b_re, b_im, c_re, c_im):
    hp = lax.Precision.HIGHEST
    L = S5_CHUNK
    lr, li = lam_re.astype(F32), lam_im.astype(F32)
    dt = jnp.exp(log_dt.astype(F32))[:, None]
    mag = jnp.exp(lr * dt)
    a_re = mag * jnp.cos(li * dt)
    a_im = mag * jnp.sin(li * dt)
    den = lr * lr + li * li
    nr = a_re - 1.0
    coef_re = (nr * lr + a_im * li) / den
    coef_im = (a_im * lr - nr * li) / den
    br, bi = b_re.astype(F32), b_im.astype(F32)
    bbar_re = coef_re[..., None] * br - coef_im[..., None] * bi
    bbar_im = coef_re[..., None] * bi + coef_im[..., None] * br
    tau = jnp.arange(L + 1, dtype=F32)[:, None, None]
    pmag = jnp.exp(tau * (lr * dt))
    p_re = pmag * jnp.cos(tau * (li * dt))
    p_im = pmag * jnp.sin(tau * (li * dt))
    cr, ci = c_re.astype(F32), c_im.astype(F32)
    ca_re = cr[None] * p_re[:, :, None, :] - ci[None] * p_im[:, :, None, :]
    ca_im = cr[None] * p_im[:, :, None, :] + ci[None] * p_re[:, :, None, :]
    kmat = (jnp.einsum('tgon,gni->tgio', ca_re[:L], bbar_re, precision=hp)
            - jnp.einsum('tgon,gni->tgio', ca_im[:L], bbar_im, precision=hp))
    s_ix = jnp.arange(L)[:, None]
    t_ix = jnp.arange(L)[None, :]
    lag = jnp.clip(t_ix - s_ix, 0, L - 1)
    m = kmat[lag]
    m = jnp.where((t_ix >= s_ix)[:, :, None, None, None], m, 0.0)
    m_intra = m.transpose(2, 0, 3, 1, 4).reshape(S5_GROUPS, L * S5_GROUP, L * S5_GROUP)
    rev = p_re[L - 1 - jnp.arange(L)], p_im[L - 1 - jnp.arange(L)]
    w_re = rev[0][..., None] * bbar_re[None] - rev[1][..., None] * bbar_im[None]
    w_im = rev[0][..., None] * bbar_im[None] + rev[1][..., None] * bbar_re[None]
    w_re = w_re.transpose(1, 0, 3, 2).reshape(S5_GROUPS, L * S5_GROUP, S5_STATE)
    w_im = w_im.transpose(1, 0, 3, 2).reshape(S5_GROUPS, L * S5_GROUP, S5_STATE)
    w_state = jnp.concatenate([w_re, w_im, w_im, w_re], axis=-1)
    cx_re = ca_re[1:].transpose(1, 3, 0, 2).reshape(S5_GROUPS, S5_STATE, L * S5_GROUP)
    cx_im = ca_im[1:].transpose(1, 3, 0, 2).reshape(S5_GROUPS, S5_STATE, L * S5_GROUP)
    cx = jnp.concatenate([cx_re, -cx_im], axis=1)
    al_re, al_im = p_re[L], p_im[L]
    a1 = jnp.concatenate([al_re, al_re], axis=-1)
    a2 = jnp.concatenate([-al_im, al_im], axis=-1)
    return m_intra.astype(BF16), w_state.astype(BF16), cx.astype(BF16), a1, a2


def _s5_state_kernel(u_ref, w_ref, e_ref):
    e_ref[0] = _dot(u_ref[0], w_ref[0])


def _s5_chunk_states(ug, w_state):
    g, r, c = ug.shape
    spec = lambda shape: pl.BlockSpec(shape, lambda i: (i, 0, 0))
    return pl.pallas_call(
        _s5_state_kernel,
        grid=(g,),
        in_specs=[spec((1, r, c)), spec((1, c, c))],
        out_specs=spec((1, r, c)),
        out_shape=jax.ShapeDtypeStruct((g, r, c), F32),
        compiler_params=_cparams(("parallel",)),
        name="s5_chunk_states",
    )(ug, w_state)


def _s5_scan_kernel(e_ref, a1_ref, a2_ref, h_ref, hp_ref, hq_ref):
    @pl.when(pl.program_id(0) == 0)
    def _():
        hp_ref[...] = jnp.zeros(hp_ref.shape, F32)
        hq_ref[...] = jnp.zeros(hq_ref.shape, F32)

    a1 = a1_ref[...]
    a2 = a2_ref[...]
    half = hp_ref.shape[1]

    def step(k, carry):
        hp, hq = carry
        h_ref[k] = hp
        e = e_ref[k]
        return (a1 * hp + a2 * hq + e[:, :half], a1 * hq - a2 * hp + e[:, half:])

    hp, hq = lax.fori_loop(0, e_ref.shape[0], step, (hp_ref[...], hq_ref[...]))
    hp_ref[...] = hp
    hq_ref[...] = hq


def _s5_scan(e_t, a1, a2, kb=64):
    nk, bg, c = e_t.shape
    half = c // 2
    kb = min(kb, nk)
    return pl.pallas_call(
        _s5_scan_kernel,
        grid=(nk // kb,),
        in_specs=[pl.BlockSpec((kb, bg, c), lambda i: (i, 0, 0)),
                  pl.BlockSpec((bg, half), lambda i: (0, 0)),
                  pl.BlockSpec((bg, half), lambda i: (0, 0))],
        out_specs=pl.BlockSpec((kb, bg, half), lambda i: (i, 0, 0)),
        out_shape=jax.ShapeDtypeStruct((nk, bg, half), F32),
        scratch_shapes=[pltpu.VMEM((bg, half), F32), pltpu.VMEM((bg, half), F32)],
        compiler_params=_cparams(("arbitrary",)),
        name="s5_scan",
    )(e_t, a1, a2)


def _s5_out_kernel(u_ref, h_ref, m_ref, cx_ref, y_ref):
    y_ref[0] = _dot(u_ref[0], m_ref[0]) + _dot(h_ref[0].astype(BF16), cx_ref[0])


def _s5_outputs(ug, h_in, m_intra, cx):
    g, r, c = ug.shape
    n2 = h_in.shape[2]
    spec = lambda shape: pl.BlockSpec(shape, lambda i: (i, 0, 0))
    return pl.pallas_call(
        _s5_out_kernel,
        grid=(g,),
        in_specs=[spec((1, r, c)), spec((1, r, n2)), spec((1, c, c)), spec((1, n2, c))],
        out_specs=spec((1, r, c)),
        out_shape=jax.ShapeDtypeStruct((g, r, c), F32),
        compiler_params=_cparams(("parallel",)),
        name="s5_outputs",
    )(ug, h_in, m_intra, cx)


def _s5_branch(u, tables, bsz, seq):
    m_intra, w_state, cx, a1, a2 = tables
    L, G, C = S5_CHUNK, S5_GROUPS, S5_GROUP
    nk = seq // L
    ug = u.reshape(bsz, nk, L, G, C).transpose(3, 0, 1, 2, 4).reshape(G, bsz * nk, L * C).astype(BF16)
    e = _s5_chunk_states(ug, w_state)
    e_t = e.reshape(G, bsz, nk, 2 * 2 * S5_STATE).transpose(2, 1, 0, 3).reshape(nk, bsz * G, 4 * S5_STATE)
    h = _s5_scan(e_t, jnp.tile(a1, (bsz, 1)), jnp.tile(a2, (bsz, 1)))
    h_in = h.reshape(nk, bsz, G, 2 * S5_STATE).transpose(2, 1, 0, 3).reshape(G, bsz * nk, 2 * S5_STATE)
    y = _s5_outputs(ug, h_in, m_intra, cx)
    return y.reshape(G, bsz, nk, L, C).transpose(1, 2, 3, 0, 4).reshape(bsz * seq, G * C)


def _merge_kernel(alpha, attn_ref, y_ref, u_ref, ga_ref, gs_ref, x_ref, aup_ref, wglu_ref, sup_ref, wo_ref,
                  dskip_ref, bglu_ref, g_ref, b_ref, o_ref):
    a = _dot(attn_ref[...], aup_ref[...])
    y = y_ref[...] + dskip_ref[...] * u_ref[...]
    z = 0.5 * y * (1.0 + jnp.tanh(math.sqrt(2.0 / math.pi) * (y + 0.044715 * (y * y * y))))
    gate = _dot(z.astype(BF16), wglu_ref[...]) + bglu_ref[...]
    s = _dot((z * _sigmoid(gate)).astype(BF16), sup_ref[...])
    mixed = _sigmoid(ga_ref[...]) * a + _sigmoid(gs_ref[...]) * s
    r = alpha * x_ref[...] + _dot(mixed.astype(BF16), wo_ref[...])
    o_ref[...] = _layer_norm(r, g_ref[...], b_ref[...])


def _merge(alpha, attn, y5, u, g_a, g_s, x2, aup, wglu, sup, wo, dskip, bglu, ln_g, ln_b, tm=256):
    t, d = x2.shape
    row = lambda i: (i, 0)
    full = lambda a: pl.BlockSpec(a.shape, lambda i: (0, 0))
    return pl.pallas_call(
        functools.partial(_merge_kernel, alpha),
        grid=(t // tm,),
        in_specs=[pl.BlockSpec((tm, ATTN_WIDTH), row), pl.BlockSpec((tm, S5_WIDTH), row),
                  pl.BlockSpec((tm, S5_WIDTH), row), pl.BlockSpec((tm, d), row), pl.BlockSpec((tm, d), row),
                  pl.BlockSpec((tm, d), row), full(aup), full(wglu), full(sup), full(wo),
                  full(dskip), full(bglu), full(ln_g), full(ln_b)],
        out_specs=pl.BlockSpec((tm, d), row),
        out_shape=jax.ShapeDtypeStruct((t, d), F32),
        compiler_params=_cparams(("parallel",)),
        name="merge_ln1",
    )(attn, y5, u, g_a, g_s, x2, aup, wglu, sup, wo, dskip, bglu, ln_g, ln_b)


def _router_kernel(x_ref, wh_ref, wl_ref, bias_ref, idx_ref, wt_ref, rank_ref, cnt_ref, carry_ref):
    tm = x_ref.shape[0]

    @pl.when(pl.program_id(0) == 0)
    def _():
        carry_ref[...] = jnp.zeros(carry_ref.shape, F32)

    x = x_ref[...]
    xh = x.astype(BF16)
    xl = (x - xh.astype(F32)).astype(BF16)
    wh = wh_ref[...]
    logits = _dot_nt(wh, xh) + _dot_nt(wh, xl) + _dot_nt(wl_ref[...], xh)
    scores = _sigmoid(logits)
    biased = scores + bias_ref[...]
    gs_rows = GROUP_SIZE
    row_g = lax.broadcasted_iota(I32, (gs_rows, tm), 0)
    group_score = []
    for g in range(N_EXPERT_GROUPS):
        vg = biased[g * gs_rows:(g + 1) * gs_rows, :]
        m1 = jnp.max(vg, axis=0, keepdims=True)
        i1 = jnp.min(jnp.where(vg == m1, row_g, gs_rows), axis=0, keepdims=True)
        m2 = jnp.max(jnp.where(row_g == i1, REMOVED, vg), axis=0, keepdims=True)
        group_score.append(m1 + m2)
    pieces = []
    for g in range(N_EXPERT_GROUPS):
        beats = jnp.zeros((1, tm), I32)
        for g2 in range(N_EXPERT_GROUPS):
            if g2 == g:
                continue
            better = group_score[g2] > group_score[g]
            if g2 < g:
                better = better | (group_score[g2] == group_score[g])
            beats = beats + better.astype(I32)
        keep = beats < TOPK_GROUPS
        pieces.append(jnp.where(keep, biased[g * gs_rows:(g + 1) * gs_rows, :], NEG_INF))
    v = jnp.concatenate(pieces, axis=0)
    row = lax.broadcasted_iota(I32, (N_EXPERTS, tm), 0)
    idxs, wts = [], []
    multi_hot = jnp.zeros((N_EXPERTS, tm), F32)
    for _ in range(TOP_K):
        m = jnp.max(v, axis=0, keepdims=True)
        ix = jnp.min(jnp.where(v == m, row, N_EXPERTS), axis=0, keepdims=True)
        hit = row == ix
        idxs.append(ix)
        wts.append(jnp.sum(jnp.where(hit, scores, 0.0), axis=0, keepdims=True))
        multi_hot = multi_hot + hit.astype(F32)
        v = jnp.where(hit, REMOVED, v)
    wsum = wts[0]
    for k in range(1, TOP_K):
        wsum = wsum + wts[k]
    s_ix = lax.broadcasted_iota(I32, (tm, tm), 0)
    t_ix = lax.broadcasted_iota(I32, (tm, tm), 1)
    mh = multi_hot.astype(BF16)
    before = _dot(mh, (s_ix < t_ix).astype(BF16)) + carry_ref[...]
    carry_ref[...] = carry_ref[...] + _dot(mh, jnp.ones((tm, tm), BF16))
    for k in range(TOP_K):
        idx_ref[k:k + 1, :] = idxs[k]
        wt_ref[k:k + 1, :] = wts[k] / wsum * ROUTED_SCALE
        rank_ref[k:k + 1, :] = jnp.sum(jnp.where(row == idxs[k], before, 0.0), axis=0, keepdims=True).astype(I32)
    cnt_ref[...] = carry_ref[:, :LANES]


def _router(x1, wr_hi, wr_lo, bias_b, tm=256):
    t, d = x1.shape
    e = wr_hi.shape[0]
    tok = lambda i: (0, i)
    const = lambda i: (0, 0)
    return pl.pallas_call(
        _router_kernel,
        grid=(t // tm,),
        in_specs=[pl.BlockSpec((tm, d), lambda i: (i, 0)), pl.BlockSpec((e, d), const),
                  pl.BlockSpec((e, d), const), pl.BlockSpec((e, tm), const)],
        out_specs=[pl.BlockSpec((TOP_K, tm), tok), pl.BlockSpec((TOP_K, tm), tok),
                   pl.BlockSpec((TOP_K, tm), tok), pl.BlockSpec((e, LANES), const)],
        out_shape=[jax.ShapeDtypeStruct((TOP_K, t), I32), jax.ShapeDtypeStruct((TOP_K, t), F32),
                   jax.ShapeDtypeStruct((TOP_K, t), I32), jax.ShapeDtypeStruct((e, LANES), F32)],
        scratch_shapes=[pltpu.VMEM((e, tm), F32)],
        compiler_params=_cparams(("arbitrary",)),
        name="router_topk_rank",
    )(x1, wr_hi, wr_lo, bias_b)


def _row_copy(src_hbm, dst, sem, src_row, dst_row):
    return pltpu.make_async_copy(src_hbm.at[pl.ds(src_row, 1), :], dst.at[pl.ds(dst_row, 1), :], sem)


def _expert_kernel(be_ref, nused_ref, tok_ref, x_hbm, wg_ref, wu_ref, wd_ref, y_ref, xbuf, sem):
    i = pl.program_id(0)
    mb = xbuf.shape[0]

    @pl.when(i < nused_ref[0])
    def _():
        def issue(r, c):
            _row_copy(x_hbm, xbuf, sem, tok_ref[0, 0, r], r).start()
            return c

        lax.fori_loop(0, mb, issue, 0)

        def drain(r, c):
            _row_copy(x_hbm, xbuf, sem, 0, r).wait()
            return c

        lax.fori_loop(0, mb, drain, 0)
        xb = xbuf[...].astype(BF16)
        hg = _dot(xb, wg_ref[0].astype(BF16))
        hu = _dot(xb, wu_ref[0].astype(BF16))
        h = hg * _sigmoid(hg) * hu
        y_ref[...] = _dot(h.astype(BF16), wd_ref[0].astype(BF16))

    @pl.when(i >= nused_ref[0])
    def _():
        y_ref[...] = jnp.zeros(y_ref.shape, F32)


def _experts(block_e, n_used, row_tok3, x1, w_gate_e, w_up_e, w_down_e):
    nb, _, mb = row_tok3.shape
    t, d = x1.shape
    hdim = w_gate_e.shape[2]
    grid_spec = pltpu.PrefetchScalarGridSpec(
        num_scalar_prefetch=2,
        grid=(nb,),
        in_specs=[pl.BlockSpec((1, 1, mb), lambda i, be, nu: (i, 0, 0), memory_space=pltpu.SMEM),
                  pl.BlockSpec(memory_space=pl.ANY),
                  pl.BlockSpec((1, d, hdim), lambda i, be, nu: (be[i], 0, 0)),
                  pl.BlockSpec((1, d, hdim), lambda i, be, nu: (be[i], 0, 0)),
                  pl.BlockSpec((1, hdim, d), lambda i, be, nu: (be[i], 0, 0))],
        out_specs=pl.BlockSpec((mb, d), lambda i, be, nu: (i, 0)),
        scratch_shapes=[pltpu.VMEM((mb, d), F32), pltpu.SemaphoreType.DMA(())],
    )
    return pl.pallas_call(
        _expert_kernel,
        grid_spec=grid_spec,
        out_shape=jax.ShapeDtypeStruct((nb * mb, d), F32),
        compiler_params=_cparams(("arbitrary",)),
        name="routed_experts",
    )(block_e, n_used, row_tok3, x1, w_gate_e, w_up_e, w_down_e)


def _final_kernel(alpha, dest_ref, x_ref, wt_ref, y_hbm, wsg_ref, wsu_ref, wsd_ref, g_ref, b_ref, o_ref, ybuf, sem):
    tm = x_ref.shape[0]
    n = TOP_K * tm

    def issue(r, c):
        k = r // tm
        t = r - k * tm
        _row_copy(y_hbm, ybuf.at[k], sem, dest_ref[k, t], t).start()
        return c

    lax.fori_loop(0, n, issue, 0)
    x = x_ref[...]
    xb = x.astype(BF16)
    hg = _dot(xb, wsg_ref[...])
    hu = _dot(xb, wsu_ref[...])
    f = _dot((hg * _sigmoid(hg) * hu).astype(BF16), wsd_ref[...])

    def drain(r, c):
        _row_copy(y_hbm, ybuf.at[0], sem, 0, 0).wait()
        return c

    lax.fori_loop(0, n, drain, 0)
    wt = wt_ref[...]
    for k in range(TOP_K):
        f = f + wt[:, k:k + 1] * ybuf[k]
    o_ref[...] = _layer_norm(alpha * x + f, g_ref[...], b_ref[...])


def _final(alpha, dest, x1, wt_tok, y_sorted, wsg, wsu, wsd, ln_g, ln_b, tm=256):
    t, d = x1.shape
    row = lambda i: (i, 0)
    full = lambda a: pl.BlockSpec(a.shape, lambda i: (0, 0))
    return pl.pallas_call(
        functools.partial(_final_kernel, alpha),
        grid=(t // tm,),
        in_specs=[pl.BlockSpec((TOP_K, tm), lambda i: (0, i), memory_space=pltpu.SMEM),
                  pl.BlockSpec((tm, d), row), pl.BlockSpec((tm, TOP_K), row),
                  pl.BlockSpec(memory_space=pl.ANY), full(wsg), full(wsu), full(wsd), full(ln_g), full(ln_b)],
        out_specs=pl.BlockSpec((tm, d), row),
        out_shape=jax.ShapeDtypeStruct((t, d), F32),
        scratch_shapes=[pltpu.VMEM((TOP_K, tm, d), F32), pltpu.SemaphoreType.DMA(())],
        compiler_params=_cparams(("arbitrary",)),
        name="combine_shared_ln2",
    )(dest, x1, wt_tok, y_sorted, wsg, wsu, wsd, ln_g, ln_b)


def _dispatch_plan(top_idx, rank, counts, n_tok):
    mb = MOE_ROWS
    n_assign = n_tok * TOP_K
    n_blocks = -(-(n_assign + N_EXPERTS * (mb - 1)) // mb)
    padded = (counts + mb - 1) // mb * mb
    padded_ends = jnp.cumsum(padded)
    padded_starts = padded_ends - padded
    dest = padded_starts[top_idx] + rank
    tok = jnp.broadcast_to(jnp.arange(n_tok, dtype=I32)[None, :], dest.shape)
    row_tok = jnp.zeros((n_blocks * mb,), I32).at[dest.reshape(-1)].set(tok.reshape(-1))
    block_e = jnp.minimum(jnp.searchsorted(padded_ends, jnp.arange(n_blocks, dtype=I32) * mb, side='right'),
                          N_EXPERTS - 1).astype(I32)
    n_used = (padded_ends[-1] // mb).astype(I32).reshape(1)
    return dest.astype(I32), row_tok.reshape(n_blocks, 1, mb), block_e, n_used


def kernel(x, positions, w_in, attn_up, lam_re, lam_im, log_dt, b_re, b_im, c_re, c_im, d_skip, w_glu, b_glu,
           s5_up, w_o, ln1_g, ln1_b, w_router, router_bias, w_gate_e, w_up_e, w_down_e, ws_gate, ws_up, ws_down,
           ln2_g, ln2_b):
    bsz, seq, d = x.shape
    depth = w_in.shape[0]
    t = bsz * seq
    alpha = (2 * depth) ** 0.25
    assert seq % MOBA_BLOCK == 0 and seq % S5_CHUNK == 0 and t % 512 == 0

    inv_freq = ROPE_THETA ** (-jnp.arange(0, HEAD_DIM, 2, dtype=F32) / HEAD_DIM)
    ang = positions.astype(F32).reshape(t, 1) * inv_freq
    cos, sin = jnp.cos(ang), jnp.sin(ang)
    cos_t = jnp.concatenate([cos, cos, cos, cos], axis=1)
    sin_t = jnp.concatenate([-sin, sin, -sin, sin], axis=1)

    xc = x.reshape(t, d)
    row2 = lambda a: a.reshape(1, -1).astype(F32)
    for l in range(depth):
        q, k, v, u, g_a, g_s = _proj(xc, w_in[l].astype(BF16), cos_t, sin_t)
        attn = _attention(q, k, v, bsz, seq)
        tables = _s5_tables(lam_re[l], lam_im[l], log_dt[l], b_re[l], b_im[l], c_re[l], c_im[l])
        y5 = _s5_branch(u, tables, bsz, seq)
        x1 = _merge(alpha, attn, y5, u, g_a, g_s, xc, attn_up[l].astype(BF16), w_glu[l].astype(BF16),
                    s5_up[l].astype(BF16), w_o[l].astype(BF16), row2(d_skip[l]), row2(b_glu[l]),
                    row2(ln1_g[l]), row2(ln1_b[l]))
        wr_t = w_router[l].astype(F32).T
        wr_hi = wr_t.astype(BF16)
        wr_lo = (wr_t - wr_hi.astype(F32)).astype(BF16)
        bias_b = jnp.broadcast_to(router_bias[l].astype(F32)[:, None], (N_EXPERTS, 256))
        top_idx, top_w, rank, counts = _router(x1, wr_hi, wr_lo, bias_b)
        dest, row_tok3, block_e, n_used = _dispatch_plan(top_idx, rank, counts[:, 0].astype(I32), t)
        y_sorted = _experts(block_e, n_used, row_tok3, x1, w_gate_e[l], w_up_e[l], w_down_e[l])
        xc = _final(alpha, dest, x1, top_w.T, y_sorted, ws_gate[l].astype(BF16), ws_up[l].astype(BF16),
                    ws_down[l].astype(BF16), row2(ln2_g[l]), row2(ln2_b[l]))
    return xc.reshape(bsz, seq, d)
```

```python
import functools
import math

import jax
import jax.numpy as jnp
from jax import lax
from jax.experimental import pallas as pl
from jax.experimental.pallas import tpu as pltpu

F32 = jnp.float32
BF16 = jnp.bfloat16
I32 = jnp.int32
U32 = jnp.uint32

N_HEADS = 8
HEAD_DIM = 64
ATTN_WIDTH = N_HEADS * HEAD_DIM
MOBA_BLOCK = 256
MOBA_TOPK = 3
ROPE_THETA = 10000.0
S5_GROUP = 16
S5_GROUPS = 32
S5_WIDTH = S5_GROUP * S5_GROUPS
S5_STATE = 64
S5_CHUNK = 16
N_EXPERTS = 256
TOP_K = 8
N_EXPERT_GROUPS = 8
TOPK_GROUPS = 4
GROUP_SIZE = N_EXPERTS // N_EXPERT_GROUPS
ROUTED_SCALE = 2.5
LN_EPS = 1e-5
NEG_INF = -1e30
REMOVED = -3e38

LANES = 128
MOE_ROWS = 256
VMEM_LIMIT = 56 * 1024 * 1024


def _cparams(sem):
    return pltpu.CompilerParams(dimension_semantics=sem, vmem_limit_bytes=VMEM_LIMIT)


def _sigmoid(x):
    return 1.0 / (1.0 + jnp.exp(-x))


def _layer_norm(r, g, b):
    mu = jnp.mean(r, axis=-1, keepdims=True)
    rc = r - mu
    var = jnp.mean(rc * rc, axis=-1, keepdims=True)
    return rc * lax.rsqrt(var + LN_EPS) * g + b


def _dot(a, b):
    return jnp.dot(a, b, preferred_element_type=F32)


def _dot_nt(a, b):
    return lax.dot_general(a, b, (((1,), (1,)), ((), ())), preferred_element_type=F32)


def _proj_kernel(x_ref, w_ref, cos_ref, sin_ref, q_ref, k_ref, v_ref, u_ref, ga_ref, gs_ref):
    xb = x_ref[...].astype(BF16)
    cos = cos_ref[...]
    sin = sin_ref[...]
    lane = lax.broadcasted_iota(I32, cos.shape, 1)
    first_half = (lane % HEAD_DIM) < (HEAD_DIM // 2)

    def rope(t):
        rot = jnp.where(first_half, pltpu.roll(t, LANES - HEAD_DIM // 2, axis=1),
                        pltpu.roll(t, HEAD_DIM // 2, axis=1))
        return t * cos + rot * sin

    aw = ATTN_WIDTH
    for c in range(aw // LANES):
        sl = slice(c * LANES, (c + 1) * LANES)
        qc = _dot(xb, w_ref[:, c * LANES:(c + 1) * LANES])
        q_ref[:, sl] = (rope(qc) * (HEAD_DIM ** -0.5)).astype(BF16)
        kc = _dot(xb, w_ref[:, aw + c * LANES:aw + (c + 1) * LANES])
        k_ref[:, sl] = rope(kc).astype(BF16)
    v_ref[...] = _dot(xb, w_ref[:, 2 * aw:3 * aw]).astype(BF16)
    u_ref[...] = _dot(xb, w_ref[:, 3 * aw:3 * aw + S5_WIDTH])
    d = ga_ref.shape[1]
    off = 3 * aw + S5_WIDTH
    ga_ref[...] = _dot(xb, w_ref[:, off:off + d])
    gs_ref[...] = _dot(xb, w_ref[:, off + d:off + 2 * d])


def _proj(x2, w_bf, cos_t, sin_t, tm=512):
    t, d = x2.shape
    n = w_bf.shape[1]
    aw = ATTN_WIDTH
    row = lambda i: (i, 0)
    return pl.pallas_call(
        _proj_kernel,
        grid=(t // tm,),
        in_specs=[pl.BlockSpec((tm, d), row),
                  pl.BlockSpec((d, n), lambda i: (0, 0)),
                  pl.BlockSpec((tm, LANES), row),
                  pl.BlockSpec((tm, LANES), row)],
        out_specs=[pl.BlockSpec((tm, aw), row), pl.BlockSpec((tm, aw), row), pl.BlockSpec((tm, aw), row),
                   pl.BlockSpec((tm, S5_WIDTH), row), pl.BlockSpec((tm, d), row), pl.BlockSpec((tm, d), row)],
        out_shape=[jax.ShapeDtypeStruct((t, aw), BF16), jax.ShapeDtypeStruct((t, aw), BF16),
                   jax.ShapeDtypeStruct((t, aw), BF16), jax.ShapeDtypeStruct((t, S5_WIDTH), F32),
                   jax.ShapeDtypeStruct((t, d), F32), jax.ShapeDtypeStruct((t, d), F32)],
        compiler_params=_cparams(("parallel",)),
        name="proj_rope",
    )(x2, w_bf, cos_t, sin_t)


def _attn_kernel(q_ref, k_ref, v_ref, o_ref, kmean_ref):
    i = pl.program_id(2)
    blk = MOBA_BLOCK
    n_blocks = k_ref.shape[0] // blk

    @pl.when(i == 0)
    def _():
        kmean_ref[...] = jnp.zeros(kmean_ref.shape, kmean_ref.dtype)
        for j in range(n_blocks):
            kj = k_ref[j * blk:(j + 1) * blk, :].astype(F32)
            kmean_ref[j:j + 1, :] = jnp.sum(kj, axis=0, keepdims=True) * (1.0 / blk)

    q = q_ref[...]
    lane = lax.broadcasted_iota(I32, (blk, LANES), 1)
    zero_b = jnp.zeros((blk, LANES), BF16)
    kmean = kmean_ref[...].astype(BF16)
    k_own = k_ref[pl.ds(pl.multiple_of(i * blk, blk), blk), :]
    v_own = v_ref[pl.ds(pl.multiple_of(i * blk, blk), blk), :]
    row = lax.broadcasted_iota(I32, (blk, blk), 0)
    col = lax.broadcasted_iota(I32, (blk, blk), 1)
    causal = col <= row

    q_aug, head_mask, m0, l0, acc0 = [], [], [], [], []
    for h in range(2):
        in_head = (lane >= h * HEAD_DIM) & (lane < (h + 1) * HEAD_DIM)
        head_mask.append(in_head)
        qh = jnp.where(in_head, q, zero_b)
        g = jnp.where(lane < i, _dot_nt(qh, kmean), NEG_INF)
        selected = jnp.zeros((blk, LANES), jnp.bool_)
        for _ in range(MOBA_TOPK):
            mx = jnp.max(g, axis=1, keepdims=True)
            first = jnp.min(jnp.where(g == mx, lane, LANES), axis=1, keepdims=True)
            pick = lane == first
            selected = selected | (pick & (lane < i))
            g = jnp.where(pick, REMOVED, g)
        sel_neg = jnp.where(selected, 0.0, NEG_INF).astype(BF16)
        q_aug.append(jnp.concatenate([qh, sel_neg], axis=1))
        s = jnp.where(causal, _dot_nt(qh, k_own), NEG_INF)
        mx = jnp.max(s, axis=1, keepdims=True)
        p = jnp.exp(s - mx)
        m0.append(mx)
        l0.append(jnp.sum(p, axis=1, keepdims=True))
        acc0.append(_dot(p.astype(BF16), v_own))

    def body(j, carry):
        out = []
        kj = k_ref[pl.ds(pl.multiple_of(j * blk, blk), blk), :]
        vj = v_ref[pl.ds(pl.multiple_of(j * blk, blk), blk), :]
        k_aug = jnp.concatenate([kj, (lane == j).astype(BF16)], axis=1)
        for h in range(2):
            m, l, acc = carry[h]
            s = _dot_nt(q_aug[h], k_aug)
            m_new = jnp.maximum(m, jnp.max(s, axis=1, keepdims=True))
            p = jnp.exp(s - m_new)
            alpha = jnp.exp(m - m_new)
            l = alpha * l + jnp.sum(p, axis=1, keepdims=True)
            acc = alpha * acc + _dot(p.astype(BF16), vj)
            out.append((m_new, l, acc))
        return tuple(out)

    res = lax.fori_loop(0, i, body, tuple((m0[h], l0[h], acc0[h]) for h in range(2)))
    o = jnp.zeros((blk, LANES), F32)
    for h in range(2):
        _, l, acc = res[h]
        o = jnp.where(head_mask[h], acc / l, o)
    o_ref[...] = o.astype(BF16)


def _attention(q, k, v, bsz, seq):
    t, aw = q.shape
    blk = MOBA_BLOCK
    nb = seq // blk
    assert nb <= LANES
    qmap = lambda b, p, i: (b * nb + i, p)
    kvmap = lambda b, p, i: (b, p)
    return pl.pallas_call(
        _attn_kernel,
        grid=(bsz, aw // LANES, nb),
        in_specs=[pl.BlockSpec((blk, LANES), qmap),
                  pl.BlockSpec((seq, LANES), kvmap),
                  pl.BlockSpec((seq, LANES), kvmap)],
        out_specs=pl.BlockSpec((blk, LANES), qmap),
        out_shape=jax.ShapeDtypeStruct((t, aw), BF16),
        scratch_shapes=[pltpu.VMEM((LANES, LANES), F32)],
        compiler_params=_cparams(("parallel", "parallel", "arbitrary")),
        name="moba_attention",
    )(q, k, v)


def _s5_tables(lam_re, lam_im, log_dt, b_re, b_im, c_re, c_im):
    hp = lax.Precision.HIGHEST
    L = S5_CHUNK
    lr, li = lam_re.astype(F32), lam_im.astype(F32)
    dt = jnp.exp(log_dt.astype(F32))[:, None]
    mag = jnp.exp(lr * dt)
    a_re = mag * jnp.cos(li * dt)
    a_im = mag * jnp.sin(li * dt)
    den = lr * lr + li * li
    nr = a_re - 1.0
    coef_re = (nr * lr + a_im * li) / den
    coef_im = (a_im * lr - nr * li) / den
    br, bi = b_re.astype(F32), b_im.astype(F32)
    bbar_re = coef_re[..., None] * br - coef_im[..., None] * bi
    bbar_im = coef_re[..., None] * bi + coef_im[..., None] * br
    tau = jnp.arange(L + 1, dtype=F32)[:, None, None]
    pmag = jnp.exp(tau * (lr * dt))
    p_re = pmag * jnp.cos(tau * (li * dt))
    p_im = pmag * jnp.sin(tau * (li * dt))
    cr, ci = c_re.astype(F32), c_im.astype(F32)
    ca_re = cr[None] * p_re[:, :, None, :] - ci[None] * p_im[:, :, None, :]
    ca_im = cr[None] * p_im[:, :, None, :] + ci[None] * p_re[:, :, None, :]
    kmat = (jnp.einsum('tgon,gni->tgio', ca_re[:L], bbar_re, precision=hp)
            - jnp.einsum('tgon,gni->tgio', ca_im[:L], bbar_im, precision=hp))
    s_ix = jnp.arange(L)[:, None]
    t_ix = jnp.arange(L)[None, :]
    lag = jnp.clip(t_ix - s_ix, 0, L - 1)
    m = kmat[lag]
    m = jnp.where((t_ix >= s_ix)[:, :, None, None, None], m, 0.0)
    m_intra = m.transpose(2, 0, 3, 1, 4).reshape(S5_GROUPS, L * S5_GROUP, L * S5_GROUP)
    rev = p_re[L - 1 - jnp.arange(L)], p_im[L - 1 - jnp.arange(L)]
    w_re = rev[0][..., None] * bbar_re[None] - rev[1][..., None] * bbar_im[None]
    w_im = rev[0][..., None] * bbar_im[None] + rev[1][..., None] * bbar_re[None]
    w_re = w_re.transpose(1, 0, 3, 2).reshape(S5_GROUPS, L * S5_GROUP, S5_STATE)
    w_im = w_im.transpose(1, 0, 3, 2).reshape(S5_GROUPS, L * S5_GROUP, S5_STATE)
    w_state = jnp.concatenate([w_re, w_im, w_im, w_re], axis=-1)
    cx_re = ca_re[1:].transpose(1, 3, 0, 2).reshape(S5_GROUPS, S5_STATE, L * S5_GROUP)
    cx_im = ca_im[1:].transpose(1, 3, 0, 2).reshape(S5_GROUPS, S5_STATE, L * S5_GROUP)
    cx = jnp.concatenate([cx_re, -cx_im], axis=1)
    al_re, al_im = p_re[L], p_im[L]
    a1 = jnp.concatenate([al_re, al_re], axis=-1)
    a2 = jnp.concatenate([-al_im, al_im], axis=-1)
    return m_intra.astype(BF16), w_state.astype(BF16), cx.astype(BF16), a1, a2


def _s5_state_kernel(u_ref, w_ref, e_ref):
    e_ref[0] = _dot(u_ref[0], w_ref[0])


def _s5_chunk_states(ug, w_state):
    g, r, c = ug.shape
    spec = lambda shape: pl.BlockSpec(shape, lambda i: (i, 0, 0))
    return pl.pallas_call(
        _s5_state_kernel,
        grid=(g,),
        in_specs=[spec((1, r, c)), spec((1, c, c))],
        out_specs=spec((1, r, c)),
        out_shape=jax.ShapeDtypeStruct((g, r, c), F32),
        compiler_params=_cparams(("parallel",)),
        name="s5_chunk_states",
    )(ug, w_state)


def _s5_scan_kernel(e_ref, a1_ref, a2_ref, h_ref, hp_ref, hq_ref):
    @pl.when(pl.program_id(0) == 0)
    def _():
        hp_ref[...] = jnp.zeros(hp_ref.shape, F32)
        hq_ref[...] = jnp.zeros(hq_ref.shape, F32)

    a1 = a1_ref[...]
    a2 = a2_ref[...]
    half = hp_ref.shape[1]

    def step(k, carry):
        hp, hq = carry
        h_ref[k] = hp
        e = e_ref[k]
        return (a1 * hp + a2 * hq + e[:, :half], a1 * hq - a2 * hp + e[:, half:])

    hp, hq = lax.fori_loop(0, e_ref.shape[0], step, (hp_ref[...], hq_ref[...]))
    hp_ref[...] = hp
    hq_ref[...] = hq


def _s5_scan(e_t, a1, a2, kb=64):
    nk, bg, c = e_t.shape
    half = c // 2
    kb = min(kb, nk)
    return pl.pallas_call(
        _s5_scan_kernel,
        grid=(nk // kb,),
        in_specs=[pl.BlockSpec((kb, bg, c), lambda i: (i, 0, 0)),
                  pl.BlockSpec((bg, half), lambda i: (0, 0)),
                  pl.BlockSpec((bg, half), lambda i: (0, 0))],
        out_specs=pl.BlockSpec((kb, bg, half), lambda i: (i, 0, 0)),
        out_shape=jax.ShapeDtypeStruct((nk, bg, half), F32),
        scratch_shapes=[pltpu.VMEM((bg, half), F32), pltpu.VMEM((bg, half), F32)],
        compiler_params=_cparams(("arbitrary",)),
        name="s5_scan",
    )(e_t, a1, a2)


def _s5_out_kernel(u_ref, h_ref, m_ref, cx_ref, y_ref):
    y_ref[0] = _dot(u_ref[0], m_ref[0]) + _dot(h_ref[0].astype(BF16), cx_ref[0])


def _s5_outputs(ug, h_in, m_intra, cx):
    g, r, c = ug.shape
    n2 = h_in.shape[2]
    spec = lambda shape: pl.BlockSpec(shape, lambda i: (i, 0, 0))
    return pl.pallas_call(
        _s5_out_kernel,
        grid=(g,),
        in_specs=[spec((1, r, c)), spec((1, r, n2)), spec((1, c, c)), spec((1, n2, c))],
        out_specs=spec((1, r, c)),
        out_shape=jax.ShapeDtypeStruct((g, r, c), F32),
        compiler_params=_cparams(("parallel",)),
        name="s5_outputs",
    )(ug, h_in, m_intra, cx)


def _s5_branch(u, tables, bsz, seq):
    m_intra, w_state, cx, a1, a2 = tables
    L, G, C = S5_CHUNK, S5_GROUPS, S5_GROUP
    nk = seq // L
    ug = u.reshape(bsz, nk, L, G, C).transpose(3, 0, 1, 2, 4).reshape(G, bsz * nk, L * C).astype(BF16)
    e = _s5_chunk_states(ug, w_state)
    e_t = e.reshape(G, bsz, nk, 2 * 2 * S5_STATE).transpose(2, 1, 0, 3).reshape(nk, bsz * G, 4 * S5_STATE)
    h = _s5_scan(e_t, jnp.tile(a1, (bsz, 1)), jnp.tile(a2, (bsz, 1)))
    h_in = h.reshape(nk, bsz, G, 2 * S5_STATE).transpose(2, 1, 0, 3).reshape(G, bsz * nk, 2 * S5_STATE)
    y = _s5_outputs(ug, h_in, m_intra, cx)
    return y.reshape(G, bsz, nk, L, C).transpose(1, 2, 3, 0, 4).reshape(bsz * seq, G * C)


def _merge_kernel(alpha, attn_ref, y_ref, u_ref, ga_ref, gs_ref, x_ref, aup_ref, wglu_ref, sup_ref, wo_ref,
                  dskip_ref, bglu_ref, g_ref, b_ref, o_ref, op_ref):
    a = _dot(attn_ref[...], aup_ref[...])
    y = y_ref[...] + dskip_ref[...] * u_ref[...]
    z = 0.5 * y * (1.0 + jnp.tanh(math.sqrt(2.0 / math.pi) * (y + 0.044715 * (y * y * y))))
    gate = _dot(z.astype(BF16), wglu_ref[...]) + bglu_ref[...]
    s = _dot((z * _sigmoid(gate)).astype(BF16), sup_ref[...])
    mixed = _sigmoid(ga_ref[...]) * a + _sigmoid(gs_ref[...]) * s
    r = alpha * x_ref[...] + _dot(mixed.astype(BF16), wo_ref[...])
    x1 = _layer_norm(r, g_ref[...], b_ref[...])
    o_ref[...] = x1
    op_ref[...] = _pack_bf16_pairs(x1)


def _pack_bf16_pairs(x):
    half = x.shape[1] // 2
    bits = lax.bitcast_convert_type(x.astype(BF16).astype(F32), U32)
    return (bits[:, :half] >> 16) | bits[:, half:]


def _unpack_bf16_pairs(w):
    lo = lax.bitcast_convert_type(w << 16, F32).astype(BF16)
    hi = lax.bitcast_convert_type(w & jnp.uint32(0xFFFF0000), F32).astype(BF16)
    return lo, hi


def _merge(alpha, attn, y5, u, g_a, g_s, x2, aup, wglu, sup, wo, dskip, bglu, ln_g, ln_b, tm=256):
    t, d = x2.shape
    row = lambda i: (i, 0)
    full = lambda a: pl.BlockSpec(a.shape, lambda i: (0, 0))
    return pl.pallas_call(
        functools.partial(_merge_kernel, alpha),
        grid=(t // tm,),
        in_specs=[pl.BlockSpec((tm, ATTN_WIDTH), row), pl.BlockSpec((tm, S5_WIDTH), row),
                  pl.BlockSpec((tm, S5_WIDTH), row), pl.BlockSpec((tm, d), row), pl.BlockSpec((tm, d), row),
                  pl.BlockSpec((tm, d), row), full(aup), full(wglu), full(sup), full(wo),
                  full(dskip), full(bglu), full(ln_g), full(ln_b)],
        out_specs=[pl.BlockSpec((tm, d), row), pl.BlockSpec((tm, d // 2), row)],
        out_shape=[jax.ShapeDtypeStruct((t, d), F32), jax.ShapeDtypeStruct((t, d // 2), U32)],
        compiler_params=_cparams(("parallel",)),
        name="merge_ln1",
    )(attn, y5, u, g_a, g_s, x2, aup, wglu, sup, wo, dskip, bglu, ln_g, ln_b)


def _router_kernel(x_ref, wh_ref, wl_ref, bias_ref, idx_ref, wt_ref, rank_ref, cnt_ref, carry_ref):
    tm = x_ref.shape[0]

    @pl.when(pl.program_id(0) == 0)
    def _():
        carry_ref[...] = jnp.zeros(carry_ref.shape, F32)

    x = x_ref[...]
    xh = x.astype(BF16)
    xl = (x - xh.astype(F32)).astype(BF16)
    wh = wh_ref[...]
    logits = _dot_nt(wh, xh) + _dot_nt(wh, xl) + _dot_nt(wl_ref[...], xh)
    scores = _sigmoid(logits)
    biased = scores + bias_ref[...]
    gs_rows = GROUP_SIZE
    row_g = lax.broadcasted_iota(I32, (gs_rows, tm), 0)
    group_score = []
    for g in range(N_EXPERT_GROUPS):
        vg = biased[g * gs_rows:(g + 1) * gs_rows, :]
        m1 = jnp.max(vg, axis=0, keepdims=True)
        i1 = jnp.min(jnp.where(vg == m1, row_g, gs_rows), axis=0, keepdims=True)
        m2 = jnp.max(jnp.where(row_g == i1, REMOVED, vg), axis=0, keepdims=True)
        group_score.append(m1 + m2)
    pieces = []
    for g in range(N_EXPERT_GROUPS):
        beats = jnp.zeros((1, tm), I32)
        for g2 in range(N_EXPERT_GROUPS):
            if g2 == g:
                continue
            better = group_score[g2] > group_score[g]
            if g2 < g:
                better = better | (group_score[g2] == group_score[g])
            beats = beats + better.astype(I32)
        keep = beats < TOPK_GROUPS
        pieces.append(jnp.where(keep, biased[g * gs_rows:(g + 1) * gs_rows, :], NEG_INF))
    v = jnp.concatenate(pieces, axis=0)
    row = lax.broadcasted_iota(I32, (N_EXPERTS, tm), 0)
    idxs, wts = [], []
    multi_hot = jnp.zeros((N_EXPERTS, tm), F32)
    for _ in range(TOP_K):
        m = jnp.max(v, axis=0, keepdims=True)
        ix = jnp.min(jnp.where(v == m, row, N_EXPERTS), axis=0, keepdims=True)
        hit = row == ix
        idxs.append(ix)
        wts.append(jnp.sum(jnp.where(hit, scores, 0.0), axis=0, keepdims=True))
        multi_hot = multi_hot + hit.astype(F32)
        v = jnp.where(hit, REMOVED, v)
    wsum = wts[0]
    for k in range(1, TOP_K):
        wsum = wsum + wts[k]
    s_ix = lax.broadcasted_iota(I32, (tm, tm), 0)
    t_ix = lax.broadcasted_iota(I32, (tm, tm), 1)
    mh = multi_hot.astype(BF16)
    before = _dot(mh, (s_ix < t_ix).astype(BF16)) + carry_ref[...]
    carry_ref[...] = carry_ref[...] + _dot(mh, jnp.ones((tm, tm), BF16))
    for k in range(TOP_K):
        idx_ref[k:k + 1, :] = idxs[k]
        wt_ref[k:k + 1, :] = wts[k] / wsum * ROUTED_SCALE
        rank_ref[k:k + 1, :] = jnp.sum(jnp.where(row == idxs[k], before, 0.0), axis=0, keepdims=True).astype(I32)
    cnt_ref[...] = carry_ref[:, :LANES]


def _router(x1, wr_hi, wr_lo, bias_b, tm=256):
    t, d = x1.shape
    e = wr_hi.shape[0]
    tok = lambda i: (0, i)
    const = lambda i: (0, 0)
    return pl.pallas_call(
        _router_kernel,
        grid=(t // tm,),
        in_specs=[pl.BlockSpec((tm, d), lambda i: (i, 0)), pl.BlockSpec((e, d), const),
                  pl.BlockSpec((e, d), const), pl.BlockSpec((e, tm), const)],
        out_specs=[pl.BlockSpec((TOP_K, tm), tok), pl.BlockSpec((TOP_K, tm), tok),
                   pl.BlockSpec((TOP_K, tm), tok), pl.BlockSpec((e, LANES), const)],
        out_shape=[jax.ShapeDtypeStruct((TOP_K, t), I32), jax.ShapeDtypeStruct((TOP_K, t), F32),
                   jax.ShapeDtypeStruct((TOP_K, t), I32), jax.ShapeDtypeStruct((e, LANES), F32)],
        scratch_shapes=[pltpu.VMEM((e, tm), F32)],
        compiler_params=_cparams(("arbitrary",)),
        name="router_topk_rank",
    )(x1, wr_hi, wr_lo, bias_b)


def _dest_kernel(idx_ref, rank_ref, pstart_ref, dest_ref):
    pstart = pstart_ref[...]
    row = lax.broadcasted_iota(I32, pstart.shape, 0)
    for k in range(TOP_K):
        start = jnp.sum(jnp.where(row == idx_ref[k:k + 1, :], pstart, 0.0), axis=0, keepdims=True)
        dest_ref[k:k + 1, :] = start.astype(I32) + rank_ref[k:k + 1, :]


def _dest_rows(top_idx, rank, pstart_b):
    tm = pstart_b.shape[1]
    tok = lambda i: (0, i)
    return pl.pallas_call(
        _dest_kernel,
        grid=(top_idx.shape[1] // tm,),
        in_specs=[pl.BlockSpec((TOP_K, tm), tok), pl.BlockSpec((TOP_K, tm), tok),
                  pl.BlockSpec(pstart_b.shape, lambda i: (0, 0))],
        out_specs=pl.BlockSpec((TOP_K, tm), tok),
        out_shape=jax.ShapeDtypeStruct(top_idx.shape, I32),
        compiler_params=_cparams(("parallel",)),
        name="moe_dest_rows",
    )(top_idx, rank, pstart_b)


def _rows_wait(src_hbm, dst, sem, n_rows):
    pltpu.make_async_copy(src_hbm.at[pl.ds(0, n_rows), :], dst, sem).wait()


def _dispatch_kernel(pend_ref, cnt_ref, dest_ref, x_hbm, xs_hbm, zbuf, zsem, sem):
    i = pl.program_id(0)
    n_steps = pl.num_programs(0)
    tm = dest_ref.shape[1]
    mb = zbuf.shape[0]

    @pl.when(i == 0)
    def _():
        zbuf[...] = jnp.zeros(zbuf.shape, zbuf.dtype)

        def zero_copy(e):
            start = pl.multiple_of(pend_ref[e] - mb, mb)
            return pltpu.make_async_copy(zbuf, xs_hbm.at[pl.ds(start, mb), :], zsem)

        def z_start(e, c):
            @pl.when(cnt_ref[e] > 0)
            def _():
                zero_copy(e).start()
            return c

        def z_wait(e, c):
            @pl.when(cnt_ref[e] > 0)
            def _():
                zero_copy(e).wait()
            return c

        def tail_copy(b):
            return pltpu.make_async_copy(zbuf, xs_hbm.at[pl.ds(pl.multiple_of(b * mb, mb), mb), :], zsem)

        def t_start(b, c):
            tail_copy(b).start()
            return c

        def t_wait(b, c):
            tail_copy(b).wait()
            return c

        n_used = pend_ref[N_EXPERTS - 1] // mb
        n_blocks = xs_hbm.shape[0] // mb
        lax.fori_loop(0, N_EXPERTS, z_start, 0)
        lax.fori_loop(n_used, n_blocks, t_start, 0)
        lax.fori_loop(0, N_EXPERTS, z_wait, 0)
        lax.fori_loop(n_used, n_blocks, t_wait, 0)

    def issue(t, c):
        src = x_hbm.at[pl.ds(i * tm + t, 1), :]
        for k in range(TOP_K):
            pltpu.make_async_copy(src, xs_hbm.at[pl.ds(dest_ref[k, t], 1), :], sem).start()
        return c

    lax.fori_loop(0, tm, issue, 0, unroll=2)

    def wait_one_step():
        for _ in range(TOP_K):
            _rows_wait(x_hbm, xs_hbm.at[pl.ds(0, tm), :], sem, tm)

    @pl.when(i > 0)
    def _():
        wait_one_step()

    @pl.when(i == n_steps - 1)
    def _():
        wait_one_step()


def _dispatch(padded_ends, counts, dest, x1p, n_rows, tm=512):
    t, half = x1p.shape
    grid_spec = pltpu.PrefetchScalarGridSpec(
        num_scalar_prefetch=2,
        grid=(t // tm,),
        in_specs=[pl.BlockSpec((TOP_K, tm), lambda i, pe, cn: (0, i), memory_space=pltpu.SMEM),
                  pl.BlockSpec(memory_space=pl.ANY)],
        out_specs=pl.BlockSpec(memory_space=pl.ANY),
        scratch_shapes=[pltpu.VMEM((MOE_ROWS, half), U32), pltpu.SemaphoreType.DMA(()),
                        pltpu.SemaphoreType.DMA(())],
    )
    return pl.pallas_call(
        _dispatch_kernel,
        grid_spec=grid_spec,
        out_shape=jax.ShapeDtypeStruct((n_rows, half), U32),
        compiler_params=_cparams(("arbitrary",)),
        name="moe_dispatch",
    )(padded_ends, counts, dest, x1p)


def _expert_kernel(be_ref, nused_ref, xs_ref, wg_ref, wu_ref, wd_ref, y_ref):
    @pl.when(pl.program_id(0) < nused_ref[0])
    def _():
        half = xs_ref.shape[1]
        x_lo, x_hi = _unpack_bf16_pairs(xs_ref[...])
        wg = wg_ref[0, 0].astype(BF16)
        wu = wu_ref[0, 0].astype(BF16)
        hg = _dot(x_lo, wg[:half]) + _dot(x_hi, wg[half:])
        hu = _dot(x_lo, wu[:half]) + _dot(x_hi, wu[half:])
        h = hg * _sigmoid(hg) * hu
        y_ref[...] = _dot(h.astype(BF16), wd_ref[0, 0].astype(BF16))


def _experts(layer, block_e, n_used, xs, w_gate_e, w_up_e, w_down_e):
    mb = MOE_ROWS
    nb = xs.shape[0] // mb
    _, _, d, hdim = w_gate_e.shape
    rows = lambda i, be, nu: (jnp.minimum(i, nu[0] - 1), 0)
    wmap = lambda i, be, nu: (layer, be[i], 0, 0)
    grid_spec = pltpu.PrefetchScalarGridSpec(
        num_scalar_prefetch=2,
        grid=(nb,),
        in_specs=[pl.BlockSpec((mb, d // 2), rows),
                  pl.BlockSpec((1, 1, d, hdim), wmap),
                  pl.BlockSpec((1, 1, d, hdim), wmap),
                  pl.BlockSpec((1, 1, hdim, d), wmap)],
        out_specs=pl.BlockSpec((mb, d), rows),
    )
    return pl.pallas_call(
        _expert_kernel,
        grid_spec=grid_spec,
        out_shape=jax.ShapeDtypeStruct((nb * mb, d), F32),
        compiler_params=_cparams(("arbitrary",)),
        name="routed_experts",
    )(block_e, n_used, xs, w_gate_e, w_up_e, w_down_e)


def _final_kernel(alpha, dcur_ref, dnext_ref, x_ref, wt_ref, y_hbm, wsg_ref, wsu_ref, wsd_ref, g_ref, b_ref,
                  o_ref, ybuf, sems):
    i = pl.program_id(0)
    n_steps = pl.num_programs(0)
    tm = x_ref.shape[0]
    slot = i % 2

    def gather(dref, s):
        def issue(t, c):
            for k in range(TOP_K):
                pltpu.make_async_copy(y_hbm.at[pl.ds(dref[k, t], 1), :], ybuf.at[s, k, pl.ds(t, 1), :],
                                      sems.at[s]).start()
            return c

        lax.fori_loop(0, tm, issue, 0, unroll=2)

    @pl.when(i == 0)
    def _():
        gather(dcur_ref, 0)

    @pl.when(i + 1 < n_steps)
    def _():
        gather(dnext_ref, 1 - slot)

    x = x_ref[...]
    xb = x.astype(BF16)
    hg = _dot(xb, wsg_ref[...])
    hu = _dot(xb, wsu_ref[...])
    f = _dot((hg * _sigmoid(hg) * hu).astype(BF16), wsd_ref[...])
    for k in range(TOP_K):
        _rows_wait(y_hbm, ybuf.at[slot, k], sems.at[slot], tm)
    wt = wt_ref[...]
    for k in range(TOP_K):
        f = f + wt[:, k:k + 1] * ybuf[slot, k]
    o_ref[...] = _layer_norm(alpha * x + f, g_ref[...], b_ref[...])


def _final(alpha, dest, x1, wt_tok, y_sorted, wsg, wsu, wsd, ln_g, ln_b, tm=256):
    t, d = x1.shape
    n = t // tm
    row = lambda i: (i, 0)
    full = lambda a: pl.BlockSpec(a.shape, lambda i: (0, 0))
    return pl.pallas_call(
        functools.partial(_final_kernel, alpha),
        grid=(n,),
        in_specs=[pl.BlockSpec((TOP_K, tm), lambda i: (0, i), memory_space=pltpu.SMEM),
                  pl.BlockSpec((TOP_K, tm), lambda i: (0, jnp.minimum(i + 1, n - 1)), memory_space=pltpu.SMEM),
                  pl.BlockSpec((tm, d), row), pl.BlockSpec((tm, TOP_K), row),
                  pl.BlockSpec(memory_space=pl.ANY), full(wsg), full(wsu), full(wsd), full(ln_g), full(ln_b)],
        out_specs=pl.BlockSpec((tm, d), row),
        out_shape=jax.ShapeDtypeStruct((t, d), F32),
        scratch_shapes=[pltpu.VMEM((2, TOP_K, tm, d), F32), pltpu.SemaphoreType.DMA((2,))],
        compiler_params=_cparams(("arbitrary",)),
        name="combine_shared_ln2",
    )(dest, dest, x1, wt_tok, y_sorted, wsg, wsu, wsd, ln_g, ln_b)


def _block_plan(counts, n_tok):
    mb = MOE_ROWS
    n_blocks = -(-(n_tok * TOP_K + N_EXPERTS * (mb - 1)) // mb)
    padded = (counts + mb - 1) // mb * mb
    padded_ends = jnp.cumsum(padded).astype(I32)
    padded_starts = padded_ends - padded
    block_e = jnp.minimum(jnp.searchsorted(padded_ends, jnp.arange(n_blocks, dtype=I32) * mb, side='right'),
                          N_EXPERTS - 1).astype(I32)
    n_used = (padded_ends[-1] // mb).astype(I32).reshape(1)
    return padded_starts, padded_ends, block_e, n_used, n_blocks * mb


def kernel(x, positions, w_in, attn_up, lam_re, lam_im, log_dt, b_re, b_im, c_re, c_im, d_skip, w_glu, b_glu,
           s5_up, w_o, ln1_g, ln1_b, w_router, router_bias, w_gate_e, w_up_e, w_down_e, ws_gate, ws_up, ws_down,
           ln2_g, ln2_b):
    bsz, seq, d = x.shape
    depth = w_in.shape[0]
    t = bsz * seq
    alpha = (2 * depth) ** 0.25
    assert seq % MOBA_BLOCK == 0 and seq % S5_CHUNK == 0 and t % 512 == 0

    inv_freq = ROPE_THETA ** (-jnp.arange(0, HEAD_DIM, 2, dtype=F32) / HEAD_DIM)
    ang = positions.astype(F32).reshape(t, 1) * inv_freq
    cos, sin = jnp.cos(ang), jnp.sin(ang)
    cos_t = jnp.concatenate([cos, cos, cos, cos], axis=1)
    sin_t = jnp.concatenate([-sin, sin, -sin, sin], axis=1)

    xc = x.reshape(t, d)
    row2 = lambda a: a.reshape(1, -1).astype(F32)
    for l in range(depth):
        q, k, v, u, g_a, g_s = _proj(xc, w_in[l].astype(BF16), cos_t, sin_t)
        attn = _attention(q, k, v, bsz, seq)
        tables = _s5_tables(lam_re[l], lam_im[l], log_dt[l], b_re[l], b_im[l], c_re[l], c_im[l])
        y5 = _s5_branch(u, tables, bsz, seq)
        x1, x1p = _merge(alpha, attn, y5, u, g_a, g_s, xc, attn_up[l].astype(BF16), w_glu[l].astype(BF16),
                         s5_up[l].astype(BF16), w_o[l].astype(BF16), row2(d_skip[l]), row2(b_glu[l]),
                         row2(ln1_g[l]), row2(ln1_b[l]))
        wr_t = w_router[l].astype(F32).T
        wr_hi = wr_t.astype(BF16)
        wr_lo = (wr_t - wr_hi.astype(F32)).astype(BF16)
        bias_b = jnp.broadcast_to(router_bias[l].astype(F32)[:, None], (N_EXPERTS, 256))
        top_idx, top_w, rank, counts = _router(x1, wr_hi, wr_lo, bias_b)
        counts = counts[:, 0].astype(I32)
        pstart, pend, block_e, n_used, n_rows = _block_plan(counts, t)
        dest = _dest_rows(top_idx, rank, jnp.broadcast_to(pstart.astype(F32)[:, None], (N_EXPERTS, 512)))
        xs = _dispatch(pend, counts, dest, x1p, n_rows)
        y_sorted = _experts(l, block_e, n_used, xs, w_gate_e, w_up_e, w_down_e)
        xc = _final(alpha, dest, x1, top_w.T, y_sorted, ws_gate[l].astype(BF16), ws_up[l].astype(BF16),
                    ws_down[l].astype(BF16), row2(ln2_g[l]), row2(ln2_b[l]))
    return xc.reshape(bsz, seq, d)
```

```python
import functools
import math

import jax
import jax.numpy as jnp
from jax import lax
from jax.experimental import pallas as pl
from jax.experimental.pallas import tpu as pltpu

F32 = jnp.float32
BF16 = jnp.bfloat16
I32 = jnp.int32
U32 = jnp.uint32

N_HEADS = 8
HEAD_DIM = 64
ATTN_WIDTH = N_HEADS * HEAD_DIM
MOBA_BLOCK = 256
MOBA_TOPK = 3
ROPE_THETA = 10000.0
S5_GROUP = 16
S5_GROUPS = 32
S5_WIDTH = S5_GROUP * S5_GROUPS
S5_STATE = 64
S5_CHUNK = 16
N_EXPERTS = 256
TOP_K = 8
N_EXPERT_GROUPS = 8
TOPK_GROUPS = 4
GROUP_SIZE = N_EXPERTS // N_EXPERT_GROUPS
ROUTED_SCALE = 2.5
LN_EPS = 1e-5
NEG_INF = -1e30
REMOVED = -3e38

LANES = 128
MOE_ROWS = 256
VMEM_LIMIT = 56 * 1024 * 1024


def _cparams(sem):
    return pltpu.CompilerParams(dimension_semantics=sem, vmem_limit_bytes=VMEM_LIMIT)


def _sigmoid(x):
    return 1.0 / (1.0 + jnp.exp(-x))


def _layer_norm(r, g, b):
    mu = jnp.mean(r, axis=-1, keepdims=True)
    rc = r - mu
    var = jnp.mean(rc * rc, axis=-1, keepdims=True)
    return rc * lax.rsqrt(var + LN_EPS) * g + b


def _dot(a, b):
    return jnp.dot(a, b, preferred_element_type=F32)


def _dot_nt(a, b):
    return lax.dot_general(a, b, (((1,), (1,)), ((), ())), preferred_element_type=F32)


def _proj_kernel(x_ref, w_ref, wvt_ref, cos_ref, sin_ref, q_ref, k_ref, v_ref, u_ref, ga_ref, gs_ref):
    xb = x_ref[...].astype(BF16)
    cos = cos_ref[...]
    sin = sin_ref[...]
    lane = lax.broadcasted_iota(I32, cos.shape, 1)
    first_half = (lane % HEAD_DIM) < (HEAD_DIM // 2)

    def rope(t):
        rot = jnp.where(first_half, pltpu.roll(t, LANES - HEAD_DIM // 2, axis=1),
                        pltpu.roll(t, HEAD_DIM // 2, axis=1))
        return t * cos + rot * sin

    aw = ATTN_WIDTH
    for c in range(aw // LANES):
        sl = slice(c * LANES, (c + 1) * LANES)
        qc = _dot(xb, w_ref[:, c * LANES:(c + 1) * LANES])
        q_ref[:, sl] = (rope(qc) * (HEAD_DIM ** -0.5)).astype(BF16)
        kc = _dot(xb, w_ref[:, aw + c * LANES:aw + (c + 1) * LANES])
        k_ref[:, sl] = rope(kc).astype(BF16)
    vt = _dot_nt(wvt_ref[...], xb).astype(BF16)
    for p in range(v_ref.shape[0]):
        for c in range(v_ref.shape[1]):
            v_ref[p, c] = vt[p * LANES:(p + 1) * LANES, c * MOBA_BLOCK:(c + 1) * MOBA_BLOCK]
    u_ref[...] = _dot(xb, w_ref[:, 3 * aw:3 * aw + S5_WIDTH])
    d = ga_ref.shape[1]
    off = 3 * aw + S5_WIDTH
    ga_ref[...] = _dot(xb, w_ref[:, off:off + d])
    gs_ref[...] = _dot(xb, w_ref[:, off + d:off + 2 * d])


def _proj(x2, w_bf, wvt_bf, cos_t, sin_t, tm=512):
    t, d = x2.shape
    n = w_bf.shape[1]
    aw = ATTN_WIDTH
    row = lambda i: (i, 0)
    vshape = (aw // LANES, t // MOBA_BLOCK, LANES, MOBA_BLOCK)
    return pl.pallas_call(
        _proj_kernel,
        grid=(t // tm,),
        in_specs=[pl.BlockSpec((tm, d), row),
                  pl.BlockSpec((d, n), lambda i: (0, 0)),
                  pl.BlockSpec((aw, d), lambda i: (0, 0)),
                  pl.BlockSpec((tm, LANES), row),
                  pl.BlockSpec((tm, LANES), row)],
        out_specs=[pl.BlockSpec((tm, aw), row), pl.BlockSpec((tm, aw), row),
                   pl.BlockSpec((vshape[0], tm // MOBA_BLOCK, LANES, MOBA_BLOCK), lambda i: (0, i, 0, 0)),
                   pl.BlockSpec((tm, S5_WIDTH), row), pl.BlockSpec((tm, d), row), pl.BlockSpec((tm, d), row)],
        out_shape=[jax.ShapeDtypeStruct((t, aw), BF16), jax.ShapeDtypeStruct((t, aw), BF16),
                   jax.ShapeDtypeStruct(vshape, BF16), jax.ShapeDtypeStruct((t, S5_WIDTH), F32),
                   jax.ShapeDtypeStruct((t, d), F32), jax.ShapeDtypeStruct((t, d), F32)],
        compiler_params=_cparams(("parallel",)),
        name="proj_rope",
    )(x2, w_bf, wvt_bf, cos_t, sin_t)


def _attn_kernel(q_ref, k_ref, v_ref, o_ref, kmean_ref):
    i = pl.program_id(1)
    blk = MOBA_BLOCK
    n_blocks = k_ref.shape[0] // blk
    n_pairs = q_ref.shape[1] // LANES

    @pl.when(i == 0)
    def _():
        kmean_ref[...] = jnp.zeros(kmean_ref.shape, kmean_ref.dtype)
        for j in range(n_blocks):
            kj = k_ref[j * blk:(j + 1) * blk, :].astype(F32)
            kmean_ref[j:j + 1, :] = jnp.sum(kj, axis=0, keepdims=True) * (1.0 / blk)

    q_t = q_ref[...].astype(F32).T.astype(BF16)
    sub = lax.broadcasted_iota(I32, (LANES, blk), 0)
    lane = lax.broadcasted_iota(I32, (blk, LANES), 1)
    key = lax.broadcasted_iota(I32, (blk, blk), 0)
    qry = lax.broadcasted_iota(I32, (blk, blk), 1)
    causal = key <= qry
    own = pl.ds(pl.multiple_of(i * blk, blk), blk)

    heads = [(pr, hh) for pr in range(n_pairs) for hh in range(2)]
    blk_row = lax.broadcasted_iota(I32, (HEAD_DIM, blk), 0)
    q_heads, gates, own_scores = [], [], []
    for pr, hh in heads:
        cols = slice(pr * LANES, (pr + 1) * LANES)
        q_pair = q_t[pr * LANES:(pr + 1) * LANES]
        in_head = (sub >= hh * HEAD_DIM) & (sub < (hh + 1) * HEAD_DIM)
        qh = jnp.where(in_head, q_pair, jnp.zeros_like(q_pair))
        q_heads.append(qh)
        gates.append(_dot(kmean_ref[:HEAD_DIM, cols].astype(BF16), qh))
        own_scores.append(_dot(k_ref[own, cols], qh))
    q_aug, own_probs = [], []
    for n, (pr, hh) in enumerate(heads):
        g = jnp.where(blk_row < i, gates[n], NEG_INF)
        selected = jnp.zeros(g.shape, jnp.bool_)
        for _ in range(MOBA_TOPK):
            mx = jnp.max(g, axis=0, keepdims=True)
            first = jnp.min(jnp.where(g == mx, blk_row, HEAD_DIM), axis=0, keepdims=True)
            pick = blk_row == first
            selected = selected | (pick & (blk_row < i))
            g = jnp.where(pick, REMOVED, g)
        sel_neg = jnp.where(selected, 0.0, NEG_INF).astype(BF16)
        q_own = q_heads[n][hh * HEAD_DIM:(hh + 1) * HEAD_DIM]
        q_aug.append(jnp.concatenate([q_own, sel_neg] if hh == 0 else [sel_neg, q_own], axis=0))
        s = jnp.where(causal, own_scores[n], NEG_INF)
        mx = jnp.max(s, axis=0, keepdims=True)
        p = jnp.exp(s - mx)
        own_probs.append((mx, jnp.sum(p, axis=0, keepdims=True), p.astype(BF16)))
    init = []
    for n, (pr, hh) in enumerate(heads):
        mx, l, p = own_probs[n]
        init.append((mx, l, _dot(v_ref[pr, i, hh * HEAD_DIM:(hh + 1) * HEAD_DIM, :], p)))

    def body(j, carry):
        rows = pl.ds(pl.multiple_of(j * blk, blk), blk)
        scores = []
        for n, (pr, hh) in enumerate(heads):
            k_pair = k_ref[rows, pr * LANES:(pr + 1) * LANES]
            other = (1 - hh) * HEAD_DIM
            in_other = (lane >= other) & (lane < other + HEAD_DIM)
            k_aug = jnp.where(in_other, (lane == other + j).astype(BF16), k_pair)
            scores.append(_dot(k_aug, q_aug[n]))
        probs = []
        for n in range(len(heads)):
            m, l, _ = carry[n]
            m_new = jnp.maximum(m, jnp.max(scores[n], axis=0, keepdims=True))
            p = jnp.exp(scores[n] - m_new)
            alpha = jnp.exp(m - m_new)
            probs.append((m_new, alpha, alpha * l + jnp.sum(p, axis=0, keepdims=True), p.astype(BF16)))
        out = []
        for n, (pr, hh) in enumerate(heads):
            m_new, alpha, l, p = probs[n]
            pv = _dot(v_ref[pr, j, hh * HEAD_DIM:(hh + 1) * HEAD_DIM, :], p)
            out.append((m_new, l, alpha * carry[n][2] + pv))
        return tuple(out)

    res = lax.fori_loop(0, i, body, tuple(init))
    o_t = jnp.concatenate([acc / l for _, l, acc in res], axis=0)
    o_ref[...] = o_t.T.astype(BF16)


def _attention(q, k, v_t, bsz, seq):
    t, aw = q.shape
    blk = MOBA_BLOCK
    nb = seq // blk
    assert nb <= HEAD_DIM
    qmap = lambda b, i: (b * nb + i, 0)
    return pl.pallas_call(
        _attn_kernel,
        grid=(bsz, nb),
        in_specs=[pl.BlockSpec((blk, aw), qmap),
                  pl.BlockSpec((seq, aw), lambda b, i: (b, 0)),
                  pl.BlockSpec((aw // LANES, nb, LANES, blk), lambda b, i: (0, b, 0, 0))],
        out_specs=pl.BlockSpec((blk, aw), qmap),
        out_shape=jax.ShapeDtypeStruct((t, aw), BF16),
        scratch_shapes=[pltpu.VMEM((LANES, aw), F32)],
        compiler_params=_cparams(("parallel", "arbitrary")),
        name="moba_attention",
    )(q, k, v_t)


def _s5_tables(lam_re, lam_im, log_dt, b_re, b_im, c_re, c_im):
    hp = lax.Precision.HIGHEST
    L = S5_CHUNK
    lr, li = lam_re.astype(F32), lam_im.astype(F32)
    dt = jnp.exp(log_dt.astype(F32))[:, None]
    mag = jnp.exp(lr * dt)
    a_re = mag * jnp.cos(li * dt)
    a_im = mag * jnp.sin(li * dt)
    den = lr * lr + li * li
    nr = a_re - 1.0
    coef_re = (nr * lr + a_im * li) / den
    coef_im = (a_im * lr - nr * li) / den
    br, bi = b_re.astype(F32), b_im.astype(F32)
    bbar_re = coef_re[..., None] * br - coef_im[..., None] * bi
    bbar_im = coef_re[..., None] * bi + coef_im[..., None] * br
    tau = jnp.arange(L + 1, dtype=F32)[:, None, None]
    pmag = jnp.exp(tau * (lr * dt))
    p_re = pmag * jnp.cos(tau * (li * dt))
    p_im = pmag * jnp.sin(tau * (li * dt))
    cr, ci = c_re.astype(F32), c_im.astype(F32)
    ca_re = cr[None] * p_re[:, :, None, :] - ci[None] * p_im[:, :, None, :]
    ca_im = cr[None] * p_im[:, :, None, :] + ci[None] * p_re[:, :, None, :]
    kmat = (jnp.einsum('tgon,gni->tgio', ca_re[:L], bbar_re, precision=hp)
            - jnp.einsum('tgon,gni->tgio', ca_im[:L], bbar_im, precision=hp))
    s_ix = jnp.arange(L)[:, None]
    t_ix = jnp.arange(L)[None, :]
    lag = jnp.clip(t_ix - s_ix, 0, L - 1)
    m = kmat[lag]
    m = jnp.where((t_ix >= s_ix)[:, :, None, None, None], m, 0.0)
    m_intra = m.transpose(2, 0, 3, 1, 4).reshape(S5_GROUPS, L * S5_GROUP, L * S5_GROUP)
    rev = p_re[L - 1 - jnp.arange(L)], p_im[L - 1 - jnp.arange(L)]
    w_re = rev[0][..., None] * bbar_re[None] - rev[1][..., None] * bbar_im[None]
    w_im = rev[0][..., None] * bbar_im[None] + rev[1][..., None] * bbar_re[None]
    w_re = w_re.transpose(1, 0, 3, 2).reshape(S5_GROUPS, L * S5_GROUP, S5_STATE)
    w_im = w_im.transpose(1, 0, 3, 2).reshape(S5_GROUPS, L * S5_GROUP, S5_STATE)
    w_state = jnp.concatenate([w_re, w_im, w_im, w_re], axis=-1)
    cx_re = ca_re[1:].transpose(1, 3, 0, 2).reshape(S5_GROUPS, S5_STATE, L * S5_GROUP)
    cx_im = ca_im[1:].transpose(1, 3, 0, 2).reshape(S5_GROUPS, S5_STATE, L * S5_GROUP)
    cx = jnp.concatenate([cx_re, -cx_im], axis=1)
    al_re, al_im = p_re[L], p_im[L]
    a1 = jnp.concatenate([al_re, al_re], axis=-1)
    a2 = jnp.concatenate([-al_im, al_im], axis=-1)
    return m_intra.astype(BF16), w_state.astype(BF16), cx.astype(BF16), a1, a2


def _s5_state_kernel(u_ref, w_ref, e_ref):
    e_ref[0] = _dot(u_ref[0], w_ref[0])


def _s5_chunk_states(ug, w_state):
    g, r, c = ug.shape
    spec = lambda shape: pl.BlockSpec(shape, lambda i: (i, 0, 0))
    return pl.pallas_call(
        _s5_state_kernel,
        grid=(g,),
        in_specs=[spec((1, r, c)), spec((1, c, c))],
        out_specs=spec((1, r, c)),
        out_shape=jax.ShapeDtypeStruct((g, r, c), F32),
        compiler_params=_cparams(("parallel",)),
        name="s5_chunk_states",
    )(ug, w_state)


def _s5_scan_kernel(e_ref, a1_ref, a2_ref, h_ref, hp_ref, hq_ref):
    @pl.when(pl.program_id(0) == 0)
    def _():
        hp_ref[...] = jnp.zeros(hp_ref.shape, F32)
        hq_ref[...] = jnp.zeros(hq_ref.shape, F32)

    a1 = a1_ref[...]
    a2 = a2_ref[...]
    half = hp_ref.shape[1]

    def step(k, carry):
        hp, hq = carry
        h_ref[k] = hp
        e = e_ref[k]
        return (a1 * hp + a2 * hq + e[:, :half], a1 * hq - a2 * hp + e[:, half:])

    hp, hq = lax.fori_loop(0, e_ref.shape[0], step, (hp_ref[...], hq_ref[...]))
    hp_ref[...] = hp
    hq_ref[...] = hq


def _s5_scan(e_t, a1, a2, kb=64):
    nk, bg, c = e_t.shape
    half = c // 2
    kb = min(kb, nk)
    return pl.pallas_call(
        _s5_scan_kernel,
        grid=(nk // kb,),
        in_specs=[pl.BlockSpec((kb, bg, c), lambda i: (i, 0, 0)),
                  pl.BlockSpec((bg, half), lambda i: (0, 0)),
                  pl.BlockSpec((bg, half), lambda i: (0, 0))],
        out_specs=pl.BlockSpec((kb, bg, half), lambda i: (i, 0, 0)),
        out_shape=jax.ShapeDtypeStruct((nk, bg, half), F32),
        scratch_shapes=[pltpu.VMEM((bg, half), F32), pltpu.VMEM((bg, half), F32)],
        compiler_params=_cparams(("arbitrary",)),
        name="s5_scan",
    )(e_t, a1, a2)


def _s5_out_kernel(u_ref, h_ref, m_ref, cx_ref, y_ref):
    y_ref[0] = _dot(u_ref[0], m_ref[0]) + _dot(h_ref[0].astype(BF16), cx_ref[0])


def _s5_outputs(ug, h_in, m_intra, cx):
    g, r, c = ug.shape
    n2 = h_in.shape[2]
    spec = lambda shape: pl.BlockSpec(shape, lambda i: (i, 0, 0))
    return pl.pallas_call(
        _s5_out_kernel,
        grid=(g,),
        in_specs=[spec((1, r, c)), spec((1, r, n2)), spec((1, c, c)), spec((1, n2, c))],
        out_specs=spec((1, r, c)),
        out_shape=jax.ShapeDtypeStruct((g, r, c), F32),
        compiler_params=_cparams(("parallel",)),
        name="s5_outputs",
    )(ug, h_in, m_intra, cx)


def _s5_branch(u, tables, bsz, seq):
    m_intra, w_state, cx, a1, a2 = tables
    L, G, C = S5_CHUNK, S5_GROUPS, S5_GROUP
    nk = seq // L
    ug = u.reshape(bsz, nk, L, G, C).transpose(3, 0, 1, 2, 4).reshape(G, bsz * nk, L * C).astype(BF16)
    e = _s5_chunk_states(ug, w_state)
    e_t = e.reshape(G, bsz, nk, 2 * 2 * S5_STATE).transpose(2, 1, 0, 3).reshape(nk, bsz * G, 4 * S5_STATE)
    h = _s5_scan(e_t, jnp.tile(a1, (bsz, 1)), jnp.tile(a2, (bsz, 1)))
    h_in = h.reshape(nk, bsz, G, 2 * S5_STATE).transpose(2, 1, 0, 3).reshape(G, bsz * nk, 2 * S5_STATE)
    y = _s5_outputs(ug, h_in, m_intra, cx)
    return y.reshape(G, bsz, nk, L, C).transpose(1, 2, 3, 0, 4).reshape(bsz * seq, G * C)


def _merge_kernel(alpha, attn_ref, y_ref, u_ref, ga_ref, gs_ref, x_ref, aup_ref, wglu_ref, sup_ref, wo_ref,
                  dskip_ref, bglu_ref, g_ref, b_ref, o_ref, op_ref):
    a = _dot(attn_ref[...], aup_ref[...])
    y = y_ref[...] + dskip_ref[...] * u_ref[...]
    z = 0.5 * y * (1.0 + jnp.tanh(math.sqrt(2.0 / math.pi) * (y + 0.044715 * (y * y * y))))
    gate = _dot(z.astype(BF16), wglu_ref[...]) + bglu_ref[...]
    s = _dot((z * _sigmoid(gate)).astype(BF16), sup_ref[...])
    mixed = _sigmoid(ga_ref[...]) * a + _sigmoid(gs_ref[...]) * s
    r = alpha * x_ref[...] + _dot(mixed.astype(BF16), wo_ref[...])
    x1 = _layer_norm(r, g_ref[...], b_ref[...])
    o_ref[...] = x1
    op_ref[...] = _pack_bf16_pairs(x1)


def _pack_bf16_pairs(x):
    half = x.shape[1] // 2
    bits = lax.bitcast_convert_type(x.astype(BF16).astype(F32), U32)
    return (bits[:, :half] >> 16) | bits[:, half:]


def _unpack_bf16_pairs(w):
    lo = lax.bitcast_convert_type(w << 16, F32).astype(BF16)
    hi = lax.bitcast_convert_type(w & jnp.uint32(0xFFFF0000), F32).astype(BF16)
    return lo, hi


def _merge(alpha, attn, y5, u, g_a, g_s, x2, aup, wglu, sup, wo, dskip, bglu, ln_g, ln_b, tm=256):
    t, d = x2.shape
    row = lambda i: (i, 0)
    full = lambda a: pl.BlockSpec(a.shape, lambda i: (0, 0))
    return pl.pallas_call(
        functools.partial(_merge_kernel, alpha),
        grid=(t // tm,),
        in_specs=[pl.BlockSpec((tm, ATTN_WIDTH), row), pl.BlockSpec((tm, S5_WIDTH), row),
                  pl.BlockSpec((tm, S5_WIDTH), row), pl.BlockSpec((tm, d), row), pl.BlockSpec((tm, d), row),
                  pl.BlockSpec((tm, d), row), full(aup), full(wglu), full(sup), full(wo),
                  full(dskip), full(bglu), full(ln_g), full(ln_b)],
        out_specs=[pl.BlockSpec((tm, d), row), pl.BlockSpec((tm, d // 2), row)],
        out_shape=[jax.ShapeDtypeStruct((t, d), F32), jax.ShapeDtypeStruct((t, d // 2), U32)],
        compiler_params=_cparams(("parallel",)),
        name="merge_ln1",
    )(attn, y5, u, g_a, g_s, x2, aup, wglu, sup, wo, dskip, bglu, ln_g, ln_b)


def _router_kernel(x_ref, wh_ref, wl_ref, bias_ref, idx_ref, wt_ref, rank_ref, cnt_ref, carry_ref):
    tm = x_ref.shape[0]

    @pl.when(pl.program_id(0) == 0)
    def _():
        carry_ref[...] = jnp.zeros(carry_ref.shape, F32)

    x = x_ref[...]
    xh = x.astype(BF16)
    xl = (x - xh.astype(F32)).astype(BF16)
    wh = wh_ref[...]
    logits = _dot_nt(wh, xh) + _dot_nt(wh, xl) + _dot_nt(wl_ref[...], xh)
    scores = _sigmoid(logits)
    biased = scores + bias_ref[...]
    gs_rows = GROUP_SIZE
    row_g = lax.broadcasted_iota(I32, (gs_rows, tm), 0)
    group_score = []
    for g in range(N_EXPERT_GROUPS):
        vg = biased[g * gs_rows:(g + 1) * gs_rows, :]
        m1 = jnp.max(vg, axis=0, keepdims=True)
        i1 = jnp.min(jnp.where(vg == m1, row_g, gs_rows), axis=0, keepdims=True)
        m2 = jnp.max(jnp.where(row_g == i1, REMOVED, vg), axis=0, keepdims=True)
        group_score.append(m1 + m2)
    pieces = []
    for g in range(N_EXPERT_GROUPS):
        beats = jnp.zeros((1, tm), I32)
        for g2 in range(N_EXPERT_GROUPS):
            if g2 == g:
                continue
            better = group_score[g2] > group_score[g]
            if g2 < g:
                better = better | (group_score[g2] == group_score[g])
            beats = beats + better.astype(I32)
        keep = beats < TOPK_GROUPS
        pieces.append(jnp.where(keep, biased[g * gs_rows:(g + 1) * gs_rows, :], NEG_INF))
    v = jnp.concatenate(pieces, axis=0)
    row = lax.broadcasted_iota(I32, (N_EXPERTS, tm), 0)
    idxs, wts = [], []
    multi_hot = jnp.zeros((N_EXPERTS, tm), F32)
    for _ in range(TOP_K):
        m = jnp.max(v, axis=0, keepdims=True)
        ix = jnp.min(jnp.where(v == m, row, N_EXPERTS), axis=0, keepdims=True)
        hit = row == ix
        idxs.append(ix)
        wts.append(jnp.sum(jnp.where(hit, scores, 0.0), axis=0, keepdims=True))
        multi_hot = multi_hot + hit.astype(F32)
        v = jnp.where(hit, REMOVED, v)
    wsum = wts[0]
    for k in range(1, TOP_K):
        wsum = wsum + wts[k]
    s_ix = lax.broadcasted_iota(I32, (tm, tm), 0)
    t_ix = lax.broadcasted_iota(I32, (tm, tm), 1)
    mh = multi_hot.astype(BF16)
    before = _dot(mh, (s_ix < t_ix).astype(BF16)) + carry_ref[...]
    carry_ref[...] = carry_ref[...] + _dot(mh, jnp.ones((tm, tm), BF16))
    for k in range(TOP_K):
        idx_ref[k:k + 1, :] = idxs[k]
        wt_ref[k:k + 1, :] = wts[k] / wsum * ROUTED_SCALE
        rank_ref[k:k + 1, :] = jnp.sum(jnp.where(row == idxs[k], before, 0.0), axis=0, keepdims=True).astype(I32)
    cnt_ref[...] = carry_ref[:, :LANES]


def _router(x1, wr_hi, wr_lo, bias_b, tm=256):
    t, d = x1.shape
    e = wr_hi.shape[0]
    tok = lambda i: (0, i)
    const = lambda i: (0, 0)
    return pl.pallas_call(
        _router_kernel,
        grid=(t // tm,),
        in_specs=[pl.BlockSpec((tm, d), lambda i: (i, 0)), pl.BlockSpec((e, d), const),
                  pl.BlockSpec((e, d), const), pl.BlockSpec((e, tm), const)],
        out_specs=[pl.BlockSpec((TOP_K, tm), tok), pl.BlockSpec((TOP_K, tm), tok),
                   pl.BlockSpec((TOP_K, tm), tok), pl.BlockSpec((e, LANES), const)],
        out_shape=[jax.ShapeDtypeStruct((TOP_K, t), I32), jax.ShapeDtypeStruct((TOP_K, t), F32),
                   jax.ShapeDtypeStruct((TOP_K, t), I32), jax.ShapeDtypeStruct((e, LANES), F32)],
        scratch_shapes=[pltpu.VMEM((e, tm), F32)],
        compiler_params=_cparams(("arbitrary",)),
        name="router_topk_rank",
    )(x1, wr_hi, wr_lo, bias_b)


def _dest_kernel(idx_ref, rank_ref, pstart_ref, dest_ref):
    pstart = pstart_ref[...]
    row = lax.broadcasted_iota(I32, pstart.shape, 0)
    for k in range(TOP_K):
        start = jnp.sum(jnp.where(row == idx_ref[k:k + 1, :], pstart, 0.0), axis=0, keepdims=True)
        dest_ref[k:k + 1, :] = start.astype(I32) + rank_ref[k:k + 1, :]


def _dest_rows(top_idx, rank, pstart_b):
    tm = pstart_b.shape[1]
    tok = lambda i: (0, i)
    return pl.pallas_call(
        _dest_kernel,
        grid=(top_idx.shape[1] // tm,),
        in_specs=[pl.BlockSpec((TOP_K, tm), tok), pl.BlockSpec((TOP_K, tm), tok),
                  pl.BlockSpec(pstart_b.shape, lambda i: (0, 0))],
        out_specs=pl.BlockSpec((TOP_K, tm), tok),
        out_shape=jax.ShapeDtypeStruct(top_idx.shape, I32),
        compiler_params=_cparams(("parallel",)),
        name="moe_dest_rows",
    )(top_idx, rank, pstart_b)


def _rows_wait(src_hbm, dst, sem, n_rows):
    pltpu.make_async_copy(src_hbm.at[pl.ds(0, n_rows), :], dst, sem).wait()


def _dispatch_kernel(pend_ref, cnt_ref, dest_ref, x_ref, xs_hbm, zbuf, zsem, sem):
    i = pl.program_id(0)
    tm = dest_ref.shape[1]
    mb = zbuf.shape[0]

    @pl.when(i == 0)
    def _():
        zbuf[...] = jnp.zeros(zbuf.shape, zbuf.dtype)

        def zero_copy(e):
            start = pl.multiple_of(pend_ref[e] - mb, mb)
            return pltpu.make_async_copy(zbuf, xs_hbm.at[pl.ds(start, mb), :], zsem)

        def z_start(e, c):
            @pl.when(cnt_ref[e] > 0)
            def _():
                zero_copy(e).start()
            return c

        def z_wait(e, c):
            @pl.when(cnt_ref[e] > 0)
            def _():
                zero_copy(e).wait()
            return c

        def tail_copy(b):
            return pltpu.make_async_copy(zbuf, xs_hbm.at[pl.ds(pl.multiple_of(b * mb, mb), mb), :], zsem)

        def t_start(b, c):
            tail_copy(b).start()
            return c

        def t_wait(b, c):
            tail_copy(b).wait()
            return c

        n_used = pend_ref[N_EXPERTS - 1] // mb
        n_blocks = xs_hbm.shape[0] // mb
        lax.fori_loop(0, N_EXPERTS, z_start, 0)
        lax.fori_loop(n_used, n_blocks, t_start, 0)
        lax.fori_loop(0, N_EXPERTS, z_wait, 0)
        lax.fori_loop(n_used, n_blocks, t_wait, 0)

    def issue(t, c):
        src = x_ref.at[pl.ds(t, 1), :]
        for k in range(TOP_K):
            pltpu.make_async_copy(src, xs_hbm.at[pl.ds(dest_ref[k, t], 1), :], sem).start()
        return c

    lax.fori_loop(0, tm, issue, 0, unroll=2)
    for _ in range(TOP_K):
        pltpu.make_async_copy(x_ref, xs_hbm.at[pl.ds(0, tm), :], sem).wait()


def _dispatch(padded_ends, counts, dest, x1p, n_rows, tm=512):
    t, half = x1p.shape
    grid_spec = pltpu.PrefetchScalarGridSpec(
        num_scalar_prefetch=2,
        grid=(t // tm,),
        in_specs=[pl.BlockSpec((TOP_K, tm), lambda i, pe, cn: (0, i), memory_space=pltpu.SMEM),
                  pl.BlockSpec((tm, half), lambda i, pe, cn: (i, 0))],
        out_specs=pl.BlockSpec(memory_space=pl.ANY),
        scratch_shapes=[pltpu.VMEM((MOE_ROWS, half), U32), pltpu.SemaphoreType.DMA(()),
                        pltpu.SemaphoreType.DMA(())],
    )
    return pl.pallas_call(
        _dispatch_kernel,
        grid_spec=grid_spec,
        out_shape=jax.ShapeDtypeStruct((n_rows, half), U32),
        compiler_params=_cparams(("arbitrary",)),
        name="moe_dispatch",
    )(padded_ends, counts, dest, x1p)


def _expert_kernel(be_ref, nused_ref, xs_ref, wg_ref, wu_ref, wd_ref, y_ref):
    @pl.when(pl.program_id(0) < nused_ref[0])
    def _():
        half = xs_ref.shape[1]
        x_lo, x_hi = _unpack_bf16_pairs(xs_ref[...])
        wg = wg_ref[0, 0].astype(BF16)
        wu = wu_ref[0, 0].astype(BF16)
        hg = _dot(x_lo, wg[:half]) + _dot(x_hi, wg[half:])
        hu = _dot(x_lo, wu[:half]) + _dot(x_hi, wu[half:])
        h = hg * _sigmoid(hg) * hu
        y_ref[...] = _dot(h.astype(BF16), wd_ref[0, 0].astype(BF16))

    @pl.when(pl.program_id(0) >= nused_ref[0])
    def _():
        y_ref[...] = jnp.zeros(y_ref.shape, y_ref.dtype)


def _experts(layer, block_e, n_used, xs, w_gate_e, w_up_e, w_down_e):
    mb = MOE_ROWS
    nb = xs.shape[0] // mb
    _, _, d, hdim = w_gate_e.shape
    rows = lambda i, be, nu: (jnp.minimum(i, nu[0] - 1), 0)
    wmap = lambda i, be, nu: (layer, be[i], 0, 0)
    grid_spec = pltpu.PrefetchScalarGridSpec(
        num_scalar_prefetch=2,
        grid=(nb,),
        in_specs=[pl.BlockSpec((mb, d // 2), rows),
                  pl.BlockSpec((1, 1, d, hdim), wmap),
                  pl.BlockSpec((1, 1, d, hdim), wmap),
                  pl.BlockSpec((1, 1, hdim, d), wmap)],
        out_specs=pl.BlockSpec((mb, d), lambda i, be, nu: (i, 0)),
    )
    return pl.pallas_call(
        _expert_kernel,
        grid_spec=grid_spec,
        out_shape=jax.ShapeDtypeStruct((nb * mb, d), F32),
        compiler_params=_cparams(("arbitrary",)),
        name="routed_experts",
    )(block_e, n_used, xs, w_gate_e, w_up_e, w_down_e)


def _final_kernel(alpha, dcur_ref, dnext_ref, x_ref, wt_ref, y_hbm, wsg_ref, wsu_ref, wsd_ref, g_ref, b_ref,
                  o_ref, ybuf, sems):
    i = pl.program_id(0)
    n_steps = pl.num_programs(0)
    tm = x_ref.shape[0]
    slot = i % 2

    def gather(dref, s):
        def issue(t, c):
            for k in range(TOP_K):
                pltpu.make_async_copy(y_hbm.at[pl.ds(dref[k, t], 1), :], ybuf.at[s, k, pl.ds(t, 1), :],
                                      sems.at[s]).start()
            return c

        lax.fori_loop(0, tm, issue, 0, unroll=2)

    @pl.when(i == 0)
    def _():
        gather(dcur_ref, 0)

    @pl.when(i + 1 < n_steps)
    def _():
        gather(dnext_ref, 1 - slot)

    x = x_ref[...]
    xb = x.astype(BF16)
    hg = _dot(xb, wsg_ref[...])
    hu = _dot(xb, wsu_ref[...])
    f = _dot((hg * _sigmoid(hg) * hu).astype(BF16), wsd_ref[...])
    for k in range(TOP_K):
        _rows_wait(y_hbm, ybuf.at[slot, k], sems.at[slot], tm)
    wt = wt_ref[...]
    for k in range(TOP_K):
        f = f + wt[:, k:k + 1] * ybuf[slot, k]
    o_ref[...] = _layer_norm(alpha * x + f, g_ref[...], b_ref[...])


def _final(alpha, dest, x1, wt_tok, y_sorted, wsg, wsu, wsd, ln_g, ln_b, tm=256):
    t, d = x1.shape
    n = t // tm
    row = lambda i: (i, 0)
    full = lambda a: pl.BlockSpec(a.shape, lambda i: (0, 0))
    return pl.pallas_call(
        functools.partial(_final_kernel, alpha),
        grid=(n,),
        in_specs=[pl.BlockSpec((TOP_K, tm), lambda i: (0, i), memory_space=pltpu.SMEM),
                  pl.BlockSpec((TOP_K, tm), lambda i: (0, jnp.minimum(i + 1, n - 1)), memory_space=pltpu.SMEM),
                  pl.BlockSpec((tm, d), row), pl.BlockSpec((tm, TOP_K), row),
                  pl.BlockSpec(memory_space=pl.ANY), full(wsg), full(wsu), full(wsd), full(ln_g), full(ln_b)],
        out_specs=pl.BlockSpec((tm, d), row),
        out_shape=jax.ShapeDtypeStruct((t, d), F32),
        scratch_shapes=[pltpu.VMEM((2, TOP_K, tm, d), F32), pltpu.SemaphoreType.DMA((2,))],
        compiler_params=_cparams(("arbitrary",)),
        name="combine_shared_ln2",
    )(dest, dest, x1, wt_tok, y_sorted, wsg, wsu, wsd, ln_g, ln_b)


def _block_plan(counts, n_tok):
    mb = MOE_ROWS
    n_blocks = -(-(n_tok * TOP_K + N_EXPERTS * (mb - 1)) // mb)
    padded = (counts + mb - 1) // mb * mb
    padded_ends = jnp.cumsum(padded).astype(I32)
    padded_starts = padded_ends - padded
    block_e = jnp.minimum(jnp.searchsorted(padded_ends, jnp.arange(n_blocks, dtype=I32) * mb, side='right'),
                          N_EXPERTS - 1).astype(I32)
    n_used = (padded_ends[-1] // mb).astype(I32).reshape(1)
    return padded_starts, padded_ends, block_e, n_used, n_blocks * mb


def kernel(x, positions, w_in, attn_up, lam_re, lam_im, log_dt, b_re, b_im, c_re, c_im, d_skip, w_glu, b_glu,
           s5_up, w_o, ln1_g, ln1_b, w_router, router_bias, w_gate_e, w_up_e, w_down_e, ws_gate, ws_up, ws_down,
           ln2_g, ln2_b):
    bsz, seq, d = x.shape
    depth = w_in.shape[0]
    t = bsz * seq
    alpha = (2 * depth) ** 0.25
    assert seq % MOBA_BLOCK == 0 and seq % S5_CHUNK == 0 and t % 512 == 0

    inv_freq = ROPE_THETA ** (-jnp.arange(0, HEAD_DIM, 2, dtype=F32) / HEAD_DIM)
    ang = positions.astype(F32).reshape(t, 1) * inv_freq
    cos, sin = jnp.cos(ang), jnp.sin(ang)
    cos_t = jnp.concatenate([cos, cos, cos, cos], axis=1)
    sin_t = jnp.concatenate([-sin, sin, -sin, sin], axis=1)

    xc = x.reshape(t, d)
    row2 = lambda a: a.reshape(1, -1).astype(F32)
    for l in range(depth):
        w_v_t = w_in[l][:, 2 * ATTN_WIDTH:3 * ATTN_WIDTH].T.astype(BF16)
        q, k, v_t, u, g_a, g_s = _proj(xc, w_in[l].astype(BF16), w_v_t, cos_t, sin_t)
        attn = _attention(q, k, v_t, bsz, seq)
        tables = _s5_tables(lam_re[l], lam_im[l], log_dt[l], b_re[l], b_im[l], c_re[l], c_im[l])
        y5 = _s5_branch(u, tables, bsz, seq)
        x1, x1p = _merge(alpha, attn, y5, u, g_a, g_s, xc, attn_up[l].astype(BF16), w_glu[l].astype(BF16),
                         s5_up[l].astype(BF16), w_o[l].astype(BF16), row2(d_skip[l]), row2(b_glu[l]),
                         row2(ln1_g[l]), row2(ln1_b[l]))
        wr_t = w_router[l].astype(F32).T
        wr_hi = wr_t.astype(BF16)
        wr_lo = (wr_t - wr_hi.astype(F32)).astype(BF16)
        bias_b = jnp.broadcast_to(router_bias[l].astype(F32)[:, None], (N_EXPERTS, 256))
        top_idx, top_w, rank, counts = _router(x1, wr_hi, wr_lo, bias_b)
        counts = counts[:, 0].astype(I32)
        pstart, pend, block_e, n_used, n_rows = _block_plan(counts, t)
        dest = _dest_rows(top_idx, rank, jnp.broadcast_to(pstart.astype(F32)[:, None], (N_EXPERTS, 512)))
        xs = _dispatch(pend, counts, dest, x1p, n_rows)
        y_sorted = _experts(l, block_e, n_used, xs, w_gate_e, w_up_e, w_down_e)
        xc = _final(alpha, dest, x1, top_w.T, y_sorted, ws_gate[l].astype(BF16), ws_up[l].astype(BF16),
                    ws_down[l].astype(BF16), row2(ln2_g[l]), row2(ln2_b[l]))
    return xc.reshape(bsz, seq, d)
```

```python
import functools
import math

import jax
import jax.numpy as jnp
from jax import lax
from jax.experimental import pallas as pl
from jax.experimental.pallas import tpu as pltpu

F32 = jnp.float32
BF16 = jnp.bfloat16
I32 = jnp.int32

N_HEADS = 8
HEAD_DIM = 64
ATTN_WIDTH = N_HEADS * HEAD_DIM
MOBA_BLOCK = 256
MOBA_TOPK = 3
ROPE_THETA = 10000.0
S5_GROUP = 16
S5_GROUPS = 32
S5_WIDTH = S5_GROUP * S5_GROUPS
S5_STATE = 64
S5_CHUNK = 16
N_EXPERTS = 256
TOP_K = 8
N_EXPERT_GROUPS = 8
TOPK_GROUPS = 4
GROUP_SIZE = N_EXPERTS // N_EXPERT_GROUPS
ROUTED_SCALE = 2.5
LN_EPS = 1e-5
NEG_INF = -1e30
REMOVED = -3e38
QK_SCALE = HEAD_DIM ** -0.5 * math.log2(math.e)
V_ROWS = HEAD_DIM + 16

LANES = 128
MOE_ROWS = 256
VMEM_LIMIT = 56 * 1024 * 1024


def _cparams(sem):
    return pltpu.CompilerParams(dimension_semantics=sem, vmem_limit_bytes=VMEM_LIMIT)


def _sigmoid(x):
    return 1.0 / (1.0 + jnp.exp(-x))


def _layer_norm(r, g, b):
    mu = jnp.mean(r, axis=-1, keepdims=True)
    rc = r - mu
    var = jnp.mean(rc * rc, axis=-1, keepdims=True)
    return rc * lax.rsqrt(var + LN_EPS) * g + b


def _dot(a, b):
    return jnp.dot(a, b, preferred_element_type=F32)


def _dot_nt(a, b):
    return lax.dot_general(a, b, (((1,), (1,)), ((), ())), preferred_element_type=F32)


def _proj_kernel(x_ref, w_ref, wvt_ref, cos_ref, sin_ref, q_ref, k_ref, v_ref, u_ref, ga_ref, gs_ref):
    xb = x_ref[...].astype(BF16)
    cos = cos_ref[...]
    sin = sin_ref[...]
    lane = lax.broadcasted_iota(I32, cos.shape, 1)
    first_half = (lane % HEAD_DIM) < (HEAD_DIM // 2)

    def rope(t):
        rot = jnp.where(first_half, pltpu.roll(t, LANES - HEAD_DIM // 2, axis=1),
                        pltpu.roll(t, HEAD_DIM // 2, axis=1))
        return t * cos + rot * sin

    aw = ATTN_WIDTH
    for c in range(aw // LANES):
        sl = slice(c * LANES, (c + 1) * LANES)
        qc = _dot(xb, w_ref[:, c * LANES:(c + 1) * LANES])
        q_ref[:, sl] = (rope(qc) * QK_SCALE).astype(BF16)
        kc = _dot(xb, w_ref[:, aw + c * LANES:aw + (c + 1) * LANES])
        k_ref[:, sl] = rope(kc).astype(BF16)
    vt = _dot_nt(wvt_ref[...], xb).astype(BF16)
    ones = jnp.ones((V_ROWS - HEAD_DIM, MOBA_BLOCK), BF16)
    for p in range(v_ref.shape[0]):
        for c in range(v_ref.shape[1]):
            cols = slice(c * MOBA_BLOCK, (c + 1) * MOBA_BLOCK)
            heads = [vt[(2 * p + hh) * HEAD_DIM:(2 * p + hh + 1) * HEAD_DIM, cols] for hh in range(2)]
            v_ref[p, c] = jnp.concatenate([heads[0], ones, heads[1], ones], axis=0)
    u_ref[...] = _dot(xb, w_ref[:, 3 * aw:3 * aw + S5_WIDTH])
    d = ga_ref.shape[1]
    off = 3 * aw + S5_WIDTH
    ga_ref[...] = _dot(xb, w_ref[:, off:off + d])
    gs_ref[...] = _dot(xb, w_ref[:, off + d:off + 2 * d])


def _proj(x2, w_bf, wvt_bf, cos_t, sin_t, tm=512):
    t, d = x2.shape
    n = w_bf.shape[1]
    aw = ATTN_WIDTH
    row = lambda i: (i, 0)
    vshape = (aw // LANES, t // MOBA_BLOCK, 2 * V_ROWS, MOBA_BLOCK)
    return pl.pallas_call(
        _proj_kernel,
        grid=(t // tm,),
        in_specs=[pl.BlockSpec((tm, d), row),
                  pl.BlockSpec((d, n), lambda i: (0, 0)),
                  pl.BlockSpec((aw, d), lambda i: (0, 0)),
                  pl.BlockSpec((tm, LANES), row),
                  pl.BlockSpec((tm, LANES), row)],
        out_specs=[pl.BlockSpec((tm, aw), row), pl.BlockSpec((tm, aw), row),
                   pl.BlockSpec((vshape[0], tm // MOBA_BLOCK) + vshape[2:], lambda i: (0, i, 0, 0)),
                   pl.BlockSpec((tm, S5_WIDTH), row), pl.BlockSpec((tm, d), row), pl.BlockSpec((tm, d), row)],
        out_shape=[jax.ShapeDtypeStruct((t, aw), BF16), jax.ShapeDtypeStruct((t, aw), BF16),
                   jax.ShapeDtypeStruct(vshape, BF16), jax.ShapeDtypeStruct((t, S5_WIDTH), F32),
                   jax.ShapeDtypeStruct((t, d), F32), jax.ShapeDtypeStruct((t, d), F32)],
        compiler_params=_cparams(("parallel",)),
        name="proj_rope",
    )(x2, w_bf, wvt_bf, cos_t, sin_t)


def _attn_kernel(q_ref, k_ref, v_ref, o_ref, kmean_ref):
    i = pl.program_id(1)
    blk = MOBA_BLOCK
    n_blocks = k_ref.shape[0] // blk
    n_pairs = q_ref.shape[1] // LANES

    @pl.when(i == 0)
    def _():
        kmean_ref[...] = jnp.zeros(kmean_ref.shape, kmean_ref.dtype)
        for j in range(n_blocks):
            kj = k_ref[j * blk:(j + 1) * blk, :].astype(F32)
            kmean_ref[j:j + 1, :] = jnp.sum(kj, axis=0, keepdims=True) * (1.0 / blk)

    q_t = q_ref[...].astype(F32).T.astype(BF16)
    sub = lax.broadcasted_iota(I32, (LANES, blk), 0)
    lane = lax.broadcasted_iota(I32, (blk, LANES), 1)
    key = lax.broadcasted_iota(I32, (blk, blk), 0)
    qry = lax.broadcasted_iota(I32, (blk, blk), 1)
    causal = key <= qry
    own = pl.ds(pl.multiple_of(i * blk, blk), blk)

    heads = [(pr, hh) for pr in range(n_pairs) for hh in range(2)]
    blk_row = lax.broadcasted_iota(I32, (HEAD_DIM, blk), 0)
    q_heads, gates, own_scores = [], [], []
    for pr, hh in heads:
        cols = slice(pr * LANES, (pr + 1) * LANES)
        q_pair = q_t[pr * LANES:(pr + 1) * LANES]
        in_head = (sub >= hh * HEAD_DIM) & (sub < (hh + 1) * HEAD_DIM)
        qh = jnp.where(in_head, q_pair, jnp.zeros_like(q_pair))
        q_heads.append(qh)
        gates.append(_dot(kmean_ref[:HEAD_DIM, cols].astype(BF16), qh))
        own_scores.append(_dot(k_ref[own, cols], qh))
    q_aug, own_probs = [], []
    for n, (pr, hh) in enumerate(heads):
        g = jnp.where(blk_row < i, gates[n], NEG_INF)
        selected = jnp.zeros(g.shape, jnp.bool_)
        for _ in range(MOBA_TOPK):
            mx = jnp.max(g, axis=0, keepdims=True)
            first = jnp.min(jnp.where(g == mx, blk_row, HEAD_DIM), axis=0, keepdims=True)
            pick = blk_row == first
            selected = selected | (pick & (blk_row < i))
            g = jnp.where(pick, REMOVED, g)
        sel_neg = jnp.where(selected, 0.0, NEG_INF).astype(BF16)
        q_own = q_heads[n][hh * HEAD_DIM:(hh + 1) * HEAD_DIM]
        q_aug.append(jnp.concatenate([q_own, sel_neg] if hh == 0 else [sel_neg, q_own], axis=0))
        s = jnp.where(causal, own_scores[n], NEG_INF)
        mx = jnp.max(s, axis=0, keepdims=True)
        own_probs.append((mx, jnp.exp2(s - mx).astype(BF16)))
    init = []
    for n, (pr, hh) in enumerate(heads):
        mx, p = own_probs[n]
        init.append((mx, _dot(v_ref[pr, i, hh * V_ROWS:(hh + 1) * V_ROWS, :], p)))

    def body(j, carry):
        rows = pl.ds(pl.multiple_of(j * blk, blk), blk)
        scores = []
        for n, (pr, hh) in enumerate(heads):
            k_pair = k_ref[rows, pr * LANES:(pr + 1) * LANES]
            other = (1 - hh) * HEAD_DIM
            in_other = (lane >= other) & (lane < other + HEAD_DIM)
            k_aug = jnp.where(in_other, (lane == other + j).astype(BF16), k_pair)
            scores.append(_dot(k_aug, q_aug[n]))
        probs = []
        for n in range(len(heads)):
            m = carry[n][0]
            m_new = jnp.maximum(m, jnp.max(scores[n], axis=0, keepdims=True))
            probs.append((m_new, jnp.exp2(m - m_new), jnp.exp2(scores[n] - m_new).astype(BF16)))
        out = []
        for n, (pr, hh) in enumerate(heads):
            m_new, alpha, p = probs[n]
            pv = _dot(v_ref[pr, j, hh * V_ROWS:(hh + 1) * V_ROWS, :], p)
            out.append((m_new, alpha * carry[n][1] + pv))
        return tuple(out)

    res = lax.fori_loop(0, i, body, tuple(init))
    o_t = jnp.concatenate([acc[:HEAD_DIM] / acc[HEAD_DIM:HEAD_DIM + 1] for _, acc in res], axis=0)
    o_ref[...] = o_t.T.astype(BF16)


def _attention(q, k, v_t, bsz, seq):
    t, aw = q.shape
    blk = MOBA_BLOCK
    nb = seq // blk
    assert nb <= HEAD_DIM
    qmap = lambda b, i: (b * nb + i, 0)
    return pl.pallas_call(
        _attn_kernel,
        grid=(bsz, nb),
        in_specs=[pl.BlockSpec((blk, aw), qmap),
                  pl.BlockSpec((seq, aw), lambda b, i: (b, 0)),
                  pl.BlockSpec((aw // LANES, nb, 2 * V_ROWS, blk), lambda b, i: (0, b, 0, 0))],
        out_specs=pl.BlockSpec((blk, aw), qmap),
        out_shape=jax.ShapeDtypeStruct((t, aw), BF16),
        scratch_shapes=[pltpu.VMEM((LANES, aw), F32)],
        compiler_params=_cparams(("parallel", "arbitrary")),
        name="moba_attention",
    )(q, k, v_t)


def _s5_tables(lam_re, lam_im, log_dt, b_re, b_im, c_re, c_im):
    hp = lax.Precision.HIGHEST
    L = S5_CHUNK
    lr, li = lam_re.astype(F32), lam_im.astype(F32)
    dt = jnp.exp(log_dt.astype(F32))[:, None]
    mag = jnp.exp(lr * dt)
    a_re = mag * jnp.cos(li * dt)
    a_im = mag * jnp.sin(li * dt)
    den = lr * lr + li * li
    nr = a_re - 1.0
    coef_re = (nr * lr + a_im * li) / den
    coef_im = (a_im * lr - nr * li) / den
    br, bi = b_re.astype(F32), b_im.astype(F32)
    bbar_re = coef_re[..., None] * br - coef_im[..., None] * bi
    bbar_im = coef_re[..., None] * bi + coef_im[..., None] * br
    tau = jnp.arange(L + 1, dtype=F32)[:, None, None]
    pmag = jnp.exp(tau * (lr * dt))
    p_re = pmag * jnp.cos(tau * (li * dt))
    p_im = pmag * jnp.sin(tau * (li * dt))
    cr, ci = c_re.astype(F32), c_im.astype(F32)
    ca_re = cr[None] * p_re[:, :, None, :] - ci[None] * p_im[:, :, None, :]
    ca_im = cr[None] * p_im[:, :, None, :] + ci[None] * p_re[:, :, None, :]
    kmat = (jnp.einsum('tgon,gni->tgio', ca_re[:L], bbar_re, precision=hp)
            - jnp.einsum('tgon,gni->tgio', ca_im[:L], bbar_im, precision=hp))
    s_ix = jnp.arange(L)[:, None]
    t_ix = jnp.arange(L)[None, :]
    lag = jnp.clip(t_ix - s_ix, 0, L - 1)
    m = kmat[lag]
    m = jnp.where((t_ix >= s_ix)[:, :, None, None, None], m, 0.0)
    m_intra = m.transpose(2, 0, 3, 1, 4).reshape(S5_GROUPS, L * S5_GROUP, L * S5_GROUP)
    rev = p_re[L - 1 - jnp.arange(L)], p_im[L - 1 - jnp.arange(L)]
    w_re = rev[0][..., None] * bbar_re[None] - rev[1][..., None] * bbar_im[None]
    w_im = rev[0][..., None] * bbar_im[None] + rev[1][..., None] * bbar_re[None]
    w_re = w_re.transpose(1, 0, 3, 2).reshape(S5_GROUPS, L * S5_GROUP, S5_STATE)
    w_im = w_im.transpose(1, 0, 3, 2).reshape(S5_GROUPS, L * S5_GROUP, S5_STATE)
    w_state = jnp.concatenate([w_re, w_im, w_im, w_re], axis=-1)
    cx_re = ca_re[1:].transpose(1, 3, 0, 2).reshape(S5_GROUPS, S5_STATE, L * S5_GROUP)
    cx_im = ca_im[1:].transpose(1, 3, 0, 2).reshape(S5_GROUPS, S5_STATE, L * S5_GROUP)
    cx = jnp.concatenate([cx_re, -cx_im], axis=1)
    al_re, al_im = p_re[L], p_im[L]
    a1 = jnp.concatenate([al_re, al_re], axis=-1)
    a2 = jnp.concatenate([-al_im, al_im], axis=-1)
    return m_intra.astype(BF16), w_state.astype(BF16), cx.astype(BF16), a1, a2


def _s5_group_kernel(n_seq, u_ref, w_ref, m_ref, cx_ref, a1_ref, a2_ref, y_ref, e_ref, h_ref):
    u = u_ref[0]
    e_ref[...] = _dot(u, w_ref[0])
    a1 = a1_ref[0]
    a2 = a2_ref[0]
    nk = e_ref.shape[0] // n_seq
    half = h_ref.shape[1]

    def step(k, carry):
        out = []
        for b in range(n_seq):
            hp, hq = carry[b]
            row = pl.ds(b * nk + k, 1)
            h_ref[row, :] = hp
            e = e_ref[row, :]
            out.append((a1 * hp + a2 * hq + e[:, :half], a1 * hq - a2 * hp + e[:, half:]))
        return tuple(out)

    zero = jnp.zeros((1, half), F32)
    lax.fori_loop(0, nk, step, tuple((zero, zero) for _ in range(n_seq)), unroll=4)
    y_ref[0] = _dot(u, m_ref[0]) + _dot(h_ref[...].astype(BF16), cx_ref[0])


def _s5_groups(ug, tables, n_seq):
    m_intra, w_state, cx, a1, a2 = tables
    g, r, c = ug.shape
    n2 = cx.shape[1]
    spec = lambda shape: pl.BlockSpec(shape, lambda i: (i, 0, 0))
    return pl.pallas_call(
        functools.partial(_s5_group_kernel, n_seq),
        grid=(g,),
        in_specs=[spec((1, r, c)), spec((1, c, c)), spec((1, c, c)), spec((1, n2, c)),
                  spec((1, 1, n2)), spec((1, 1, n2))],
        out_specs=spec((1, r, c)),
        out_shape=jax.ShapeDtypeStruct((g, r, c), F32),
        scratch_shapes=[pltpu.VMEM((r, c), F32), pltpu.VMEM((r, n2), F32)],
        compiler_params=_cparams(("parallel",)),
        name="s5_groups",
    )(ug, w_state, m_intra, cx, a1[:, None, :], a2[:, None, :])


def _s5_branch(u, tables, bsz, seq):
    L, G, C = S5_CHUNK, S5_GROUPS, S5_GROUP
    nk = seq // L
    ug = u.reshape(bsz, nk, L, G, C).transpose(3, 0, 1, 2, 4).reshape(G, bsz * nk, L * C).astype(BF16)
    y = _s5_groups(ug, tables, bsz)
    return y.reshape(G, bsz, nk, L, C).transpose(1, 2, 3, 0, 4).reshape(bsz * seq, G * C)


def _merge_kernel(alpha, attn_ref, y_ref, u_ref, ga_ref, gs_ref, x_ref, aup_ref, wglu_ref, sup_ref, wo_ref,
                  dskip_ref, bglu_ref, g_ref, b_ref, o_ref):
    a = _dot(attn_ref[...], aup_ref[...])
    y = y_ref[...] + dskip_ref[...] * u_ref[...]
    z = 0.5 * y * (1.0 + jnp.tanh(math.sqrt(2.0 / math.pi) * (y + 0.044715 * (y * y * y))))
    gate = _dot(z.astype(BF16), wglu_ref[...]) + bglu_ref[...]
    s = _dot((z * _sigmoid(gate)).astype(BF16), sup_ref[...])
    mixed = _sigmoid(ga_ref[...]) * a + _sigmoid(gs_ref[...]) * s
    r = alpha * x_ref[...] + _dot(mixed.astype(BF16), wo_ref[...])
    o_ref[...] = _layer_norm(r, g_ref[...], b_ref[...])


def _merge(alpha, attn, y5, u, g_a, g_s, x2, aup, wglu, sup, wo, dskip, bglu, ln_g, ln_b, tm=256):
    t, d = x2.shape
    row = lambda i: (i, 0)
    full = lambda a: pl.BlockSpec(a.shape, lambda i: (0, 0))
    return pl.pallas_call(
        functools.partial(_merge_kernel, alpha),
        grid=(t // tm,),
        in_specs=[pl.BlockSpec((tm, ATTN_WIDTH), row), pl.BlockSpec((tm, S5_WIDTH), row),
                  pl.BlockSpec((tm, S5_WIDTH), row), pl.BlockSpec((tm, d), row), pl.BlockSpec((tm, d), row),
                  pl.BlockSpec((tm, d), row), full(aup), full(wglu), full(sup), full(wo),
                  full(dskip), full(bglu), full(ln_g), full(ln_b)],
        out_specs=pl.BlockSpec((tm, d), row),
        out_shape=jax.ShapeDtypeStruct((t, d), F32),
        compiler_params=_cparams(("parallel",)),
        name="merge_ln1",
    )(attn, y5, u, g_a, g_s, x2, aup, wglu, sup, wo, dskip, bglu, ln_g, ln_b)


def _router_kernel(x_ref, wh_ref, wl_ref, bias_ref, idx_ref, wt_ref, rank_ref, cnt_ref, carry_ref):
    tm = x_ref.shape[0]

    @pl.when(pl.program_id(0) == 0)
    def _():
        carry_ref[...] = jnp.zeros(carry_ref.shape, F32)

    x = x_ref[...]
    xh = x.astype(BF16)
    xl = (x - xh.astype(F32)).astype(BF16)
    wh = wh_ref[...]
    logits = _dot_nt(wh, xh) + _dot_nt(wh, xl) + _dot_nt(wl_ref[...], xh)
    scores = _sigmoid(logits)
    biased = scores + bias_ref[...]
    gs_rows = GROUP_SIZE
    row_g = lax.broadcasted_iota(I32, (gs_rows, tm), 0)
    group_score = []
    for g in range(N_EXPERT_GROUPS):
        vg = biased[g * gs_rows:(g + 1) * gs_rows, :]
        m1 = jnp.max(vg, axis=0, keepdims=True)
        i1 = jnp.min(jnp.where(vg == m1, row_g, gs_rows), axis=0, keepdims=True)
        m2 = jnp.max(jnp.where(row_g == i1, REMOVED, vg), axis=0, keepdims=True)
        group_score.append(m1 + m2)
    pieces = []
    for g in range(N_EXPERT_GROUPS):
        beats = jnp.zeros((1, tm), I32)
        for g2 in range(N_EXPERT_GROUPS):
            if g2 == g:
                continue
            better = group_score[g2] > group_score[g]
            if g2 < g:
                better = better | (group_score[g2] == group_score[g])
            beats = beats + better.astype(I32)
        keep = beats < TOPK_GROUPS
        pieces.append(jnp.where(keep, biased[g * gs_rows:(g + 1) * gs_rows, :], NEG_INF))
    v = jnp.concatenate(pieces, axis=0)
    row = lax.broadcasted_iota(I32, (N_EXPERTS, tm), 0)
    idxs, wts = [], []
    multi_hot = jnp.zeros((N_EXPERTS, tm), F32)
    for _ in range(TOP_K):
        m = jnp.max(v, axis=0, keepdims=True)
        ix = jnp.min(jnp.where(v == m, row, N_EXPERTS), axis=0, keepdims=True)
        hit = row == ix
        idxs.append(ix)
        wts.append(jnp.sum(jnp.where(hit, scores, 0.0), axis=0, keepdims=True))
        multi_hot = multi_hot + hit.astype(F32)
        v = jnp.where(hit, REMOVED, v)
    wsum = wts[0]
    for k in range(1, TOP_K):
        wsum = wsum + wts[k]
    s_ix = lax.broadcasted_iota(I32, (tm, tm), 0)
    t_ix = lax.broadcasted_iota(I32, (tm, tm), 1)
    mh = multi_hot.astype(BF16)
    before = _dot(mh, (s_ix < t_ix).astype(BF16)) + carry_ref[...]
    carry_ref[...] = carry_ref[...] + _dot(mh, jnp.ones((tm, tm), BF16))
    for k in range(TOP_K):
        idx_ref[k:k + 1, :] = idxs[k]
        wt_ref[k:k + 1, :] = wts[k] / wsum * ROUTED_SCALE
        rank_ref[k:k + 1, :] = jnp.sum(jnp.where(row == idxs[k], before, 0.0), axis=0, keepdims=True).astype(I32)
    cnt_ref[...] = carry_ref[:, :LANES]


def _router(x1, wr_hi, wr_lo, bias_b, tm=256):
    t, d = x1.shape
    e = wr_hi.shape[0]
    tok = lambda i: (0, i)
    const = lambda i: (0, 0)
    return pl.pallas_call(
        _router_kernel,
        grid=(t // tm,),
        in_specs=[pl.BlockSpec((tm, d), lambda i: (i, 0)), pl.BlockSpec((e, d), const),
                  pl.BlockSpec((e, d), const), pl.BlockSpec((e, tm), const)],
        out_specs=[pl.BlockSpec((TOP_K, tm), tok), pl.BlockSpec((TOP_K, tm), tok),
                   pl.BlockSpec((TOP_K, tm), tok), pl.BlockSpec((e, LANES), const)],
        out_shape=[jax.ShapeDtypeStruct((TOP_K, t), I32), jax.ShapeDtypeStruct((TOP_K, t), F32),
                   jax.ShapeDtypeStruct((TOP_K, t), I32), jax.ShapeDtypeStruct((e, LANES), F32)],
        scratch_shapes=[pltpu.VMEM((e, tm), F32)],
        compiler_params=_cparams(("arbitrary",)),
        name="router_topk_rank",
    )(x1, wr_hi, wr_lo, bias_b)


def _dest_kernel(idx_ref, rank_ref, pstart_ref, dest_ref):
    pstart = pstart_ref[...]
    row = lax.broadcasted_iota(I32, pstart.shape, 0)
    for k in range(TOP_K):
        start = jnp.sum(jnp.where(row == idx_ref[k:k + 1, :], pstart, 0.0), axis=0, keepdims=True)
        dest_ref[k:k + 1, :] = start.astype(I32) + rank_ref[k:k + 1, :]


def _dest_rows(top_idx, rank, pstart_b):
    tm = pstart_b.shape[1]
    tok = lambda i: (0, i)
    return pl.pallas_call(
        _dest_kernel,
        grid=(top_idx.shape[1] // tm,),
        in_specs=[pl.BlockSpec((TOP_K, tm), tok), pl.BlockSpec((TOP_K, tm), tok),
                  pl.BlockSpec(pstart_b.shape, lambda i: (0, 0))],
        out_specs=pl.BlockSpec((TOP_K, tm), tok),
        out_shape=jax.ShapeDtypeStruct(top_idx.shape, I32),
        compiler_params=_cparams(("parallel",)),
        name="moe_dest_rows",
    )(top_idx, rank, pstart_b)


def _rows_wait(src_hbm, dst, sem, n_rows):
    pltpu.make_async_copy(src_hbm.at[pl.ds(0, n_rows), :], dst, sem).wait()


def _dispatch_kernel(pend_ref, cnt_ref, dest_ref, x_ref, xs_hbm, zbuf, zsem, sem):
    i = pl.program_id(0)
    tm = dest_ref.shape[1]
    mb = zbuf.shape[0]

    @pl.when(i == 0)
    def _():
        zbuf[...] = jnp.zeros(zbuf.shape, zbuf.dtype)

        def zero_copy(e):
            start = pl.multiple_of(pend_ref[e] - mb, mb)
            return pltpu.make_async_copy(zbuf, xs_hbm.at[pl.ds(start, mb), :], zsem)

        def z_start(e, c):
            @pl.when(cnt_ref[e] > 0)
            def _():
                zero_copy(e).start()
            return c

        def z_wait(e, c):
            @pl.when(cnt_ref[e] > 0)
            def _():
                zero_copy(e).wait()
            return c

        def tail_copy(b):
            return pltpu.make_async_copy(zbuf, xs_hbm.at[pl.ds(pl.multiple_of(b * mb, mb), mb), :], zsem)

        def t_start(b, c):
            tail_copy(b).start()
            return c

        def t_wait(b, c):
            tail_copy(b).wait()
            return c

        n_used = pend_ref[N_EXPERTS - 1] // mb
        n_blocks = xs_hbm.shape[0] // mb
        lax.fori_loop(0, N_EXPERTS, z_start, 0)
        lax.fori_loop(n_used, n_blocks, t_start, 0)
        lax.fori_loop(0, N_EXPERTS, z_wait, 0)
        lax.fori_loop(n_used, n_blocks, t_wait, 0)

    def issue(t, c):
        src = x_ref.at[pl.ds(t, 1), :]
        for k in range(TOP_K):
            pltpu.make_async_copy(src, xs_hbm.at[pl.ds(dest_ref[k, t], 1), :], sem).start()
        return c

    lax.fori_loop(0, tm, issue, 0, unroll=2)
    for _ in range(TOP_K):
        pltpu.make_async_copy(x_ref, xs_hbm.at[pl.ds(0, tm), :], sem).wait()


def _dispatch(padded_ends, counts, dest, x1, n_rows, tm=512):
    t, d = x1.shape
    grid_spec = pltpu.PrefetchScalarGridSpec(
        num_scalar_prefetch=2,
        grid=(t // tm,),
        in_specs=[pl.BlockSpec((TOP_K, tm), lambda i, pe, cn: (0, i), memory_space=pltpu.SMEM),
                  pl.BlockSpec((tm, d), lambda i, pe, cn: (i, 0))],
        out_specs=pl.BlockSpec(memory_space=pl.ANY),
        scratch_shapes=[pltpu.VMEM((MOE_ROWS, d), x1.dtype), pltpu.SemaphoreType.DMA(()),
                        pltpu.SemaphoreType.DMA(())],
    )
    return pl.pallas_call(
        _dispatch_kernel,
        grid_spec=grid_spec,
        out_shape=jax.ShapeDtypeStruct((n_rows, d), x1.dtype),
        compiler_params=_cparams(("arbitrary",)),
        name="moe_dispatch",
    )(padded_ends, counts, dest, x1)


def _expert_kernel(first_ref, nblk_ref, nused_ref, xs_hbm, wg_ref, wu_ref, wd_ref, y_hbm,
                   xbuf, ybuf, wgb, wub, wdb, xsem, ysem):
    e = pl.program_id(0)
    n_used = nused_ref[0]
    mb = xbuf.shape[1]

    def x_copy(g, slot):
        return pltpu.make_async_copy(xs_hbm.at[pl.ds(pl.multiple_of(g * mb, mb), mb), :], xbuf.at[slot],
                                     xsem.at[slot])

    def y_copy(g, slot):
        return pltpu.make_async_copy(ybuf.at[slot], y_hbm.at[pl.ds(pl.multiple_of(g * mb, mb), mb), :],
                                     ysem.at[slot])

    @pl.when(e == 0)
    def _():
        x_copy(0, 0).start()

    wgb[...] = wg_ref[0, 0].astype(BF16)
    wub[...] = wu_ref[0, 0].astype(BF16)
    wdb[...] = wd_ref[0, 0].astype(BF16)

    def block(b, c):
        g = first_ref[e] + b
        slot = g % 2

        @pl.when(g + 1 < n_used)
        def _():
            x_copy(g + 1, 1 - slot).start()

        x_copy(g, slot).wait()

        @pl.when(g >= 2)
        def _():
            y_copy(g - 2, slot).wait()

        xb = xbuf[slot].astype(BF16)
        hg = _dot(xb, wgb[...])
        hu = _dot(xb, wub[...])
        h = hg * _sigmoid(hg) * hu
        ybuf[slot] = _dot(h.astype(BF16), wdb[...])
        y_copy(g, slot).start()
        return c

    lax.fori_loop(0, nblk_ref[e], block, 0)

    @pl.when(e == pl.num_programs(0) - 1)
    def _():
        @pl.when(n_used >= 2)
        def _():
            y_copy(n_used - 2, n_used % 2).wait()

        y_copy(n_used - 1, (n_used - 1) % 2).wait()
        ybuf[0] = jnp.zeros(ybuf.shape[1:], ybuf.dtype)
        n_blocks = y_hbm.shape[0] // mb

        def z_start(g, c):
            y_copy(g, 0).start()
            return c

        def z_wait(g, c):
            y_copy(g, 0).wait()
            return c

        lax.fori_loop(n_used, n_blocks, z_start, 0)
        lax.fori_loop(n_used, n_blocks, z_wait, 0)


def _experts(layer, first_block, n_blk, n_used, xs, w_gate_e, w_up_e, w_down_e):
    mb = MOE_ROWS
    n_rows, d = xs.shape
    _, n_exp, _, hdim = w_gate_e.shape
    wmap = lambda e, fb, nb, nu: (layer, e, 0, 0)
    grid_spec = pltpu.PrefetchScalarGridSpec(
        num_scalar_prefetch=3,
        grid=(n_exp,),
        in_specs=[pl.BlockSpec(memory_space=pl.ANY),
                  pl.BlockSpec((1, 1, d, hdim), wmap),
                  pl.BlockSpec((1, 1, d, hdim), wmap),
                  pl.BlockSpec((1, 1, hdim, d), wmap)],
        out_specs=pl.BlockSpec(memory_space=pl.ANY),
        scratch_shapes=[pltpu.VMEM((2, mb, d), xs.dtype), pltpu.VMEM((2, mb, d), F32),
                        pltpu.VMEM((d, hdim), BF16), pltpu.VMEM((d, hdim), BF16), pltpu.VMEM((hdim, d), BF16),
                        pltpu.SemaphoreType.DMA((2,)), pltpu.SemaphoreType.DMA((2,))],
    )
    return pl.pallas_call(
        _expert_kernel,
        grid_spec=grid_spec,
        out_shape=jax.ShapeDtypeStruct((n_rows, d), F32),
        compiler_params=_cparams(("arbitrary",)),
        name="routed_experts",
    )(first_block, n_blk, n_used, xs, w_gate_e, w_up_e, w_down_e)


def _final_kernel(alpha, dcur_ref, dnext_ref, x_ref, wt_ref, y_hbm, wsg_ref, wsu_ref, wsd_ref, g_ref, b_ref,
                  o_ref, ybuf, sems):
    i = pl.program_id(0)
    n_steps = pl.num_programs(0)
    tm = x_ref.shape[0]
    slot = i % 2

    def gather(dref, s):
        def issue(t, c):
            for k in range(TOP_K):
                pltpu.make_async_copy(y_hbm.at[pl.ds(dref[k, t], 1), :], ybuf.at[s, k, pl.ds(t, 1), :],
                                      sems.at[s]).start()
            return c

        lax.fori_loop(0, tm, issue, 0, unroll=2)

    @pl.when(i == 0)
    def _():
        gather(dcur_ref, 0)

    @pl.when(i + 1 < n_steps)
    def _():
        gather(dnext_ref, 1 - slot)

    x = x_ref[...]
    xb = x.astype(BF16)
    hg = _dot(xb, wsg_ref[...])
    hu = _dot(xb, wsu_ref[...])
    f = _dot((hg * _sigmoid(hg) * hu).astype(BF16), wsd_ref[...])
    for k in range(TOP_K):
        _rows_wait(y_hbm, ybuf.at[slot, k], sems.at[slot], tm)
    wt = wt_ref[...]
    for k in range(TOP_K):
        f = f + wt[:, k:k + 1] * ybuf[slot, k]
    o_ref[...] = _layer_norm(alpha * x + f, g_ref[...], b_ref[...])


def _final(alpha, dest, x1, wt_tok, y_sorted, wsg, wsu, wsd, ln_g, ln_b, tm=256):
    t, d = x1.shape
    n = t // tm
    row = lambda i: (i, 0)
    full = lambda a: pl.BlockSpec(a.shape, lambda i: (0, 0))
    return pl.pallas_call(
        functools.partial(_final_kernel, alpha),
        grid=(n,),
        in_specs=[pl.BlockSpec((TOP_K, tm), lambda i: (0, i), memory_space=pltpu.SMEM),
                  pl.BlockSpec((TOP_K, tm), lambda i: (0, jnp.minimum(i + 1, n - 1)), memory_space=pltpu.SMEM),
                  pl.BlockSpec((tm, d), row), pl.BlockSpec((tm, TOP_K), row),
                  pl.BlockSpec(memory_space=pl.ANY), full(wsg), full(wsu), full(wsd), full(ln_g), full(ln_b)],
        out_specs=pl.BlockSpec((tm, d), row),
        out_shape=jax.ShapeDtypeStruct((t, d), F32),
        scratch_shapes=[pltpu.VMEM((2, TOP_K, tm, d), F32), pltpu.SemaphoreType.DMA((2,))],
        compiler_params=_cparams(("arbitrary",)),
        name="combine_shared_ln2",
    )(dest, dest, x1, wt_tok, y_sorted, wsg, wsu, wsd, ln_g, ln_b)


def _block_plan(counts, n_tok):
    mb = MOE_ROWS
    n_blocks = -(-(n_tok * TOP_K + N_EXPERTS * (mb - 1)) // mb)
    padded = (counts + mb - 1) // mb * mb
    padded_ends = jnp.cumsum(padded).astype(I32)
    padded_starts = padded_ends - padded
    n_used = (padded_ends[-1] // mb).astype(I32).reshape(1)
    return padded_starts, padded_ends, padded_starts // mb, padded // mb, n_used, n_blocks * mb


def kernel(x, positions, w_in, attn_up, lam_re, lam_im, log_dt, b_re, b_im, c_re, c_im, d_skip, w_glu, b_glu,
           s5_up, w_o, ln1_g, ln1_b, w_router, router_bias, w_gate_e, w_up_e, w_down_e, ws_gate, ws_up, ws_down,
           ln2_g, ln2_b):
    bsz, seq, d = x.shape
    depth = w_in.shape[0]
    t = bsz * seq
    alpha = (2 * depth) ** 0.25
    assert seq % MOBA_BLOCK == 0 and seq % S5_CHUNK == 0 and t % 512 == 0

    inv_freq = ROPE_THETA ** (-jnp.arange(0, HEAD_DIM, 2, dtype=F32) / HEAD_DIM)
    ang = positions.astype(F32).reshape(t, 1) * inv_freq
    cos, sin = jnp.cos(ang), jnp.sin(ang)
    cos_t = jnp.concatenate([cos, cos, cos, cos], axis=1)
    sin_t = jnp.concatenate([-sin, sin, -sin, sin], axis=1)

    xc = x.reshape(t, d)
    row2 = lambda a: a.reshape(1, -1).astype(F32)
    for l in range(depth):
        w_v_t = w_in[l][:, 2 * ATTN_WIDTH:3 * ATTN_WIDTH].T.astype(BF16)
        q, k, v_t, u, g_a, g_s = _proj(xc, w_in[l].astype(BF16), w_v_t, cos_t, sin_t)
        attn = _attention(q, k, v_t, bsz, seq)
        tables = _s5_tables(lam_re[l], lam_im[l], log_dt[l], b_re[l], b_im[l], c_re[l], c_im[l])
        y5 = _s5_branch(u, tables, bsz, seq)
        x1 = _merge(alpha, attn, y5, u, g_a, g_s, xc, attn_up[l].astype(BF16), w_glu[l].astype(BF16),
                    s5_up[l].astype(BF16), w_o[l].astype(BF16), row2(d_skip[l]), row2(b_glu[l]),
                    row2(ln1_g[l]), row2(ln1_b[l]))
        wr_t = w_router[l].astype(F32).T
        wr_hi = wr_t.astype(BF16)
        wr_lo = (wr_t - wr_hi.astype(F32)).astype(BF16)
        bias_b = jnp.broadcast_to(router_bias[l].astype(F32)[:, None], (N_EXPERTS, 256))
        top_idx, top_w, rank, counts = _router(x1, wr_hi, wr_lo, bias_b)
        counts = counts[:, 0].astype(I32)
        pstart, pend, first_block, n_blk, n_used, n_rows = _block_plan(counts, t)
        dest = _dest_rows(top_idx, rank, jnp.broadcast_to(pstart.astype(F32)[:, None], (N_EXPERTS, 512)))
        xs = _dispatch(pend, counts, dest, x1, n_rows)
        y_sorted = _experts(l, first_block, n_blk, n_used, xs, w_gate_e, w_up_e, w_down_e)
        xc = _final(alpha, dest, x1, top_w.T, y_sorted, ws_gate[l].astype(BF16), ws_up[l].astype(BF16),
                    ws_down[l].astype(BF16), row2(ln2_g[l]), row2(ln2_b[l]))
    return xc.reshape(bsz, seq, d)
```

```python
import functools
import math

import jax
import jax.numpy as jnp
from jax import lax
from jax.experimental import pallas as pl
from jax.experimental.pallas import tpu as pltpu

F32 = jnp.float32
BF16 = jnp.bfloat16
I32 = jnp.int32

N_HEADS = 8
HEAD_DIM = 64
ATTN_WIDTH = N_HEADS * HEAD_DIM
MOBA_BLOCK = 256
MOBA_TOPK = 3
ROPE_THETA = 10000.0
S5_GROUP = 16
S5_GROUPS = 32
S5_WIDTH = S5_GROUP * S5_GROUPS
S5_STATE = 64
S5_CHUNK = 16
N_EXPERTS = 256
TOP_K = 8
N_EXPERT_GROUPS = 8
TOPK_GROUPS = 4
GROUP_SIZE = N_EXPERTS // N_EXPERT_GROUPS
ROUTED_SCALE = 2.5
LN_EPS = 1e-5
NEG_INF = -1e30
REMOVED = -3e38
QK_SCALE = HEAD_DIM ** -0.5 * math.log2(math.e)
V_ROWS = HEAD_DIM + 16

LANES = 128
MOE_ROWS = 128
VMEM_LIMIT = 56 * 1024 * 1024


def _cparams(sem):
    return pltpu.CompilerParams(dimension_semantics=sem, vmem_limit_bytes=VMEM_LIMIT)


def _sigmoid(x):
    return 1.0 / (1.0 + jnp.exp(-x))


def _layer_norm(r, g, b):
    mu = jnp.mean(r, axis=-1, keepdims=True)
    rc = r - mu
    var = jnp.mean(rc * rc, axis=-1, keepdims=True)
    return rc * lax.rsqrt(var + LN_EPS) * g + b


def _dot(a, b):
    return jnp.dot(a, b, preferred_element_type=F32)


def _dot_nt(a, b):
    return lax.dot_general(a, b, (((1,), (1,)), ((), ())), preferred_element_type=F32)


def _proj_kernel(x_ref, w_ref, wvt_ref, cos_ref, sin_ref, q_ref, k_ref, v_ref, u_ref, ga_ref, gs_ref):
    xb = x_ref[...].astype(BF16)
    cos = cos_ref[...]
    sin = sin_ref[...]
    lane = lax.broadcasted_iota(I32, cos.shape, 1)
    first_half = (lane % HEAD_DIM) < (HEAD_DIM // 2)

    def rope(t):
        rot = jnp.where(first_half, pltpu.roll(t, LANES - HEAD_DIM // 2, axis=1),
                        pltpu.roll(t, HEAD_DIM // 2, axis=1))
        return t * cos + rot * sin

    aw = ATTN_WIDTH
    for c in range(aw // LANES):
        sl = slice(c * LANES, (c + 1) * LANES)
        qc = _dot(xb, w_ref[:, c * LANES:(c + 1) * LANES])
        q_ref[:, sl] = (rope(qc) * QK_SCALE).astype(BF16)
        kc = _dot(xb, w_ref[:, aw + c * LANES:aw + (c + 1) * LANES])
        k_ref[:, sl] = rope(kc).astype(BF16)
    vt = _dot_nt(wvt_ref[...], xb).astype(BF16)
    ones = jnp.ones((V_ROWS - HEAD_DIM, MOBA_BLOCK), BF16)
    for p in range(v_ref.shape[0]):
        for c in range(v_ref.shape[1]):
            cols = slice(c * MOBA_BLOCK, (c + 1) * MOBA_BLOCK)
            heads = [vt[(2 * p + hh) * HEAD_DIM:(2 * p + hh + 1) * HEAD_DIM, cols] for hh in range(2)]
            v_ref[p, c] = jnp.concatenate([heads[0], ones, heads[1], ones], axis=0)
    u_ref[...] = _dot(xb, w_ref[:, 3 * aw:3 * aw + S5_WIDTH])
    d = ga_ref.shape[1]
    off = 3 * aw + S5_WIDTH
    ga_ref[...] = _dot(xb, w_ref[:, off:off + d])
    gs_ref[...] = _dot(xb, w_ref[:, off + d:off + 2 * d])


def _proj(x2, w_bf, wvt_bf, cos_t, sin_t, tm=512):
    t, d = x2.shape
    n = w_bf.shape[1]
    aw = ATTN_WIDTH
    row = lambda i: (i, 0)
    vshape = (aw // LANES, t // MOBA_BLOCK, 2 * V_ROWS, MOBA_BLOCK)
    return pl.pallas_call(
        _proj_kernel,
        grid=(t // tm,),
        in_specs=[pl.BlockSpec((tm, d), row),
                  pl.BlockSpec((d, n), lambda i: (0, 0)),
                  pl.BlockSpec((aw, d), lambda i: (0, 0)),
                  pl.BlockSpec((tm, LANES), row),
                  pl.BlockSpec((tm, LANES), row)],
        out_specs=[pl.BlockSpec((tm, aw), row), pl.BlockSpec((tm, aw), row),
                   pl.BlockSpec((vshape[0], tm // MOBA_BLOCK) + vshape[2:], lambda i: (0, i, 0, 0)),
                   pl.BlockSpec((tm, S5_WIDTH), row), pl.BlockSpec((tm, d), row), pl.BlockSpec((tm, d), row)],
        out_shape=[jax.ShapeDtypeStruct((t, aw), BF16), jax.ShapeDtypeStruct((t, aw), BF16),
                   jax.ShapeDtypeStruct(vshape, BF16), jax.ShapeDtypeStruct((t, S5_WIDTH), F32),
                   jax.ShapeDtypeStruct((t, d), F32), jax.ShapeDtypeStruct((t, d), F32)],
        compiler_params=_cparams(("parallel",)),
        name="proj_rope",
    )(x2, w_bf, wvt_bf, cos_t, sin_t)


def _attn_kernel(q_ref, k_ref, v_ref, o_ref, kmean_ref):
    i = pl.program_id(1)
    blk = MOBA_BLOCK
    n_blocks = k_ref.shape[0] // blk
    n_pairs = q_ref.shape[1] // LANES

    @pl.when(i == 0)
    def _():
        kmean_ref[...] = jnp.zeros(kmean_ref.shape, kmean_ref.dtype)
        for j in range(n_blocks):
            kj = k_ref[j * blk:(j + 1) * blk, :].astype(F32)
            kmean_ref[j:j + 1, :] = jnp.sum(kj, axis=0, keepdims=True) * (1.0 / blk)

    q_t = q_ref[...].astype(F32).T.astype(BF16)
    sub = lax.broadcasted_iota(I32, (LANES, blk), 0)
    key = lax.broadcasted_iota(I32, (blk, blk), 0)
    qry = lax.broadcasted_iota(I32, (blk, blk), 1)
    causal = key <= qry
    own = pl.ds(pl.multiple_of(i * blk, blk), blk)

    heads = [(pr, hh) for pr in range(n_pairs) for hh in range(2)]
    blk_row = lax.broadcasted_iota(I32, (HEAD_DIM, blk), 0)
    q_heads, gates, own_scores = [], [], []
    for pr, hh in heads:
        cols = slice(pr * LANES, (pr + 1) * LANES)
        q_pair = q_t[pr * LANES:(pr + 1) * LANES]
        in_head = (sub >= hh * HEAD_DIM) & (sub < (hh + 1) * HEAD_DIM)
        qh = jnp.where(in_head, q_pair, jnp.zeros_like(q_pair))
        q_heads.append(qh)
        gates.append(_dot(kmean_ref[:HEAD_DIM, cols].astype(BF16), qh))
        own_scores.append(_dot(k_ref[own, cols], qh))
    q_aug, own_probs = [], []
    for n, (pr, hh) in enumerate(heads):
        g = jnp.where(blk_row < i, gates[n], NEG_INF)
        selected = jnp.zeros(g.shape, jnp.bool_)
        for _ in range(MOBA_TOPK):
            mx = jnp.max(g, axis=0, keepdims=True)
            first = jnp.min(jnp.where(g == mx, blk_row, HEAD_DIM), axis=0, keepdims=True)
            pick = blk_row == first
            selected = selected | (pick & (blk_row < i))
            g = jnp.where(pick, REMOVED, g)
        sel_neg = jnp.where(selected, 0.0, NEG_INF).astype(BF16)
        q_own = q_heads[n][hh * HEAD_DIM:(hh + 1) * HEAD_DIM]
        q_aug.append(jnp.concatenate([q_own, sel_neg] if hh == 0 else [sel_neg, q_own], axis=0))
        s = jnp.where(causal, own_scores[n], NEG_INF)
        mx = jnp.max(s, axis=0, keepdims=True)
        own_probs.append((mx, jnp.exp2(s - mx).astype(BF16)))
    init = []
    for n, (pr, hh) in enumerate(heads):
        mx, p = own_probs[n]
        init.append((mx, _dot(v_ref[pr, i, hh * V_ROWS:(hh + 1) * V_ROWS, :], p)))

    lane2 = lax.broadcasted_iota(I32, (2 * blk, LANES), 1)
    second = (lax.broadcasted_iota(I32, (2 * blk, LANES), 0) >= blk).astype(I32)

    def body(jj, carry):
        rows = pl.ds(pl.multiple_of(jj * (2 * blk), 2 * blk), 2 * blk)
        blk_id = 2 * jj + second
        scores = []
        for n, (pr, hh) in enumerate(heads):
            k_pair = k_ref[rows, pr * LANES:(pr + 1) * LANES]
            other = (1 - hh) * HEAD_DIM
            in_other = (lane2 >= other) & (lane2 < other + HEAD_DIM)
            k_aug = jnp.where(in_other, (lane2 == other + blk_id).astype(BF16), k_pair)
            scores.append(_dot(k_aug, q_aug[n]))
        probs = []
        for n in range(len(heads)):
            m = carry[n][0]
            m_new = jnp.maximum(m, jnp.max(scores[n], axis=0, keepdims=True))
            probs.append((m_new, jnp.exp2(m - m_new), jnp.exp2(scores[n] - m_new).astype(BF16)))
        out = []
        for n, (pr, hh) in enumerate(heads):
            m_new, alpha, p = probs[n]
            vrows = slice(hh * V_ROWS, (hh + 1) * V_ROWS)
            pv = _dot(v_ref[pr, 2 * jj, vrows, :], p[:blk]) + _dot(v_ref[pr, 2 * jj + 1, vrows, :], p[blk:])
            out.append((m_new, alpha * carry[n][1] + pv))
        return tuple(out)

    res = lax.fori_loop(0, (i + 1) // 2, body, tuple(init))
    o_t = jnp.concatenate([acc[:HEAD_DIM] / acc[HEAD_DIM:HEAD_DIM + 1] for _, acc in res], axis=0)
    o_ref[...] = o_t.T.astype(BF16)


def _attention(q, k, v_t, bsz, seq):
    t, aw = q.shape
    blk = MOBA_BLOCK
    nb = seq // blk
    assert nb <= HEAD_DIM
    assert nb % 2 == 0
    qmap = lambda b, i: (b * nb + i, 0)
    return pl.pallas_call(
        _attn_kernel,
        grid=(bsz, nb),
        in_specs=[pl.BlockSpec((blk, aw), qmap),
                  pl.BlockSpec((seq, aw), lambda b, i: (b, 0)),
                  pl.BlockSpec((aw // LANES, nb, 2 * V_ROWS, blk), lambda b, i: (0, b, 0, 0))],
        out_specs=pl.BlockSpec((blk, aw), qmap),
        out_shape=jax.ShapeDtypeStruct((t, aw), BF16),
        scratch_shapes=[pltpu.VMEM((LANES, aw), F32)],
        compiler_params=_cparams(("parallel", "arbitrary")),
        name="moba_attention",
    )(q, k, v_t)


def _s5_tables(lam_re, lam_im, log_dt, b_re, b_im, c_re, c_im):
    hp = lax.Precision.HIGHEST
    L = S5_CHUNK
    lr, li = lam_re.astype(F32), lam_im.astype(F32)
    dt = jnp.exp(log_dt.astype(F32))[:, None]
    mag = jnp.exp(lr * dt)
    a_re = mag * jnp.cos(li * dt)
    a_im = mag * jnp.sin(li * dt)
    den = lr * lr + li * li
    nr = a_re - 1.0
    coef_re = (nr * lr + a_im * li) / den
    coef_im = (a_im * lr - nr * li) / den
    br, bi = b_re.astype(F32), b_im.astype(F32)
    bbar_re = coef_re[..., None] * br - coef_im[..., None] * bi
    bbar_im = coef_re[..., None] * bi + coef_im[..., None] * br
    tau = jnp.arange(L + 1, dtype=F32)[:, None, None]
    pmag = jnp.exp(tau * (lr * dt))
    p_re = pmag * jnp.cos(tau * (li * dt))
    p_im = pmag * jnp.sin(tau * (li * dt))
    cr, ci = c_re.astype(F32), c_im.astype(F32)
    ca_re = cr[None] * p_re[:, :, None, :] - ci[None] * p_im[:, :, None, :]
    ca_im = cr[None] * p_im[:, :, None, :] + ci[None] * p_re[:, :, None, :]
    kmat = (jnp.einsum('tgon,gni->tgio', ca_re[:L], bbar_re, precision=hp)
            - jnp.einsum('tgon,gni->tgio', ca_im[:L], bbar_im, precision=hp))
    s_ix = jnp.arange(L)[:, None]
    t_ix = jnp.arange(L)[None, :]
    lag = jnp.clip(t_ix - s_ix, 0, L - 1)
    m = kmat[lag]
    m = jnp.where((t_ix >= s_ix)[:, :, None, None, None], m, 0.0)
    m_intra = m.transpose(2, 0, 3, 1, 4).reshape(S5_GROUPS, L * S5_GROUP, L * S5_GROUP)
    rev = p_re[L - 1 - jnp.arange(L)], p_im[L - 1 - jnp.arange(L)]
    w_re = rev[0][..., None] * bbar_re[None] - rev[1][..., None] * bbar_im[None]
    w_im = rev[0][..., None] * bbar_im[None] + rev[1][..., None] * bbar_re[None]
    w_re = w_re.transpose(1, 0, 3, 2).reshape(S5_GROUPS, L * S5_GROUP, S5_STATE)
    w_im = w_im.transpose(1, 0, 3, 2).reshape(S5_GROUPS, L * S5_GROUP, S5_STATE)
    w_state = jnp.concatenate([w_re, w_im, w_im, w_re], axis=-1)
    cx_re = ca_re[1:].transpose(1, 3, 0, 2).reshape(S5_GROUPS, S5_STATE, L * S5_GROUP)
    cx_im = ca_im[1:].transpose(1, 3, 0, 2).reshape(S5_GROUPS, S5_STATE, L * S5_GROUP)
    cx = jnp.concatenate([cx_re, -cx_im], axis=1)
    al_re, al_im = p_re[L], p_im[L]
    a1 = jnp.concatenate([al_re, al_re], axis=-1)
    a2 = jnp.concatenate([-al_im, al_im], axis=-1)
    return m_intra.astype(BF16), w_state.astype(BF16), cx.astype(BF16), a1, a2


def _s5_group_kernel(n_seq, u_ref, w_ref, m_ref, cx_ref, a1_ref, a2_ref, y_ref, e_ref, h_ref):
    u = u_ref[0]
    e_ref[...] = _dot(u, w_ref[0])
    a1 = a1_ref[0]
    a2 = a2_ref[0]
    nk = e_ref.shape[0] // n_seq
    half = h_ref.shape[1]

    def step(k, carry):
        out = []
        for b in range(n_seq):
            hp, hq = carry[b]
            row = pl.ds(b * nk + k, 1)
            h_ref[row, :] = hp
            e = e_ref[row, :]
            out.append((a1 * hp + a2 * hq + e[:, :half], a1 * hq - a2 * hp + e[:, half:]))
        return tuple(out)

    zero = jnp.zeros((1, half), F32)
    lax.fori_loop(0, nk, step, tuple((zero, zero) for _ in range(n_seq)), unroll=4)
    y_ref[0] = _dot(u, m_ref[0]) + _dot(h_ref[...].astype(BF16), cx_ref[0])


def _s5_groups(ug, tables, n_seq):
    m_intra, w_state, cx, a1, a2 = tables
    g, r, c = ug.shape
    n2 = cx.shape[1]
    spec = lambda shape: pl.BlockSpec(shape, lambda i: (i, 0, 0))
    return pl.pallas_call(
        functools.partial(_s5_group_kernel, n_seq),
        grid=(g,),
        in_specs=[spec((1, r, c)), spec((1, c, c)), spec((1, c, c)), spec((1, n2, c)),
                  spec((1, 1, n2)), spec((1, 1, n2))],
        out_specs=spec((1, r, c)),
        out_shape=jax.ShapeDtypeStruct((g, r, c), F32),
        scratch_shapes=[pltpu.VMEM((r, c), F32), pltpu.VMEM((r, n2), F32)],
        compiler_params=_cparams(("parallel",)),
        name="s5_groups",
    )(ug, w_state, m_intra, cx, a1[:, None, :], a2[:, None, :])


def _s5_branch(u, tables, bsz, seq):
    L, G, C = S5_CHUNK, S5_GROUPS, S5_GROUP
    nk = seq // L
    ug = u.reshape(bsz, nk, L, G, C).transpose(3, 0, 1, 2, 4).reshape(G, bsz * nk, L * C).astype(BF16)
    y = _s5_groups(ug, tables, bsz)
    return y.reshape(G, bsz, nk, L, C).transpose(1, 2, 3, 0, 4).reshape(bsz * seq, G * C)


def _merge_kernel(alpha, attn_ref, y_ref, u_ref, ga_ref, gs_ref, x_ref, aup_ref, wglu_ref, sup_ref, wo_ref,
                  dskip_ref, bglu_ref, g_ref, b_ref, o_ref):
    a = _dot(attn_ref[...], aup_ref[...])
    y = y_ref[...] + dskip_ref[...] * u_ref[...]
    z = 0.5 * y * (1.0 + jnp.tanh(math.sqrt(2.0 / math.pi) * (y + 0.044715 * (y * y * y))))
    gate = _dot(z.astype(BF16), wglu_ref[...]) + bglu_ref[...]
    s = _dot((z * _sigmoid(gate)).astype(BF16), sup_ref[...])
    mixed = _sigmoid(ga_ref[...]) * a + _sigmoid(gs_ref[...]) * s
    r = alpha * x_ref[...] + _dot(mixed.astype(BF16), wo_ref[...])
    o_ref[...] = _layer_norm(r, g_ref[...], b_ref[...])


def _merge(alpha, attn, y5, u, g_a, g_s, x2, aup, wglu, sup, wo, dskip, bglu, ln_g, ln_b, tm=256):
    t, d = x2.shape
    row = lambda i: (i, 0)
    full = lambda a: pl.BlockSpec(a.shape, lambda i: (0, 0))
    return pl.pallas_call(
        functools.partial(_merge_kernel, alpha),
        grid=(t // tm,),
        in_specs=[pl.BlockSpec((tm, ATTN_WIDTH), row), pl.BlockSpec((tm, S5_WIDTH), row),
                  pl.BlockSpec((tm, S5_WIDTH), row), pl.BlockSpec((tm, d), row), pl.BlockSpec((tm, d), row),
                  pl.BlockSpec((tm, d), row), full(aup), full(wglu), full(sup), full(wo),
                  full(dskip), full(bglu), full(ln_g), full(ln_b)],
        out_specs=pl.BlockSpec((tm, d), row),
        out_shape=jax.ShapeDtypeStruct((t, d), F32),
        compiler_params=_cparams(("parallel",)),
        name="merge_ln1",
    )(attn, y5, u, g_a, g_s, x2, aup, wglu, sup, wo, dskip, bglu, ln_g, ln_b)


def _router_kernel(x_ref, wh_ref, wl_ref, bias_ref, idx_ref, wt_ref, rank_ref, cnt_ref, carry_ref):
    tm = x_ref.shape[0]

    @pl.when(pl.program_id(0) == 0)
    def _():
        carry_ref[...] = jnp.zeros(carry_ref.shape, F32)

    x = x_ref[...]
    xh = x.astype(BF16)
    xl = (x - xh.astype(F32)).astype(BF16)
    wh = wh_ref[...]
    logits = _dot_nt(wh, xh) + _dot_nt(wh, xl) + _dot_nt(wl_ref[...], xh)
    scores = _sigmoid(logits)
    biased = scores + bias_ref[...]
    gs_rows = GROUP_SIZE
    row_g = lax.broadcasted_iota(I32, (gs_rows, tm), 0)
    group_score = []
    for g in range(N_EXPERT_GROUPS):
        vg = biased[g * gs_rows:(g + 1) * gs_rows, :]
        m1 = jnp.max(vg, axis=0, keepdims=True)
        i1 = jnp.min(jnp.where(vg == m1, row_g, gs_rows), axis=0, keepdims=True)
        m2 = jnp.max(jnp.where(row_g == i1, REMOVED, vg), axis=0, keepdims=True)
        group_score.append(m1 + m2)
    pieces = []
    for g in range(N_EXPERT_GROUPS):
        beats = jnp.zeros((1, tm), I32)
        for g2 in range(N_EXPERT_GROUPS):
            if g2 == g:
                continue
            better = group_score[g2] > group_score[g]
            if g2 < g:
                better = better | (group_score[g2] == group_score[g])
            beats = beats + better.astype(I32)
        keep = beats < TOPK_GROUPS
        pieces.append(jnp.where(keep, biased[g * gs_rows:(g + 1) * gs_rows, :], NEG_INF))
    v = jnp.concatenate(pieces, axis=0)
    row = lax.broadcasted_iota(I32, (N_EXPERTS, tm), 0)
    idxs, wts = [], []
    multi_hot = jnp.zeros((N_EXPERTS, tm), F32)
    for _ in range(TOP_K):
        m = jnp.max(v, axis=0, keepdims=True)
        ix = jnp.min(jnp.where(v == m, row, N_EXPERTS), axis=0, keepdims=True)
        hit = row == ix
        idxs.append(ix)
        wts.append(jnp.sum(jnp.where(hit, scores, 0.0), axis=0, keepdims=True))
        multi_hot = multi_hot + hit.astype(F32)
        v = jnp.where(hit, REMOVED, v)
    wsum = wts[0]
    for k in range(1, TOP_K):
        wsum = wsum + wts[k]
    s_ix = lax.broadcasted_iota(I32, (tm, tm), 0)
    t_ix = lax.broadcasted_iota(I32, (tm, tm), 1)
    mh = multi_hot.astype(BF16)
    before = _dot(mh, (s_ix < t_ix).astype(BF16)) + carry_ref[...]
    carry_ref[...] = carry_ref[...] + _dot(mh, jnp.ones((tm, tm), BF16))
    for k in range(TOP_K):
        idx_ref[k:k + 1, :] = idxs[k]
        wt_ref[k:k + 1, :] = wts[k] / wsum * ROUTED_SCALE
        rank_ref[k:k + 1, :] = jnp.sum(jnp.where(row == idxs[k], before, 0.0), axis=0, keepdims=True).astype(I32)
    cnt_ref[...] = carry_ref[:, :LANES]


def _router(x1, wr_hi, wr_lo, bias_b, tm=256):
    t, d = x1.shape
    e = wr_hi.shape[0]
    tok = lambda i: (0, i)
    const = lambda i: (0, 0)
    return pl.pallas_call(
        _router_kernel,
        grid=(t // tm,),
        in_specs=[pl.BlockSpec((tm, d), lambda i: (i, 0)), pl.BlockSpec((e, d), const),
                  pl.BlockSpec((e, d), const), pl.BlockSpec((e, tm), const)],
        out_specs=[pl.BlockSpec((TOP_K, tm), tok), pl.BlockSpec((TOP_K, tm), tok),
                   pl.BlockSpec((TOP_K, tm), tok), pl.BlockSpec((e, LANES), const)],
        out_shape=[jax.ShapeDtypeStruct((TOP_K, t), I32), jax.ShapeDtypeStruct((TOP_K, t), F32),
                   jax.ShapeDtypeStruct((TOP_K, t), I32), jax.ShapeDtypeStruct((e, LANES), F32)],
        scratch_shapes=[pltpu.VMEM((e, tm), F32)],
        compiler_params=_cparams(("arbitrary",)),
        name="router_topk_rank",
    )(x1, wr_hi, wr_lo, bias_b)


def _dest_kernel(idx_ref, rank_ref, pstart_ref, dest_ref):
    pstart = pstart_ref[...]
    row = lax.broadcasted_iota(I32, pstart.shape, 0)
    for k in range(TOP_K):
        start = jnp.sum(jnp.where(row == idx_ref[k:k + 1, :], pstart, 0.0), axis=0, keepdims=True)
        dest_ref[k:k + 1, :] = start.astype(I32) + rank_ref[k:k + 1, :]


def _dest_rows(top_idx, rank, pstart_b):
    tm = pstart_b.shape[1]
    tok = lambda i: (0, i)
    return pl.pallas_call(
        _dest_kernel,
        grid=(top_idx.shape[1] // tm,),
        in_specs=[pl.BlockSpec((TOP_K, tm), tok), pl.BlockSpec((TOP_K, tm), tok),
                  pl.BlockSpec(pstart_b.shape, lambda i: (0, 0))],
        out_specs=pl.BlockSpec((TOP_K, tm), tok),
        out_shape=jax.ShapeDtypeStruct(top_idx.shape, I32),
        compiler_params=_cparams(("parallel",)),
        name="moe_dest_rows",
    )(top_idx, rank, pstart_b)


def _rows_wait(src_hbm, dst, sem, n_rows):
    pltpu.make_async_copy(src_hbm.at[pl.ds(0, n_rows), :], dst, sem).wait()


def _dispatch_kernel(pend_ref, cnt_ref, dest_ref, x_ref, xs_hbm, zbuf, zsem, sem):
    i = pl.program_id(0)
    tm = dest_ref.shape[1]
    mb = zbuf.shape[0]

    @pl.when(i == 0)
    def _():
        zbuf[...] = jnp.zeros(zbuf.shape, zbuf.dtype)

        def zero_copy(e):
            start = pl.multiple_of(pend_ref[e] - mb, mb)
            return pltpu.make_async_copy(zbuf, xs_hbm.at[pl.ds(start, mb), :], zsem)

        def z_start(e, c):
            @pl.when(cnt_ref[e] > 0)
            def _():
                zero_copy(e).start()
            return c

        def z_wait(e, c):
            @pl.when(cnt_ref[e] > 0)
            def _():
                zero_copy(e).wait()
            return c

        def tail_copy(b):
            return pltpu.make_async_copy(zbuf, xs_hbm.at[pl.ds(pl.multiple_of(b * mb, mb), mb), :], zsem)

        def t_start(b, c):
            tail_copy(b).start()
            return c

        def t_wait(b, c):
            tail_copy(b).wait()
            return c

        n_used = pend_ref[N_EXPERTS - 1] // mb
        n_blocks = xs_hbm.shape[0] // mb
        lax.fori_loop(0, N_EXPERTS, z_start, 0)
        lax.fori_loop(n_used, n_blocks, t_start, 0)
        lax.fori_loop(0, N_EXPERTS, z_wait, 0)
        lax.fori_loop(n_used, n_blocks, t_wait, 0)

    def issue(t, c):
        src = x_ref.at[pl.ds(t, 1), :]
        for k in range(TOP_K):
            pltpu.make_async_copy(src, xs_hbm.at[pl.ds(dest_ref[k, t], 1), :], sem).start()
        return c

    lax.fori_loop(0, tm, issue, 0, unroll=2)
    for _ in range(TOP_K):
        pltpu.make_async_copy(x_ref, xs_hbm.at[pl.ds(0, tm), :], sem).wait()


def _dispatch(padded_ends, counts, dest, x1, n_rows, tm=1024):
    t, d = x1.shape
    grid_spec = pltpu.PrefetchScalarGridSpec(
        num_scalar_prefetch=2,
        grid=(t // tm,),
        in_specs=[pl.BlockSpec((TOP_K, tm), lambda i, pe, cn: (0, i), memory_space=pltpu.SMEM),
                  pl.BlockSpec((tm, d), lambda i, pe, cn: (i, 0))],
        out_specs=pl.BlockSpec(memory_space=pl.ANY),
        scratch_shapes=[pltpu.VMEM((MOE_ROWS, d), x1.dtype), pltpu.SemaphoreType.DMA(()),
                        pltpu.SemaphoreType.DMA(())],
    )
    return pl.pallas_call(
        _dispatch_kernel,
        grid_spec=grid_spec,
        out_shape=jax.ShapeDtypeStruct((n_rows, d), x1.dtype),
        compiler_params=_cparams(("arbitrary",)),
        name="moe_dispatch",
    )(padded_ends, counts, dest, x1)


def _expert_kernel(first_ref, nblk_ref, nused_ref, xs_hbm, wg_ref, wu_ref, wd_ref, y_hbm,
                   xbuf, ybuf, wgb, wub, wdb, xsem, ysem):
    e = pl.program_id(0)
    n_used = nused_ref[0]
    mb = xbuf.shape[1]

    def x_copy(g, slot):
        return pltpu.make_async_copy(xs_hbm.at[pl.ds(pl.multiple_of(g * mb, mb), mb), :], xbuf.at[slot],
                                     xsem.at[slot])

    def y_copy(g, slot):
        return pltpu.make_async_copy(ybuf.at[slot], y_hbm.at[pl.ds(pl.multiple_of(g * mb, mb), mb), :],
                                     ysem.at[slot])

    @pl.when(e == 0)
    def _():
        x_copy(0, 0).start()

    wgb[...] = wg_ref[0, 0].astype(BF16)
    wub[...] = wu_ref[0, 0].astype(BF16)
    wdb[...] = wd_ref[0, 0].astype(BF16)

    def block(b, c):
        g = first_ref[e] + b
        slot = g % 2

        @pl.when(g + 1 < n_used)
        def _():
            x_copy(g + 1, 1 - slot).start()

        x_copy(g, slot).wait()

        @pl.when(g >= 2)
        def _():
            y_copy(g - 2, slot).wait()

        xb = xbuf[slot].astype(BF16)
        hg = _dot(xb, wgb[...])
        hu = _dot(xb, wub[...])
        h = hg * _sigmoid(hg) * hu
        ybuf[slot] = _dot(h.astype(BF16), wdb[...])
        y_copy(g, slot).start()
        return c

    lax.fori_loop(0, nblk_ref[e], block, 0)

    @pl.when(e == pl.num_programs(0) - 1)
    def _():
        @pl.when(n_used >= 2)
        def _():
            y_copy(n_used - 2, n_used % 2).wait()

        y_copy(n_used - 1, (n_used - 1) % 2).wait()
        ybuf[0] = jnp.zeros(ybuf.shape[1:], ybuf.dtype)
        n_blocks = y_hbm.shape[0] // mb

        def z_start(g, c):
            y_copy(g, 0).start()
            return c

        def z_wait(g, c):
            y_copy(g, 0).wait()
            return c

        lax.fori_loop(n_used, n_blocks, z_start, 0)
        lax.fori_loop(n_used, n_blocks, z_wait, 0)


def _experts(layer, first_block, n_blk, n_used, xs, w_gate_e, w_up_e, w_down_e):
    mb = MOE_ROWS
    n_rows, d = xs.shape
    _, n_exp, _, hdim = w_gate_e.shape
    wmap = lambda e, fb, nb, nu: (layer, e, 0, 0)
    grid_spec = pltpu.PrefetchScalarGridSpec(
        num_scalar_prefetch=3,
        grid=(n_exp,),
        in_specs=[pl.BlockSpec(memory_space=pl.ANY),
                  pl.BlockSpec((1, 1, d, hdim), wmap),
                  pl.BlockSpec((1, 1, d, hdim), wmap),
                  pl.BlockSpec((1, 1, hdim, d), wmap)],
        out_specs=pl.BlockSpec(memory_space=pl.ANY),
        scratch_shapes=[pltpu.VMEM((2, mb, d), xs.dtype), pltpu.VMEM((2, mb, d), F32),
                        pltpu.VMEM((d, hdim), BF16), pltpu.VMEM((d, hdim), BF16), pltpu.VMEM((hdim, d), BF16),
                        pltpu.SemaphoreType.DMA((2,)), pltpu.SemaphoreType.DMA((2,))],
    )
    return pl.pallas_call(
        _expert_kernel,
        grid_spec=grid_spec,
        out_shape=jax.ShapeDtypeStruct((n_rows, d), F32),
        compiler_params=_cparams(("arbitrary",)),
        name="routed_experts",
    )(first_block, n_blk, n_used, xs, w_gate_e, w_up_e, w_down_e)


def _final_kernel(alpha, dcur_ref, dnext_ref, x_ref, wt_ref, y_hbm, wsg_ref, wsu_ref, wsd_ref, g_ref, b_ref,
                  o_ref, ybuf, sems):
    i = pl.program_id(0)
    n_steps = pl.num_programs(0)
    tm = x_ref.shape[0]
    slot = i % 2

    def gather(dref, s):
        def issue(t, c):
            for k in range(TOP_K):
                pltpu.make_async_copy(y_hbm.at[pl.ds(dref[k, t], 1), :], ybuf.at[s, k, pl.ds(t, 1), :],
                                      sems.at[s]).start()
            return c

        lax.fori_loop(0, tm, issue, 0, unroll=2)

    @pl.when(i == 0)
    def _():
        gather(dcur_ref, 0)

    @pl.when(i + 1 < n_steps)
    def _():
        gather(dnext_ref, 1 - slot)

    x = x_ref[...]
    xb = x.astype(BF16)
    hg = _dot(xb, wsg_ref[...])
    hu = _dot(xb, wsu_ref[...])
    f = _dot((hg * _sigmoid(hg) * hu).astype(BF16), wsd_ref[...])
    for k in range(TOP_K):
        _rows_wait(y_hbm, ybuf.at[slot, k], sems.at[slot], tm)
    wt = wt_ref[...]
    for k in range(TOP_K):
        f = f + wt[:, k:k + 1] * ybuf[slot, k]
    o_ref[...] = _layer_norm(alpha * x + f, g_ref[...], b_ref[...])


def _final(alpha, dest, x1, wt_tok, y_sorted, wsg, wsu, wsd, ln_g, ln_b, tm=256):
    t, d = x1.shape
    n = t // tm
    row = lambda i: (i, 0)
    full = lambda a: pl.BlockSpec(a.shape, lambda i: (0, 0))
    return pl.pallas_call(
        functools.partial(_final_kernel, alpha),
        grid=(n,),
        in_specs=[pl.BlockSpec((TOP_K, tm), lambda i: (0, i), memory_space=pltpu.SMEM),
                  pl.BlockSpec((TOP_K, tm), lambda i: (0, jnp.minimum(i + 1, n - 1)), memory_space=pltpu.SMEM),
                  pl.BlockSpec((tm, d), row), pl.BlockSpec((tm, TOP_K), row),
                  pl.BlockSpec(memory_space=pl.ANY), full(wsg), full(wsu), full(wsd), full(ln_g), full(ln_b)],
        out_specs=pl.BlockSpec((tm, d), row),
        out_shape=jax.ShapeDtypeStruct((t, d), F32),
        scratch_shapes=[pltpu.VMEM((2, TOP_K, tm, d), F32), pltpu.SemaphoreType.DMA((2,))],
        compiler_params=_cparams(("arbitrary",)),
        name="combine_shared_ln2",
    )(dest, dest, x1, wt_tok, y_sorted, wsg, wsu, wsd, ln_g, ln_b)


def _block_plan(counts, n_tok):
    mb = MOE_ROWS
    n_blocks = -(-(n_tok * TOP_K + N_EXPERTS * (mb - 1)) // mb)
    padded = (counts + mb - 1) // mb * mb
    padded_ends = jnp.cumsum(padded).astype(I32)
    padded_starts = padded_ends - padded
    n_used = (padded_ends[-1] // mb).astype(I32).reshape(1)
    return padded_starts, padded_ends, padded_starts // mb, padded // mb, n_used, n_blocks * mb


def kernel(x, positions, w_in, attn_up, lam_re, lam_im, log_dt, b_re, b_im, c_re, c_im, d_skip, w_glu, b_glu,
           s5_up, w_o, ln1_g, ln1_b, w_router, router_bias, w_gate_e, w_up_e, w_down_e, ws_gate, ws_up, ws_down,
           ln2_g, ln2_b):
    bsz, seq, d = x.shape
    depth = w_in.shape[0]
    t = bsz * seq
    alpha = (2 * depth) ** 0.25
    assert seq % MOBA_BLOCK == 0 and seq % S5_CHUNK == 0 and t % 512 == 0

    inv_freq = ROPE_THETA ** (-jnp.arange(0, HEAD_DIM, 2, dtype=F32) / HEAD_DIM)
    ang = positions.astype(F32).reshape(t, 1) * inv_freq
    cos, sin = jnp.cos(ang), jnp.sin(ang)
    cos_t = jnp.concatenate([cos, cos, cos, cos], axis=1)
    sin_t = jnp.concatenate([-sin, sin, -sin, sin], axis=1)

    xc = x.reshape(t, d)
    row2 = lambda a: a.reshape(1, -1).astype(F32)
    for l in range(depth):
        w_v_t = w_in[l][:, 2 * ATTN_WIDTH:3 * ATTN_WIDTH].T.astype(BF16)
        q, k, v_t, u, g_a, g_s = _proj(xc, w_in[l].astype(BF16), w_v_t, cos_t, sin_t)
        attn = _attention(q, k, v_t, bsz, seq)
        tables = _s5_tables(lam_re[l], lam_im[l], log_dt[l], b_re[l], b_im[l], c_re[l], c_im[l])
        y5 = _s5_branch(u, tables, bsz, seq)
        x1 = _merge(alpha, attn, y5, u, g_a, g_s, xc, attn_up[l].astype(BF16), w_glu[l].astype(BF16),
                    s5_up[l].astype(BF16), w_o[l].astype(BF16), row2(d_skip[l]), row2(b_glu[l]),
                    row2(ln1_g[l]), row2(ln1_b[l]))
        wr_t = w_router[l].astype(F32).T
        wr_hi = wr_t.astype(BF16)
        wr_lo = (wr_t - wr_hi.astype(F32)).astype(BF16)
        bias_b = jnp.broadcast_to(router_bias[l].astype(F32)[:, None], (N_EXPERTS, 256))
        top_idx, top_w, rank, counts = _router(x1, wr_hi, wr_lo, bias_b)
        counts = counts[:, 0].astype(I32)
        pstart, pend, first_block, n_blk, n_used, n_rows = _block_plan(counts, t)
        dest = _dest_rows(top_idx, rank, jnp.broadcast_to(pstart.astype(F32)[:, None], (N_EXPERTS, 512)))
        xs = _dispatch(pend, counts, dest, x1, n_rows)
        y_sorted = _experts(l, first_block, n_blk, n_used, xs, w_gate_e, w_up_e, w_down_e)
        xc = _final(alpha, dest, x1, top_w.T, y_sorted, ws_gate[l].astype(BF16), ws_up[l].astype(BF16),
                    ws_down[l].astype(BF16), row2(ln2_g[l]), row2(ln2_b[l]))
    return xc.reshape(bsz, seq, d)
```

```python
import functools
import math

import jax
import jax.numpy as jnp
from jax import lax
from jax.experimental import pallas as pl
from jax.experimental.pallas import tpu as pltpu

F32 = jnp.float32
BF16 = jnp.bfloat16
I32 = jnp.int32

N_HEADS = 8
HEAD_DIM = 64
ATTN_WIDTH = N_HEADS * HEAD_DIM
MOBA_BLOCK = 256
MOBA_TOPK = 3
ROPE_THETA = 10000.0
S5_GROUP = 16
S5_GROUPS = 32
S5_WIDTH = S5_GROUP * S5_GROUPS
S5_STATE = 64
S5_CHUNK = 16
N_EXPERTS = 256
TOP_K = 8
N_EXPERT_GROUPS = 8
TOPK_GROUPS = 4
GROUP_SIZE = N_EXPERTS // N_EXPERT_GROUPS
ROUTED_SCALE = 2.5
LN_EPS = 1e-5
NEG_INF = -1e30
REMOVED = -3e38
QK_SCALE = HEAD_DIM ** -0.5 * math.log2(math.e)
V_ROWS = HEAD_DIM + 16

LANES = 128
MOE_ROWS = 256
EXPERT_IN_SLOTS = 4
VMEM_LIMIT = 56 * 1024 * 1024


def _cparams(sem):
    return pltpu.CompilerParams(dimension_semantics=sem, vmem_limit_bytes=VMEM_LIMIT)


def _sigmoid(x):
    return 1.0 / (1.0 + jnp.exp(-x))


def _layer_norm(r, g, b):
    mu = jnp.mean(r, axis=-1, keepdims=True)
    rc = r - mu
    var = jnp.mean(rc * rc, axis=-1, keepdims=True)
    return rc * lax.rsqrt(var + LN_EPS) * g + b


def _dot(a, b):
    return jnp.dot(a, b, preferred_element_type=F32)


def _dot_nt(a, b):
    return lax.dot_general(a, b, (((1,), (1,)), ((), ())), preferred_element_type=F32)


def _proj_kernel(x_ref, w_ref, wvt_ref, cos_ref, sin_ref, q_ref, k_ref, v_ref, u_ref, ga_ref, gs_ref):
    xb = x_ref[...].astype(BF16)
    cos = cos_ref[...]
    sin = sin_ref[...]
    lane = lax.broadcasted_iota(I32, cos.shape, 1)
    first_half = (lane % HEAD_DIM) < (HEAD_DIM // 2)

    def rope(t):
        rot = jnp.where(first_half, pltpu.roll(t, LANES - HEAD_DIM // 2, axis=1),
                        pltpu.roll(t, HEAD_DIM // 2, axis=1))
        return t * cos + rot * sin

    aw = ATTN_WIDTH
    for c in range(aw // LANES):
        sl = slice(c * LANES, (c + 1) * LANES)
        qc = _dot(xb, w_ref[:, c * LANES:(c + 1) * LANES])
        q_ref[:, sl] = (rope(qc) * QK_SCALE).astype(BF16)
        kc = _dot(xb, w_ref[:, aw + c * LANES:aw + (c + 1) * LANES])
        k_ref[:, sl] = rope(kc).astype(BF16)
    vt = _dot_nt(wvt_ref[...], xb).astype(BF16)
    ones = jnp.ones((V_ROWS - HEAD_DIM, MOBA_BLOCK), BF16)
    for p in range(v_ref.shape[0]):
        for c in range(v_ref.shape[1]):
            cols = slice(c * MOBA_BLOCK, (c + 1) * MOBA_BLOCK)
            heads = [vt[(2 * p + hh) * HEAD_DIM:(2 * p + hh + 1) * HEAD_DIM, cols] for hh in range(2)]
            v_ref[p, c] = jnp.concatenate([heads[0], ones, heads[1], ones], axis=0)
    u_ref[...] = _dot(xb, w_ref[:, 3 * aw:3 * aw + S5_WIDTH])
    d = ga_ref.shape[1]
    off = 3 * aw + S5_WIDTH
    ga_ref[...] = _dot(xb, w_ref[:, off:off + d])
    gs_ref[...] = _dot(xb, w_ref[:, off + d:off + 2 * d])


def _proj(x2, w_bf, wvt_bf, cos_t, sin_t, tm=512):
    t, d = x2.shape
    n = w_bf.shape[1]
    aw = ATTN_WIDTH
    row = lambda i: (i, 0)
    vshape = (aw // LANES, t // MOBA_BLOCK, 2 * V_ROWS, MOBA_BLOCK)
    return pl.pallas_call(
        _proj_kernel,
        grid=(t // tm,),
        in_specs=[pl.BlockSpec((tm, d), row),
                  pl.BlockSpec((d, n), lambda i: (0, 0)),
                  pl.BlockSpec((aw, d), lambda i: (0, 0)),
                  pl.BlockSpec((tm, LANES), row),
                  pl.BlockSpec((tm, LANES), row)],
        out_specs=[pl.BlockSpec((tm, aw), row), pl.BlockSpec((tm, aw), row),
                   pl.BlockSpec((vshape[0], tm // MOBA_BLOCK) + vshape[2:], lambda i: (0, i, 0, 0)),
                   pl.BlockSpec((tm, S5_WIDTH), row), pl.BlockSpec((tm, d), row), pl.BlockSpec((tm, d), row)],
        out_shape=[jax.ShapeDtypeStruct((t, aw), BF16), jax.ShapeDtypeStruct((t, aw), BF16),
                   jax.ShapeDtypeStruct(vshape, BF16), jax.ShapeDtypeStruct((t, S5_WIDTH), F32),
                   jax.ShapeDtypeStruct((t, d), F32), jax.ShapeDtypeStruct((t, d), F32)],
        compiler_params=_cparams(("parallel",)),
        name="proj_rope",
    )(x2, w_bf, wvt_bf, cos_t, sin_t)


def _attn_kernel(q_ref, k_ref, v_ref, o_ref, kmean_ref):
    i = pl.program_id(1)
    blk = MOBA_BLOCK
    n_blocks = k_ref.shape[0] // blk
    n_pairs = q_ref.shape[1] // LANES

    @pl.when(i == 0)
    def _():
        kmean_ref[...] = jnp.zeros(kmean_ref.shape, kmean_ref.dtype)
        for j in range(n_blocks):
            kj = k_ref[j * blk:(j + 1) * blk, :].astype(F32)
            kmean_ref[j:j + 1, :] = jnp.sum(kj, axis=0, keepdims=True) * (1.0 / blk)

    q_t = q_ref[...].astype(F32).T.astype(BF16)
    sub = lax.broadcasted_iota(I32, (LANES, blk), 0)
    key = lax.broadcasted_iota(I32, (blk, blk), 0)
    qry = lax.broadcasted_iota(I32, (blk, blk), 1)
    causal = key <= qry
    own = pl.ds(pl.multiple_of(i * blk, blk), blk)

    heads = [(pr, hh) for pr in range(n_pairs) for hh in range(2)]
    blk_row = lax.broadcasted_iota(I32, (HEAD_DIM, blk), 0)
    q_heads, gates, own_scores = [], [], []
    for pr, hh in heads:
        cols = slice(pr * LANES, (pr + 1) * LANES)
        q_pair = q_t[pr * LANES:(pr + 1) * LANES]
        in_head = (sub >= hh * HEAD_DIM) & (sub < (hh + 1) * HEAD_DIM)
        qh = jnp.where(in_head, q_pair, jnp.zeros_like(q_pair))
        q_heads.append(qh)
        gates.append(_dot(kmean_ref[:HEAD_DIM, cols].astype(BF16), qh))
        own_scores.append(_dot(k_ref[own, cols], qh))
    q_aug, own_probs = [], []
    for n, (pr, hh) in enumerate(heads):
        g = jnp.where(blk_row < i, gates[n], NEG_INF)
        selected = jnp.zeros(g.shape, jnp.bool_)
        for _ in range(MOBA_TOPK):
            mx = jnp.max(g, axis=0, keepdims=True)
            first = jnp.min(jnp.where(g == mx, blk_row, HEAD_DIM), axis=0, keepdims=True)
            pick = blk_row == first
            selected = selected | (pick & (blk_row < i))
            g = jnp.where(pick, REMOVED, g)
        sel_neg = jnp.where(selected, 0.0, NEG_INF).astype(BF16)
        q_own = q_heads[n][hh * HEAD_DIM:(hh + 1) * HEAD_DIM]
        q_aug.append(jnp.concatenate([q_own, sel_neg] if hh == 0 else [sel_neg, q_own], axis=0))
        s = jnp.where(causal, own_scores[n], NEG_INF)
        mx = jnp.max(s, axis=0, keepdims=True)
        own_probs.append((mx, jnp.exp2(s - mx).astype(BF16)))
    init = []
    for n, (pr, hh) in enumerate(heads):
        mx, p = own_probs[n]
        init.append((mx, _dot(v_ref[pr, i, hh * V_ROWS:(hh + 1) * V_ROWS, :], p)))

    lane2 = lax.broadcasted_iota(I32, (2 * blk, LANES), 1)
    second = (lax.broadcasted_iota(I32, (2 * blk, LANES), 0) >= blk).astype(I32)

    def body(jj, carry):
        rows = pl.ds(pl.multiple_of(jj * (2 * blk), 2 * blk), 2 * blk)
        blk_id = 2 * jj + second
        scores = []
        for n, (pr, hh) in enumerate(heads):
            k_pair = k_ref[rows, pr * LANES:(pr + 1) * LANES]
            other = (1 - hh) * HEAD_DIM
            in_other = (lane2 >= other) & (lane2 < other + HEAD_DIM)
            k_aug = jnp.where(in_other, (lane2 == other + blk_id).astype(BF16), k_pair)
            scores.append(_dot(k_aug, q_aug[n]))
        probs = []
        for n in range(len(heads)):
            m = carry[n][0]
            m_new = jnp.maximum(m, jnp.max(scores[n], axis=0, keepdims=True))
            probs.append((m_new, jnp.exp2(m - m_new), jnp.exp2(scores[n] - m_new).astype(BF16)))
        out = []
        for n, (pr, hh) in enumerate(heads):
            m_new, alpha, p = probs[n]
            vrows = slice(hh * V_ROWS, (hh + 1) * V_ROWS)
            pv = _dot(v_ref[pr, 2 * jj, vrows, :], p[:blk]) + _dot(v_ref[pr, 2 * jj + 1, vrows, :], p[blk:])
            out.append((m_new, alpha * carry[n][1] + pv))
        return tuple(out)

    res = lax.fori_loop(0, (i + 1) // 2, body, tuple(init))
    o_t = jnp.concatenate([acc[:HEAD_DIM] / acc[HEAD_DIM:HEAD_DIM + 1] for _, acc in res], axis=0)
    o_ref[...] = o_t.T.astype(BF16)


def _attention(q, k, v_t, bsz, seq):
    t, aw = q.shape
    blk = MOBA_BLOCK
    nb = seq // blk
    assert nb <= HEAD_DIM
    assert nb % 2 == 0
    qmap = lambda b, i: (b * nb + i, 0)
    return pl.pallas_call(
        _attn_kernel,
        grid=(bsz, nb),
        in_specs=[pl.BlockSpec((blk, aw), qmap),
                  pl.BlockSpec((seq, aw), lambda b, i: (b, 0)),
                  pl.BlockSpec((aw // LANES, nb, 2 * V_ROWS, blk), lambda b, i: (0, b, 0, 0))],
        out_specs=pl.BlockSpec((blk, aw), qmap),
        out_shape=jax.ShapeDtypeStruct((t, aw), BF16),
        scratch_shapes=[pltpu.VMEM((LANES, aw), F32)],
        compiler_params=_cparams(("parallel", "arbitrary")),
        name="moba_attention",
    )(q, k, v_t)


def _s5_tables(lam_re, lam_im, log_dt, b_re, b_im, c_re, c_im):
    hp = lax.Precision.HIGHEST
    L = S5_CHUNK
    lr, li = lam_re.astype(F32), lam_im.astype(F32)
    dt = jnp.exp(log_dt.astype(F32))[:, None]
    mag = jnp.exp(lr * dt)
    a_re = mag * jnp.cos(li * dt)
    a_im = mag * jnp.sin(li * dt)
    den = lr * lr + li * li
    nr = a_re - 1.0
    coef_re = (nr * lr + a_im * li) / den
    coef_im = (a_im * lr - nr * li) / den
    br, bi = b_re.astype(F32), b_im.astype(F32)
    bbar_re = coef_re[..., None] * br - coef_im[..., None] * bi
    bbar_im = coef_re[..., None] * bi + coef_im[..., None] * br
    tau = jnp.arange(L + 1, dtype=F32)[:, None, None]
    pmag = jnp.exp(tau * (lr * dt))
    p_re = pmag * jnp.cos(tau * (li * dt))
    p_im = pmag * jnp.sin(tau * (li * dt))
    cr, ci = c_re.astype(F32), c_im.astype(F32)
    ca_re = cr[None] * p_re[:, :, None, :] - ci[None] * p_im[:, :, None, :]
    ca_im = cr[None] * p_im[:, :, None, :] + ci[None] * p_re[:, :, None, :]
    kmat = (jnp.einsum('tgon,gni->tgio', ca_re[:L], bbar_re, precision=hp)
            - jnp.einsum('tgon,gni->tgio', ca_im[:L], bbar_im, precision=hp))
    s_ix = jnp.arange(L)[:, None]
    t_ix = jnp.arange(L)[None, :]
    lag = jnp.clip(t_ix - s_ix, 0, L - 1)
    m = kmat[lag]
    m = jnp.where((t_ix >= s_ix)[:, :, None, None, None], m, 0.0)
    m_intra = m.transpose(2, 0, 3, 1, 4).reshape(S5_GROUPS, L * S5_GROUP, L * S5_GROUP)
    rev = p_re[L - 1 - jnp.arange(L)], p_im[L - 1 - jnp.arange(L)]
    w_re = rev[0][..., None] * bbar_re[None] - rev[1][..., None] * bbar_im[None]
    w_im = rev[0][..., None] * bbar_im[None] + rev[1][..., None] * bbar_re[None]
    w_re = w_re.transpose(1, 0, 3, 2).reshape(S5_GROUPS, L * S5_GROUP, S5_STATE)
    w_im = w_im.transpose(1, 0, 3, 2).reshape(S5_GROUPS, L * S5_GROUP, S5_STATE)
    w_state = jnp.concatenate([w_re, w_im, w_im, w_re], axis=-1)
    cx_re = ca_re[1:].transpose(1, 3, 0, 2).reshape(S5_GROUPS, S5_STATE, L * S5_GROUP)
    cx_im = ca_im[1:].transpose(1, 3, 0, 2).reshape(S5_GROUPS, S5_STATE, L * S5_GROUP)
    cx = jnp.concatenate([cx_re, -cx_im], axis=1)
    al_re, al_im = p_re[L], p_im[L]
    a1 = jnp.concatenate([al_re, al_re], axis=-1)
    a2 = jnp.concatenate([-al_im, al_im], axis=-1)
    return m_intra.astype(BF16), w_state.astype(BF16), cx.astype(BF16), a1, a2


def _s5_group_kernel(n_seq, u_ref, w_ref, m_ref, cx_ref, a1_ref, a2_ref, y_ref, e_ref, h_ref):
    u = u_ref[0]
    e_ref[...] = _dot(u, w_ref[0])
    a1 = a1_ref[0]
    a2 = a2_ref[0]
    nk = e_ref.shape[0] // n_seq
    half = h_ref.shape[1]

    def step(k, carry):
        out = []
        for b in range(n_seq):
            hp, hq = carry[b]
            row = pl.ds(b * nk + k, 1)
            h_ref[row, :] = hp
            e = e_ref[row, :]
            out.append((a1 * hp + a2 * hq + e[:, :half], a1 * hq - a2 * hp + e[:, half:]))
        return tuple(out)

    zero = jnp.zeros((1, half), F32)
    lax.fori_loop(0, nk, step, tuple((zero, zero) for _ in range(n_seq)), unroll=4)
    y_ref[0] = _dot(u, m_ref[0]) + _dot(h_ref[...].astype(BF16), cx_ref[0])


def _s5_groups(ug, tables, n_seq):
    m_intra, w_state, cx, a1, a2 = tables
    g, r, c = ug.shape
    n2 = cx.shape[1]
    spec = lambda shape: pl.BlockSpec(shape, lambda i: (i, 0, 0))
    return pl.pallas_call(
        functools.partial(_s5_group_kernel, n_seq),
        grid=(g,),
        in_specs=[spec((1, r, c)), spec((1, c, c)), spec((1, c, c)), spec((1, n2, c)),
                  spec((1, 1, n2)), spec((1, 1, n2))],
        out_specs=spec((1, r, c)),
        out_shape=jax.ShapeDtypeStruct((g, r, c), F32),
        scratch_shapes=[pltpu.VMEM((r, c), F32), pltpu.VMEM((r, n2), F32)],
        compiler_params=_cparams(("parallel",)),
        name="s5_groups",
    )(ug, w_state, m_intra, cx, a1[:, None, :], a2[:, None, :])


def _s5_branch(u, tables, bsz, seq):
    L, G, C = S5_CHUNK, S5_GROUPS, S5_GROUP
    nk = seq // L
    ug = u.reshape(bsz, nk, L, G, C).transpose(3, 0, 1, 2, 4).reshape(G, bsz * nk, L * C).astype(BF16)
    y = _s5_groups(ug, tables, bsz)
    return y.reshape(G, bsz, nk, L, C).transpose(1, 2, 3, 0, 4).reshape(bsz * seq, G * C)


def _merge_kernel(alpha, attn_ref, y_ref, u_ref, ga_ref, gs_ref, x_ref, aup_ref, wglu_ref, sup_ref, wo_ref,
                  dskip_ref, bglu_ref, g_ref, b_ref, o_ref):
    a = _dot(attn_ref[...], aup_ref[...])
    y = y_ref[...] + dskip_ref[...] * u_ref[...]
    z = 0.5 * y * (1.0 + jnp.tanh(math.sqrt(2.0 / math.pi) * (y + 0.044715 * (y * y * y))))
    gate = _dot(z.astype(BF16), wglu_ref[...]) + bglu_ref[...]
    s = _dot((z * _sigmoid(gate)).astype(BF16), sup_ref[...])
    mixed = _sigmoid(ga_ref[...]) * a + _sigmoid(gs_ref[...]) * s
    r = alpha * x_ref[...] + _dot(mixed.astype(BF16), wo_ref[...])
    o_ref[...] = _layer_norm(r, g_ref[...], b_ref[...])


def _merge(alpha, attn, y5, u, g_a, g_s, x2, aup, wglu, sup, wo, dskip, bglu, ln_g, ln_b, tm=256):
    t, d = x2.shape
    row = lambda i: (i, 0)
    full = lambda a: pl.BlockSpec(a.shape, lambda i: (0, 0))
    return pl.pallas_call(
        functools.partial(_merge_kernel, alpha),
        grid=(t // tm,),
        in_specs=[pl.BlockSpec((tm, ATTN_WIDTH), row), pl.BlockSpec((tm, S5_WIDTH), row),
                  pl.BlockSpec((tm, S5_WIDTH), row), pl.BlockSpec((tm, d), row), pl.BlockSpec((tm, d), row),
                  pl.BlockSpec((tm, d), row), full(aup), full(wglu), full(sup), full(wo),
                  full(dskip), full(bglu), full(ln_g), full(ln_b)],
        out_specs=pl.BlockSpec((tm, d), row),
        out_shape=jax.ShapeDtypeStruct((t, d), F32),
        compiler_params=_cparams(("parallel",)),
        name="merge_ln1",
    )(attn, y5, u, g_a, g_s, x2, aup, wglu, sup, wo, dskip, bglu, ln_g, ln_b)


def _router_kernel(x_ref, wh_ref, wl_ref, bias_ref, idx_ref, wt_ref, rank_ref, cnt_ref, carry_ref):
    tm = x_ref.shape[0]

    @pl.when(pl.program_id(0) == 0)
    def _():
        carry_ref[...] = jnp.zeros(carry_ref.shape, F32)

    x = x_ref[...]
    xh = x.astype(BF16)
    xl = (x - xh.astype(F32)).astype(BF16)
    wh = wh_ref[...]
    logits = _dot_nt(wh, xh) + _dot_nt(wh, xl) + _dot_nt(wl_ref[...], xh)
    scores = _sigmoid(logits)
    biased = scores + bias_ref[...]
    gs_rows = GROUP_SIZE
    row_g = lax.broadcasted_iota(I32, (gs_rows, tm), 0)
    group_score = []
    for g in range(N_EXPERT_GROUPS):
        vg = biased[g * gs_rows:(g + 1) * gs_rows, :]
        m1 = jnp.max(vg, axis=0, keepdims=True)
        i1 = jnp.min(jnp.where(vg == m1, row_g, gs_rows), axis=0, keepdims=True)
        m2 = jnp.max(jnp.where(row_g == i1, REMOVED, vg), axis=0, keepdims=True)
        group_score.append(m1 + m2)
    pieces = []
    for g in range(N_EXPERT_GROUPS):
        beats = jnp.zeros((1, tm), I32)
        for g2 in range(N_EXPERT_GROUPS):
            if g2 == g:
                continue
            better = group_score[g2] > group_score[g]
            if g2 < g:
                better = better | (group_score[g2] == group_score[g])
            beats = beats + better.astype(I32)
        keep = beats < TOPK_GROUPS
        pieces.append(jnp.where(keep, biased[g * gs_rows:(g + 1) * gs_rows, :], NEG_INF))
    v = jnp.concatenate(pieces, axis=0)
    row = lax.broadcasted_iota(I32, (N_EXPERTS, tm), 0)
    idxs, wts = [], []
    multi_hot = jnp.zeros((N_EXPERTS, tm), F32)
    for _ in range(TOP_K):
        m = jnp.max(v, axis=0, keepdims=True)
        ix = jnp.min(jnp.where(v == m, row, N_EXPERTS), axis=0, keepdims=True)
        hit = row == ix
        idxs.append(ix)
        wts.append(jnp.sum(jnp.where(hit, scores, 0.0), axis=0, keepdims=True))
        multi_hot = multi_hot + hit.astype(F32)
        v = jnp.where(hit, REMOVED, v)
    wsum = wts[0]
    for k in range(1, TOP_K):
        wsum = wsum + wts[k]
    s_ix = lax.broadcasted_iota(I32, (tm, tm), 0)
    t_ix = lax.broadcasted_iota(I32, (tm, tm), 1)
    mh = multi_hot.astype(BF16)
    before = _dot(mh, (s_ix < t_ix).astype(BF16)) + carry_ref[...]
    carry_ref[...] = carry_ref[...] + _dot(mh, jnp.ones((tm, tm), BF16))
    for k in range(TOP_K):
        idx_ref[k:k + 1, :] = idxs[k]
        wt_ref[k:k + 1, :] = wts[k] / wsum * ROUTED_SCALE
        rank_ref[k:k + 1, :] = jnp.sum(jnp.where(row == idxs[k], before, 0.0), axis=0, keepdims=True).astype(I32)
    cnt_ref[...] = carry_ref[:, :LANES]


def _router(x1, wr_hi, wr_lo, bias_b, tm=256):
    t, d = x1.shape
    e = wr_hi.shape[0]
    tok = lambda i: (0, i)
    const = lambda i: (0, 0)
    return pl.pallas_call(
        _router_kernel,
        grid=(t // tm,),
        in_specs=[pl.BlockSpec((tm, d), lambda i: (i, 0)), pl.BlockSpec((e, d), const),
                  pl.BlockSpec((e, d), const), pl.BlockSpec((e, tm), const)],
        out_specs=[pl.BlockSpec((TOP_K, tm), tok), pl.BlockSpec((TOP_K, tm), tok),
                   pl.BlockSpec((TOP_K, tm), tok), pl.BlockSpec((e, LANES), const)],
        out_shape=[jax.ShapeDtypeStruct((TOP_K, t), I32), jax.ShapeDtypeStruct((TOP_K, t), F32),
                   jax.ShapeDtypeStruct((TOP_K, t), I32), jax.ShapeDtypeStruct((e, LANES), F32)],
        scratch_shapes=[pltpu.VMEM((e, tm), F32)],
        compiler_params=_cparams(("arbitrary",)),
        name="router_topk_rank",
    )(x1, wr_hi, wr_lo, bias_b)


def _dest_kernel(idx_ref, rank_ref, pstart_ref, dest_ref):
    pstart = pstart_ref[...]
    row = lax.broadcasted_iota(I32, pstart.shape, 0)
    for k in range(TOP_K):
        start = jnp.sum(jnp.where(row == idx_ref[k:k + 1, :], pstart, 0.0), axis=0, keepdims=True)
        dest_ref[k:k + 1, :] = start.astype(I32) + rank_ref[k:k + 1, :]


def _dest_rows(top_idx, rank, pstart_b):
    tm = pstart_b.shape[1]
    tok = lambda i: (0, i)
    return pl.pallas_call(
        _dest_kernel,
        grid=(top_idx.shape[1] // tm,),
        in_specs=[pl.BlockSpec((TOP_K, tm), tok), pl.BlockSpec((TOP_K, tm), tok),
                  pl.BlockSpec(pstart_b.shape, lambda i: (0, 0))],
        out_specs=pl.BlockSpec((TOP_K, tm), tok),
        out_shape=jax.ShapeDtypeStruct(top_idx.shape, I32),
        compiler_params=_cparams(("parallel",)),
        name="moe_dest_rows",
    )(top_idx, rank, pstart_b)


def _rows_wait(src_hbm, dst, sem, n_rows):
    pltpu.make_async_copy(src_hbm.at[pl.ds(0, n_rows), :], dst, sem).wait()


def _dispatch_kernel(pend_ref, cnt_ref, dest_ref, x_ref, xs_hbm, zbuf, zsem, sem):
    i = pl.program_id(0)
    tm = dest_ref.shape[1]
    mb = zbuf.shape[0]

    @pl.when(i == 0)
    def _():
        zbuf[...] = jnp.zeros(zbuf.shape, zbuf.dtype)

        def zero_copy(e):
            start = pl.multiple_of(pend_ref[e] - mb, mb)
            return pltpu.make_async_copy(zbuf, xs_hbm.at[pl.ds(start, mb), :], zsem)

        def z_start(e, c):
            @pl.when(cnt_ref[e] > 0)
            def _():
                zero_copy(e).start()
            return c

        def z_wait(e, c):
            @pl.when(cnt_ref[e] > 0)
            def _():
                zero_copy(e).wait()
            return c

        def tail_copy(b):
            return pltpu.make_async_copy(zbuf, xs_hbm.at[pl.ds(pl.multiple_of(b * mb, mb), mb), :], zsem)

        def t_start(b, c):
            tail_copy(b).start()
            return c

        def t_wait(b, c):
            tail_copy(b).wait()
            return c

        n_used = pend_ref[N_EXPERTS - 1] // mb
        n_blocks = xs_hbm.shape[0] // mb
        lax.fori_loop(0, N_EXPERTS, z_start, 0)
        lax.fori_loop(n_used, n_blocks, t_start, 0)
        lax.fori_loop(0, N_EXPERTS, z_wait, 0)
        lax.fori_loop(n_used, n_blocks, t_wait, 0)

    def issue(t, c):
        src = x_ref.at[pl.ds(t, 1), :]
        for k in range(TOP_K):
            pltpu.make_async_copy(src, xs_hbm.at[pl.ds(dest_ref[k, t], 1), :], sem).start()
        return c

    lax.fori_loop(0, tm, issue, 0, unroll=2)
    for _ in range(TOP_K):
        pltpu.make_async_copy(x_ref, xs_hbm.at[pl.ds(0, tm), :], sem).wait()


def _dispatch(padded_ends, counts, dest, x1, n_rows, tm=1024):
    t, d = x1.shape
    grid_spec = pltpu.PrefetchScalarGridSpec(
        num_scalar_prefetch=2,
        grid=(t // tm,),
        in_specs=[pl.BlockSpec((TOP_K, tm), lambda i, pe, cn: (0, i), memory_space=pltpu.SMEM),
                  pl.BlockSpec((tm, d), lambda i, pe, cn: (i, 0))],
        out_specs=pl.BlockSpec(memory_space=pl.ANY),
        scratch_shapes=[pltpu.VMEM((MOE_ROWS, d), x1.dtype), pltpu.SemaphoreType.DMA(()),
                        pltpu.SemaphoreType.DMA(())],
    )
    return pl.pallas_call(
        _dispatch_kernel,
        grid_spec=grid_spec,
        out_shape=jax.ShapeDtypeStruct((n_rows, d), x1.dtype),
        compiler_params=_cparams(("arbitrary",)),
        name="moe_dispatch",
    )(padded_ends, counts, dest, x1)


def _expert_kernel(first_ref, nblk_ref, nused_ref, xs_hbm, wg_ref, wu_ref, wd_ref, y_hbm,
                   xbuf, ybuf, wgb, wub, wdb, xsem, ysem):
    e = pl.program_id(0)
    n_used = nused_ref[0]
    nx, mb = xbuf.shape[0], xbuf.shape[1]

    def x_copy(g, slot):
        return pltpu.make_async_copy(xs_hbm.at[pl.ds(pl.multiple_of(g * mb, mb), mb), :], xbuf.at[slot],
                                     xsem.at[slot])

    def y_copy(g, slot):
        return pltpu.make_async_copy(ybuf.at[slot], y_hbm.at[pl.ds(pl.multiple_of(g * mb, mb), mb), :],
                                     ysem.at[slot])

    @pl.when(e == 0)
    def _():
        for s in range(nx - 1):
            @pl.when(s < n_used)
            def _():
                x_copy(s, s).start()

    wgb[...] = wg_ref[0, 0].astype(BF16)
    wub[...] = wu_ref[0, 0].astype(BF16)
    wdb[...] = wd_ref[0, 0].astype(BF16)

    def block(b, c):
        g = first_ref[e] + b
        slot = g % nx
        yslot = g % 2
        ahead = g + (nx - 1)

        @pl.when(ahead < n_used)
        def _():
            x_copy(ahead, ahead % nx).start()

        x_copy(g, slot).wait()

        @pl.when(g >= 2)
        def _():
            y_copy(g - 2, yslot).wait()

        xb = xbuf[slot].astype(BF16)
        hg = _dot(xb, wgb[...])
        hu = _dot(xb, wub[...])
        h = hg * _sigmoid(hg) * hu
        ybuf[yslot] = _dot(h.astype(BF16), wdb[...])
        y_copy(g, yslot).start()
        return c

    lax.fori_loop(0, nblk_ref[e], block, 0)

    @pl.when(e == pl.num_programs(0) - 1)
    def _():
        @pl.when(n_used >= 2)
        def _():
            y_copy(n_used - 2, n_used % 2).wait()

        y_copy(n_used - 1, (n_used - 1) % 2).wait()
        ybuf[0] = jnp.zeros(ybuf.shape[1:], ybuf.dtype)
        n_blocks = y_hbm.shape[0] // mb

        def z_start(g, c):
            y_copy(g, 0).start()
            return c

        def z_wait(g, c):
            y_copy(g, 0).wait()
            return c

        lax.fori_loop(n_used, n_blocks, z_start, 0)
        lax.fori_loop(n_used, n_blocks, z_wait, 0)


def _experts(layer, first_block, n_blk, n_used, xs, w_gate_e, w_up_e, w_down_e):
    mb = MOE_ROWS
    n_rows, d = xs.shape
    _, n_exp, _, hdim = w_gate_e.shape
    wmap = lambda e, fb, nb, nu: (layer, e, 0, 0)
    grid_spec = pltpu.PrefetchScalarGridSpec(
        num_scalar_prefetch=3,
        grid=(n_exp,),
        in_specs=[pl.BlockSpec(memory_space=pl.ANY),
                  pl.BlockSpec((1, 1, d, hdim), wmap),
                  pl.BlockSpec((1, 1, d, hdim), wmap),
                  pl.BlockSpec((1, 1, hdim, d), wmap)],
        out_specs=pl.BlockSpec(memory_space=pl.ANY),
        scratch_shapes=[pltpu.VMEM((EXPERT_IN_SLOTS, mb, d), xs.dtype), pltpu.VMEM((2, mb, d), F32),
                        pltpu.VMEM((d, hdim), BF16), pltpu.VMEM((d, hdim), BF16), pltpu.VMEM((hdim, d), BF16),
                        pltpu.SemaphoreType.DMA((EXPERT_IN_SLOTS,)), pltpu.SemaphoreType.DMA((2,))],
    )
    return pl.pallas_call(
        _expert_kernel,
        grid_spec=grid_spec,
        out_shape=jax.ShapeDtypeStruct((n_rows, d), F32),
        compiler_params=_cparams(("arbitrary",)),
        name="routed_experts",
    )(first_block, n_blk, n_used, xs, w_gate_e, w_up_e, w_down_e)


def _final_kernel(alpha, dcur_ref, dnext_ref, x_ref, wt_ref, y_hbm, wsg_ref, wsu_ref, wsd_ref, g_ref, b_ref,
                  o_ref, ybuf, sems):
    i = pl.program_id(0)
    n_steps = pl.num_programs(0)
    tm = x_ref.shape[0]
    slot = i % 2

    def gather(dref, s):
        def issue(t, c):
            for k in range(TOP_K):
                pltpu.make_async_copy(y_hbm.at[pl.ds(dref[k, t], 1), :], ybuf.at[s, k, pl.ds(t, 1), :],
                                      sems.at[s]).start()
            return c

        lax.fori_loop(0, tm, issue, 0, unroll=2)

    @pl.when(i == 0)
    def _():
        gather(dcur_ref, 0)

    @pl.when(i + 1 < n_steps)
    def _():
        gather(dnext_ref, 1 - slot)

    x = x_ref[...]
    xb = x.astype(BF16)
    hg = _dot(xb, wsg_ref[...])
    hu = _dot(xb, wsu_ref[...])
    f = _dot((hg * _sigmoid(hg) * hu).astype(BF16), wsd_ref[...])
    for k in range(TOP_K):
        _rows_wait(y_hbm, ybuf.at[slot, k], sems.at[slot], tm)
    wt = wt_ref[...]
    for k in range(TOP_K):
        f = f + wt[:, k:k + 1] * ybuf[slot, k]
    o_ref[...] = _layer_norm(alpha * x + f, g_ref[...], b_ref[...])


def _final(alpha, dest, x1, wt_tok, y_sorted, wsg, wsu, wsd, ln_g, ln_b, tm=256):
    t, d = x1.shape
    n = t // tm
    row = lambda i: (i, 0)
    full = lambda a: pl.BlockSpec(a.shape, lambda i: (0, 0))
    return pl.pallas_call(
        functools.partial(_final_kernel, alpha),
        grid=(n,),
        in_specs=[pl.BlockSpec((TOP_K, tm), lambda i: (0, i), memory_space=pltpu.SMEM),
                  pl.BlockSpec((TOP_K, tm), lambda i: (0, jnp.minimum(i + 1, n - 1)), memory_space=pltpu.SMEM),
                  pl.BlockSpec((tm, d), row), pl.BlockSpec((tm, TOP_K), row),
                  pl.BlockSpec(memory_space=pl.ANY), full(wsg), full(wsu), full(wsd), full(ln_g), full(ln_b)],
        out_specs=pl.BlockSpec((tm, d), row),
        out_shape=jax.ShapeDtypeStruct((t, d), F32),
        scratch_shapes=[pltpu.VMEM((2, TOP_K, tm, d), F32), pltpu.SemaphoreType.DMA((2,))],
        compiler_params=_cparams(("arbitrary",)),
        name="combine_shared_ln2",
    )(dest, dest, x1, wt_tok, y_sorted, wsg, wsu, wsd, ln_g, ln_b)


def _block_plan(counts, n_tok):
    mb = MOE_ROWS
    n_blocks = -(-(n_tok * TOP_K + N_EXPERTS * (mb - 1)) // mb)
    padded = (counts + mb - 1) // mb * mb
    padded_ends = jnp.cumsum(padded).astype(I32)
    padded_starts = padded_ends - padded
    n_used = (padded_ends[-1] // mb).astype(I32).reshape(1)
    return padded_starts, padded_ends, padded_starts // mb, padded // mb, n_used, n_blocks * mb


def kernel(x, positions, w_in, attn_up, lam_re, lam_im, log_dt, b_re, b_im, c_re, c_im, d_skip, w_glu, b_glu,
           s5_up, w_o, ln1_g, ln1_b, w_router, router_bias, w_gate_e, w_up_e, w_down_e, ws_gate, ws_up, ws_down,
           ln2_g, ln2_b):
    bsz, seq, d = x.shape
    depth = w_in.shape[0]
    t = bsz * seq
    alpha = (2 * depth) ** 0.25
    assert seq % MOBA_BLOCK == 0 and seq % S5_CHUNK == 0 and t % 512 == 0

    inv_freq = ROPE_THETA ** (-jnp.arange(0, HEAD_DIM, 2, dtype=F32) / HEAD_DIM)
    ang = positions.astype(F32).reshape(t, 1) * inv_freq
    cos, sin = jnp.cos(ang), jnp.sin(ang)
    cos_t = jnp.concatenate([cos, cos, cos, cos], axis=1)
    sin_t = jnp.concatenate([-sin, sin, -sin, sin], axis=1)

    xc = x.reshape(t, d)
    row2 = lambda a: a.reshape(1, -1).astype(F32)
    for l in range(depth):
        w_v_t = w_in[l][:, 2 * ATTN_WIDTH:3 * ATTN_WIDTH].T.astype(BF16)
        q, k, v_t, u, g_a, g_s = _proj(xc, w_in[l].astype(BF16), w_v_t, cos_t, sin_t)
        attn = _attention(q, k, v_t, bsz, seq)
        tables = _s5_tables(lam_re[l], lam_im[l], log_dt[l], b_re[l], b_im[l], c_re[l], c_im[l])
        y5 = _s5_branch(u, tables, bsz, seq)
        x1 = _merge(alpha, attn, y5, u, g_a, g_s, xc, attn_up[l].astype(BF16), w_glu[l].astype(BF16),
                    s5_up[l].astype(BF16), w_o[l].astype(BF16), row2(d_skip[l]), row2(b_glu[l]),
                    row2(ln1_g[l]), row2(ln1_b[l]))
        wr_t = w_router[l].astype(F32).T
        wr_hi = wr_t.astype(BF16)
        wr_lo = (wr_t - wr_hi.astype(F32)).astype(BF16)
        bias_b = jnp.broadcast_to(router_bias[l].astype(F32)[:, None], (N_EXPERTS, 256))
        top_idx, top_w, rank, counts = _router(x1, wr_hi, wr_lo, bias_b)
        counts = counts[:, 0].astype(I32)
        pstart, pend, first_block, n_blk, n_used, n_rows = _block_plan(counts, t)
        dest = _dest_rows(top_idx, rank, jnp.broadcast_to(pstart.astype(F32)[:, None], (N_EXPERTS, 512)))
        xs = _dispatch(pend, counts, dest, x1, n_rows)
        y_sorted = _experts(l, first_block, n_blk, n_used, xs, w_gate_e, w_up_e, w_down_e)
        xc = _final(alpha, dest, x1, top_w.T, y_sorted, ws_gate[l].astype(BF16), ws_up[l].astype(BF16),
                    ws_down[l].astype(BF16), row2(ln2_g[l]), row2(ln2_b[l]))
    return xc.reshape(bsz, seq, d)
```

```python
import functools
import math

import jax
import jax.numpy as jnp
from jax import lax
from jax.experimental import pallas as pl
from jax.experimental.pallas import tpu as pltpu

F32 = jnp.float32
BF16 = jnp.bfloat16
I32 = jnp.int32

N_HEADS = 8
HEAD_DIM = 64
ATTN_WIDTH = N_HEADS * HEAD_DIM
MOBA_BLOCK = 256
MOBA_TOPK = 3
ROPE_THETA = 10000.0
S5_GROUP = 16
S5_GROUPS = 32
S5_WIDTH = S5_GROUP * S5_GROUPS
S5_STATE = 64
S5_CHUNK = 16
N_EXPERTS = 256
TOP_K = 8
N_EXPERT_GROUPS = 8
TOPK_GROUPS = 4
GROUP_SIZE = N_EXPERTS // N_EXPERT_GROUPS
ROUTED_SCALE = 2.5
LN_EPS = 1e-5
NEG_INF = -1e30
REMOVED = -3e38
QK_SCALE = HEAD_DIM ** -0.5 * math.log2(math.e)
V_ROWS = HEAD_DIM + 16

LANES = 128
MOE_ROWS = 256
EXPERT_IN_SLOTS = 4
VMEM_LIMIT = 56 * 1024 * 1024


def _cparams(sem):
    return pltpu.CompilerParams(dimension_semantics=sem, vmem_limit_bytes=VMEM_LIMIT)


def _sigmoid(x):
    return 1.0 / (1.0 + jnp.exp(-x))


def _layer_norm(r, g, b):
    mu = jnp.mean(r, axis=-1, keepdims=True)
    rc = r - mu
    var = jnp.mean(rc * rc, axis=-1, keepdims=True)
    return rc * lax.rsqrt(var + LN_EPS) * g + b


def _dot(a, b):
    return jnp.dot(a, b, preferred_element_type=F32)


def _dot_nt(a, b):
    return lax.dot_general(a, b, (((1,), (1,)), ((), ())), preferred_element_type=F32)


def _proj_kernel(x_ref, w_ref, wvt_ref, cos_ref, sin_ref, q_ref, k_ref, v_ref, u_ref, ga_ref, gs_ref):
    xb = x_ref[...].astype(BF16)
    cos = cos_ref[...]
    sin = sin_ref[...]
    lane = lax.broadcasted_iota(I32, cos.shape, 1)
    first_half = (lane % HEAD_DIM) < (HEAD_DIM // 2)

    def rope(t):
        rot = jnp.where(first_half, pltpu.roll(t, LANES - HEAD_DIM // 2, axis=1),
                        pltpu.roll(t, HEAD_DIM // 2, axis=1))
        return t * cos + rot * sin

    aw = ATTN_WIDTH
    for c in range(aw // LANES):
        sl = slice(c * LANES, (c + 1) * LANES)
        qc = _dot(xb, w_ref[:, c * LANES:(c + 1) * LANES])
        q_ref[:, sl] = (rope(qc) * QK_SCALE).astype(BF16)
        kc = _dot(xb, w_ref[:, aw + c * LANES:aw + (c + 1) * LANES])
        k_ref[:, sl] = rope(kc).astype(BF16)
    vt = _dot_nt(wvt_ref[...], xb).astype(BF16)
    ones = jnp.ones((V_ROWS - HEAD_DIM, MOBA_BLOCK), BF16)
    for p in range(v_ref.shape[0]):
        for c in range(v_ref.shape[1]):
            cols = slice(c * MOBA_BLOCK, (c + 1) * MOBA_BLOCK)
            heads = [vt[(2 * p + hh) * HEAD_DIM:(2 * p + hh + 1) * HEAD_DIM, cols] for hh in range(2)]
            v_ref[p, c] = jnp.concatenate([heads[0], ones, heads[1], ones], axis=0)
    u_ref[...] = _dot(xb, w_ref[:, 3 * aw:3 * aw + S5_WIDTH])
    d = ga_ref.shape[1]
    off = 3 * aw + S5_WIDTH
    ga_ref[...] = _dot(xb, w_ref[:, off:off + d])
    gs_ref[...] = _dot(xb, w_ref[:, off + d:off + 2 * d])


def _proj(x2, w_bf, wvt_bf, cos_t, sin_t, tm=512):
    t, d = x2.shape
    n = w_bf.shape[1]
    aw = ATTN_WIDTH
    row = lambda i: (i, 0)
    vshape = (aw // LANES, t // MOBA_BLOCK, 2 * V_ROWS, MOBA_BLOCK)
    return pl.pallas_call(
        _proj_kernel,
        grid=(t // tm,),
        in_specs=[pl.BlockSpec((tm, d), row),
                  pl.BlockSpec((d, n), lambda i: (0, 0)),
                  pl.BlockSpec((aw, d), lambda i: (0, 0)),
                  pl.BlockSpec((tm, LANES), row),
                  pl.BlockSpec((tm, LANES), row)],
        out_specs=[pl.BlockSpec((tm, aw), row), pl.BlockSpec((tm, aw), row),
                   pl.BlockSpec((vshape[0], tm // MOBA_BLOCK) + vshape[2:], lambda i: (0, i, 0, 0)),
                   pl.BlockSpec((tm, S5_WIDTH), row), pl.BlockSpec((tm, d), row), pl.BlockSpec((tm, d), row)],
        out_shape=[jax.ShapeDtypeStruct((t, aw), BF16), jax.ShapeDtypeStruct((t, aw), BF16),
                   jax.ShapeDtypeStruct(vshape, BF16), jax.ShapeDtypeStruct((t, S5_WIDTH), F32),
                   jax.ShapeDtypeStruct((t, d), F32), jax.ShapeDtypeStruct((t, d), F32)],
        compiler_params=_cparams(("parallel",)),
        name="proj_rope",
    )(x2, w_bf, wvt_bf, cos_t, sin_t)


def _attn_kernel(q_ref, k_ref, v_ref, o_ref, kmean_ref):
    i = pl.program_id(1)
    blk = MOBA_BLOCK
    n_blocks = k_ref.shape[0] // blk
    n_pairs = q_ref.shape[1] // LANES

    @pl.when(i == 0)
    def _():
        kmean_ref[...] = jnp.zeros(kmean_ref.shape, kmean_ref.dtype)
        for j in range(n_blocks):
            kj = k_ref[j * blk:(j + 1) * blk, :].astype(F32)
            kmean_ref[j:j + 1, :] = jnp.sum(kj, axis=0, keepdims=True) * (1.0 / blk)

    q_t = q_ref[...].astype(F32).T.astype(BF16)
    sub = lax.broadcasted_iota(I32, (LANES, blk), 0)
    key = lax.broadcasted_iota(I32, (blk, blk), 0)
    qry = lax.broadcasted_iota(I32, (blk, blk), 1)
    causal = key <= qry
    own = pl.ds(pl.multiple_of(i * blk, blk), blk)

    heads = [(pr, hh) for pr in range(n_pairs) for hh in range(2)]
    blk_row = lax.broadcasted_iota(I32, (HEAD_DIM, blk), 0)
    q_heads, gates, own_scores = [], [], []
    for pr, hh in heads:
        cols = slice(pr * LANES, (pr + 1) * LANES)
        q_pair = q_t[pr * LANES:(pr + 1) * LANES]
        in_head = (sub >= hh * HEAD_DIM) & (sub < (hh + 1) * HEAD_DIM)
        qh = jnp.where(in_head, q_pair, jnp.zeros_like(q_pair))
        q_heads.append(qh)
        gates.append(_dot(kmean_ref[:HEAD_DIM, cols].astype(BF16), qh))
        own_scores.append(_dot(k_ref[own, cols], qh))
    q_aug, own_probs = [], []
    for n, (pr, hh) in enumerate(heads):
        g = jnp.where(blk_row < i, gates[n], NEG_INF)
        selected = jnp.zeros(g.shape, jnp.bool_)
        for _ in range(MOBA_TOPK):
            mx = jnp.max(g, axis=0, keepdims=True)
            first = jnp.min(jnp.where(g == mx, blk_row, HEAD_DIM), axis=0, keepdims=True)
            pick = blk_row == first
            selected = selected | (pick & (blk_row < i))
            g = jnp.where(pick, REMOVED, g)
        sel_neg = jnp.where(selected, 0.0, NEG_INF).astype(BF16)
        q_own = q_heads[n][hh * HEAD_DIM:(hh + 1) * HEAD_DIM]
        q_aug.append(jnp.concatenate([q_own, sel_neg] if hh == 0 else [sel_neg, q_own], axis=0))
        s = jnp.where(causal, own_scores[n], NEG_INF)
        mx = jnp.max(s, axis=0, keepdims=True)
        own_probs.append((mx, jnp.exp2(s - mx).astype(BF16)))
    init = []
    for n, (pr, hh) in enumerate(heads):
        mx, p = own_probs[n]
        init.append((mx, _dot(v_ref[pr, i, hh * V_ROWS:(hh + 1) * V_ROWS, :], p)))

    lane2 = lax.broadcasted_iota(I32, (2 * blk, LANES), 1)
    second = (lax.broadcasted_iota(I32, (2 * blk, LANES), 0) >= blk).astype(I32)

    def body(jj, carry):
        rows = pl.ds(pl.multiple_of(jj * (2 * blk), 2 * blk), 2 * blk)
        blk_id = 2 * jj + second
        scores = []
        for n, (pr, hh) in enumerate(heads):
            k_pair = k_ref[rows, pr * LANES:(pr + 1) * LANES]
            other = (1 - hh) * HEAD_DIM
            in_other = (lane2 >= other) & (lane2 < other + HEAD_DIM)
            k_aug = jnp.where(in_other, (lane2 == other + blk_id).astype(BF16), k_pair)
            scores.append(_dot(k_aug, q_aug[n]))
        probs = []
        for n in range(len(heads)):
            m = carry[n][0]
            m_new = jnp.maximum(m, jnp.max(scores[n], axis=0, keepdims=True))
            probs.append((m_new, jnp.exp2(m - m_new), jnp.exp2(scores[n] - m_new).astype(BF16)))
        out = []
        for n, (pr, hh) in enumerate(heads):
            m_new, alpha, p = probs[n]
            vrows = slice(hh * V_ROWS, (hh + 1) * V_ROWS)
            pv = _dot(v_ref[pr, 2 * jj, vrows, :], p[:blk]) + _dot(v_ref[pr, 2 * jj + 1, vrows, :], p[blk:])
            out.append((m_new, alpha * carry[n][1] + pv))
        return tuple(out)

    res = lax.fori_loop(0, (i + 1) // 2, body, tuple(init))
    o_t = jnp.concatenate([acc[:HEAD_DIM] / acc[HEAD_DIM:HEAD_DIM + 1] for _, acc in res], axis=0)
    o_ref[...] = o_t.T.astype(BF16)


def _attention(q, k, v_t, bsz, seq):
    t, aw = q.shape
    blk = MOBA_BLOCK
    nb = seq // blk
    assert nb <= HEAD_DIM
    assert nb % 2 == 0
    qmap = lambda b, i: (b * nb + i, 0)
    return pl.pallas_call(
        _attn_kernel,
        grid=(bsz, nb),
        in_specs=[pl.BlockSpec((blk, aw), qmap),
                  pl.BlockSpec((seq, aw), lambda b, i: (b, 0)),
                  pl.BlockSpec((aw // LANES, nb, 2 * V_ROWS, blk), lambda b, i: (0, b, 0, 0))],
        out_specs=pl.BlockSpec((blk, aw), qmap),
        out_shape=jax.ShapeDtypeStruct((t, aw), BF16),
        scratch_shapes=[pltpu.VMEM((LANES, aw), F32)],
        compiler_params=_cparams(("parallel", "arbitrary")),
        name="moba_attention",
    )(q, k, v_t)


def _s5_tables(lam_re, lam_im, log_dt, b_re, b_im, c_re, c_im):
    hp = lax.Precision.HIGHEST
    L = S5_CHUNK
    lr, li = lam_re.astype(F32), lam_im.astype(F32)
    dt = jnp.exp(log_dt.astype(F32))[:, None]
    mag = jnp.exp(lr * dt)
    a_re = mag * jnp.cos(li * dt)
    a_im = mag * jnp.sin(li * dt)
    den = lr * lr + li * li
    nr = a_re - 1.0
    coef_re = (nr * lr + a_im * li) / den
    coef_im = (a_im * lr - nr * li) / den
    br, bi = b_re.astype(F32), b_im.astype(F32)
    bbar_re = coef_re[..., None] * br - coef_im[..., None] * bi
    bbar_im = coef_re[..., None] * bi + coef_im[..., None] * br
    tau = jnp.arange(L + 1, dtype=F32)[:, None, None]
    pmag = jnp.exp(tau * (lr * dt))
    p_re = pmag * jnp.cos(tau * (li * dt))
    p_im = pmag * jnp.sin(tau * (li * dt))
    cr, ci = c_re.astype(F32), c_im.astype(F32)
    ca_re = cr[None] * p_re[:, :, None, :] - ci[None] * p_im[:, :, None, :]
    ca_im = cr[None] * p_im[:, :, None, :] + ci[None] * p_re[:, :, None, :]
    kmat = (jnp.einsum('tgon,gni->tgio', ca_re[:L], bbar_re, precision=hp)
            - jnp.einsum('tgon,gni->tgio', ca_im[:L], bbar_im, precision=hp))
    s_ix = jnp.arange(L)[:, None]
    t_ix = jnp.arange(L)[None, :]
    lag = jnp.clip(t_ix - s_ix, 0, L - 1)
    m = kmat[lag]
    m = jnp.where((t_ix >= s_ix)[:, :, None, None, None], m, 0.0)
    m_intra = m.transpose(2, 0, 3, 1, 4).reshape(S5_GROUPS, L * S5_GROUP, L * S5_GROUP)
    rev = p_re[L - 1 - jnp.arange(L)], p_im[L - 1 - jnp.arange(L)]
    w_re = rev[0][..., None] * bbar_re[None] - rev[1][..., None] * bbar_im[None]
    w_im = rev[0][..., None] * bbar_im[None] + rev[1][..., None] * bbar_re[None]
    w_re = w_re.transpose(1, 0, 3, 2).reshape(S5_GROUPS, L * S5_GROUP, S5_STATE)
    w_im = w_im.transpose(1, 0, 3, 2).reshape(S5_GROUPS, L * S5_GROUP, S5_STATE)
    w_state = jnp.concatenate([w_re, w_im, w_im, w_re], axis=-1)
    cx_re = ca_re[1:].transpose(1, 3, 0, 2).reshape(S5_GROUPS, S5_STATE, L * S5_GROUP)
    cx_im = ca_im[1:].transpose(1, 3, 0, 2).reshape(S5_GROUPS, S5_STATE, L * S5_GROUP)
    cx = jnp.concatenate([cx_re, -cx_im], axis=1)
    al_re, al_im = p_re[L], p_im[L]
    a1 = jnp.concatenate([al_re, al_re], axis=-1)
    a2 = jnp.concatenate([-al_im, al_im], axis=-1)
    return m_intra.astype(BF16), w_state.astype(BF16), cx.astype(BF16), a1, a2


def _s5_group_kernel(n_seq, u_ref, w_ref, m_ref, cx_ref, a1_ref, a2_ref, y_ref, e_ref, h_ref):
    u = u_ref[0]
    e_ref[...] = _dot(u, w_ref[0])
    a1 = a1_ref[0]
    a2 = a2_ref[0]
    nk = e_ref.shape[0] // n_seq
    half = h_ref.shape[1]

    def step(k, carry):
        out = []
        for b in range(n_seq):
            hp, hq = carry[b]
            row = pl.ds(b * nk + k, 1)
            h_ref[row, :] = hp
            e = e_ref[row, :]
            out.append((a1 * hp + a2 * hq + e[:, :half], a1 * hq - a2 * hp + e[:, half:]))
        return tuple(out)

    zero = jnp.zeros((1, half), F32)
    lax.fori_loop(0, nk, step, tuple((zero, zero) for _ in range(n_seq)), unroll=4)
    y_ref[0] = _dot(u, m_ref[0]) + _dot(h_ref[...].astype(BF16), cx_ref[0])


def _s5_groups(ug, tables, n_seq):
    m_intra, w_state, cx, a1, a2 = tables
    g, r, c = ug.shape
    n2 = cx.shape[1]
    spec = lambda shape: pl.BlockSpec(shape, lambda i: (i, 0, 0))
    return pl.pallas_call(
        functools.partial(_s5_group_kernel, n_seq),
        grid=(g,),
        in_specs=[spec((1, r, c)), spec((1, c, c)), spec((1, c, c)), spec((1, n2, c)),
                  spec((1, 1, n2)), spec((1, 1, n2))],
        out_specs=spec((1, r, c)),
        out_shape=jax.ShapeDtypeStruct((g, r, c), F32),
        scratch_shapes=[pltpu.VMEM((r, c), F32), pltpu.VMEM((r, n2), F32)],
        compiler_params=_cparams(("parallel",)),
        name="s5_groups",
    )(ug, w_state, m_intra, cx, a1[:, None, :], a2[:, None, :])


def _s5_branch(u, tables, bsz, seq):
    L, G, C = S5_CHUNK, S5_GROUPS, S5_GROUP
    nk = seq // L
    ug = u.reshape(bsz, nk, L, G, C).transpose(3, 0, 1, 2, 4).reshape(G, bsz * nk, L * C).astype(BF16)
    y = _s5_groups(ug, tables, bsz)
    return y.reshape(G, bsz, nk, L, C).transpose(1, 2, 3, 0, 4).reshape(bsz * seq, G * C)


def _merge_kernel(alpha, attn_ref, y_ref, u_ref, ga_ref, gs_ref, x_ref, aup_ref, wglu_ref, sup_ref, wo_ref,
                  dskip_ref, bglu_ref, g_ref, b_ref, o_ref):
    a = _dot(attn_ref[...], aup_ref[...])
    y = y_ref[...] + dskip_ref[...] * u_ref[...]
    z = 0.5 * y * (1.0 + jnp.tanh(math.sqrt(2.0 / math.pi) * (y + 0.044715 * (y * y * y))))
    gate = _dot(z.astype(BF16), wglu_ref[...]) + bglu_ref[...]
    s = _dot((z * _sigmoid(gate)).astype(BF16), sup_ref[...])
    mixed = _sigmoid(ga_ref[...]) * a + _sigmoid(gs_ref[...]) * s
    r = alpha * x_ref[...] + _dot(mixed.astype(BF16), wo_ref[...])
    o_ref[...] = _layer_norm(r, g_ref[...], b_ref[...])


def _merge(alpha, attn, y5, u, g_a, g_s, x2, aup, wglu, sup, wo, dskip, bglu, ln_g, ln_b, tm=256):
    t, d = x2.shape
    row = lambda i: (i, 0)
    full = lambda a: pl.BlockSpec(a.shape, lambda i: (0, 0))
    return pl.pallas_call(
        functools.partial(_merge_kernel, alpha),
        grid=(t // tm,),
        in_specs=[pl.BlockSpec((tm, ATTN_WIDTH), row), pl.BlockSpec((tm, S5_WIDTH), row),
                  pl.BlockSpec((tm, S5_WIDTH), row), pl.BlockSpec((tm, d), row), pl.BlockSpec((tm, d), row),
                  pl.BlockSpec((tm, d), row), full(aup), full(wglu), full(sup), full(wo),
                  full(dskip), full(bglu), full(ln_g), full(ln_b)],
        out_specs=pl.BlockSpec((tm, d), row),
        out_shape=jax.ShapeDtypeStruct((t, d), F32),
        compiler_params=_cparams(("parallel",)),
        name="merge_ln1",
    )(attn, y5, u, g_a, g_s, x2, aup, wglu, sup, wo, dskip, bglu, ln_g, ln_b)


def _router_kernel(x_ref, wh_ref, wl_ref, bias_ref, idx_ref, wt_ref, rank_ref, cnt_ref, carry_ref):
    tm = x_ref.shape[0]

    @pl.when(pl.program_id(0) == 0)
    def _():
        carry_ref[...] = jnp.zeros(carry_ref.shape, F32)

    x = x_ref[...]
    xh = x.astype(BF16)
    xl = (x - xh.astype(F32)).astype(BF16)
    wh = wh_ref[...]
    logits = _dot_nt(wh, xh) + _dot_nt(wh, xl) + _dot_nt(wl_ref[...], xh)
    scores = _sigmoid(logits)
    biased = scores + bias_ref[...]
    gs_rows = GROUP_SIZE
    row_g = lax.broadcasted_iota(I32, (gs_rows, tm), 0)
    group_score = []
    for g in range(N_EXPERT_GROUPS):
        vg = biased[g * gs_rows:(g + 1) * gs_rows, :]
        m1 = jnp.max(vg, axis=0, keepdims=True)
        i1 = jnp.min(jnp.where(vg == m1, row_g, gs_rows), axis=0, keepdims=True)
        m2 = jnp.max(jnp.where(row_g == i1, REMOVED, vg), axis=0, keepdims=True)
        group_score.append(m1 + m2)
    pieces = []
    for g in range(N_EXPERT_GROUPS):
        beats = jnp.zeros((1, tm), I32)
        for g2 in range(N_EXPERT_GROUPS):
            if g2 == g:
                continue
            better = group_score[g2] > group_score[g]
            if g2 < g:
                better = better | (group_score[g2] == group_score[g])
            beats = beats + better.astype(I32)
        keep = beats < TOPK_GROUPS
        pieces.append(jnp.where(keep, biased[g * gs_rows:(g + 1) * gs_rows, :], NEG_INF))
    v = jnp.concatenate(pieces, axis=0)
    row = lax.broadcasted_iota(I32, (N_EXPERTS, tm), 0)
    idxs, wts = [], []
    multi_hot = jnp.zeros((N_EXPERTS, tm), F32)
    for _ in range(TOP_K):
        m = jnp.max(v, axis=0, keepdims=True)
        ix = jnp.min(jnp.where(v == m, row, N_EXPERTS), axis=0, keepdims=True)
        hit = row == ix
        idxs.append(ix)
        wts.append(jnp.sum(jnp.where(hit, scores, 0.0), axis=0, keepdims=True))
        multi_hot = multi_hot + hit.astype(F32)
        v = jnp.where(hit, REMOVED, v)
    wsum = wts[0]
    for k in range(1, TOP_K):
        wsum = wsum + wts[k]
    s_ix = lax.broadcasted_iota(I32, (tm, tm), 0)
    t_ix = lax.broadcasted_iota(I32, (tm, tm), 1)
    mh = multi_hot.astype(BF16)
    before = _dot(mh, (s_ix < t_ix).astype(BF16)) + carry_ref[...]
    carry_ref[...] = carry_ref[...] + _dot(mh, jnp.ones((tm, tm), BF16))
    for k in range(TOP_K):
        idx_ref[k:k + 1, :] = idxs[k]
        wt_ref[k:k + 1, :] = wts[k] / wsum * ROUTED_SCALE
        rank_ref[k:k + 1, :] = jnp.sum(jnp.where(row == idxs[k], before, 0.0), axis=0, keepdims=True).astype(I32)
    cnt_ref[...] = carry_ref[:, :LANES]


def _router(x1, wr_hi, wr_lo, bias_b, tm=256):
    t, d = x1.shape
    e = wr_hi.shape[0]
    tok = lambda i: (0, i)
    const = lambda i: (0, 0)
    return pl.pallas_call(
        _router_kernel,
        grid=(t // tm,),
        in_specs=[pl.BlockSpec((tm, d), lambda i: (i, 0)), pl.BlockSpec((e, d), const),
                  pl.BlockSpec((e, d), const), pl.BlockSpec((e, tm), const)],
        out_specs=[pl.BlockSpec((TOP_K, tm), tok), pl.BlockSpec((TOP_K, tm), tok),
                   pl.BlockSpec((TOP_K, tm), tok), pl.BlockSpec((e, LANES), const)],
        out_shape=[jax.ShapeDtypeStruct((TOP_K, t), I32), jax.ShapeDtypeStruct((TOP_K, t), F32),
                   jax.ShapeDtypeStruct((TOP_K, t), I32), jax.ShapeDtypeStruct((e, LANES), F32)],
        scratch_shapes=[pltpu.VMEM((e, tm), F32)],
        compiler_params=_cparams(("arbitrary",)),
        name="router_topk_rank",
    )(x1, wr_hi, wr_lo, bias_b)


def _dest_kernel(idx_ref, rank_ref, pstart_ref, dest_ref):
    pstart = pstart_ref[...]
    row = lax.broadcasted_iota(I32, pstart.shape, 0)
    for k in range(TOP_K):
        start = jnp.sum(jnp.where(row == idx_ref[k:k + 1, :], pstart, 0.0), axis=0, keepdims=True)
        dest_ref[k:k + 1, :] = start.astype(I32) + rank_ref[k:k + 1, :]


def _dest_rows(top_idx, rank, pstart_b):
    tm = pstart_b.shape[1]
    tok = lambda i: (0, i)
    return pl.pallas_call(
        _dest_kernel,
        grid=(top_idx.shape[1] // tm,),
        in_specs=[pl.BlockSpec((TOP_K, tm), tok), pl.BlockSpec((TOP_K, tm), tok),
                  pl.BlockSpec(pstart_b.shape, lambda i: (0, 0))],
        out_specs=pl.BlockSpec((TOP_K, tm), tok),
        out_shape=jax.ShapeDtypeStruct(top_idx.shape, I32),
        compiler_params=_cparams(("parallel",)),
        name="moe_dest_rows",
    )(top_idx, rank, pstart_b)


def _rows_wait(src_hbm, dst, sem, n_rows):
    pltpu.make_async_copy(src_hbm.at[pl.ds(0, n_rows), :], dst, sem).wait()


def _dispatch_kernel(pend_ref, cnt_ref, dest_ref, x_ref, xs_hbm, zbuf, zsem, sem):
    i = pl.program_id(0)
    tm = dest_ref.shape[1]
    mb = zbuf.shape[0]

    @pl.when(i == 0)
    def _():
        zbuf[...] = jnp.zeros(zbuf.shape, zbuf.dtype)

        def zero_copy(e):
            start = pl.multiple_of(pend_ref[e] - mb, mb)
            return pltpu.make_async_copy(zbuf, xs_hbm.at[pl.ds(start, mb), :], zsem)

        def z_start(e, c):
            @pl.when(cnt_ref[e] > 0)
            def _():
                zero_copy(e).start()
            return c

        def z_wait(e, c):
            @pl.when(cnt_ref[e] > 0)
            def _():
                zero_copy(e).wait()
            return c

        def tail_copy(b):
            return pltpu.make_async_copy(zbuf, xs_hbm.at[pl.ds(pl.multiple_of(b * mb, mb), mb), :], zsem)

        def t_start(b, c):
            tail_copy(b).start()
            return c

        def t_wait(b, c):
            tail_copy(b).wait()
            return c

        n_used = pend_ref[N_EXPERTS - 1] // mb
        n_blocks = xs_hbm.shape[0] // mb
        lax.fori_loop(0, N_EXPERTS, z_start, 0)
        lax.fori_loop(n_used, n_blocks, t_start, 0)
        lax.fori_loop(0, N_EXPERTS, z_wait, 0)
        lax.fori_loop(n_used, n_blocks, t_wait, 0)

    def issue(t, c):
        src = x_ref.at[pl.ds(t, 1), :]
        for k in range(TOP_K):
            pltpu.make_async_copy(src, xs_hbm.at[pl.ds(dest_ref[k, t], 1), :], sem).start()
        return c

    lax.fori_loop(0, tm, issue, 0, unroll=2)
    for _ in range(TOP_K):
        pltpu.make_async_copy(x_ref, xs_hbm.at[pl.ds(0, tm), :], sem).wait()


def _dispatch(padded_ends, counts, dest, x1, n_rows, tm=1024):
    t, d = x1.shape
    grid_spec = pltpu.PrefetchScalarGridSpec(
        num_scalar_prefetch=2,
        grid=(t // tm,),
        in_specs=[pl.BlockSpec((TOP_K, tm), lambda i, pe, cn: (0, i), memory_space=pltpu.SMEM),
                  pl.BlockSpec((tm, d), lambda i, pe, cn: (i, 0))],
        out_specs=pl.BlockSpec(memory_space=pl.ANY),
        scratch_shapes=[pltpu.VMEM((MOE_ROWS, d), x1.dtype), pltpu.SemaphoreType.DMA(()),
                        pltpu.SemaphoreType.DMA(())],
    )
    return pl.pallas_call(
        _dispatch_kernel,
        grid_spec=grid_spec,
        out_shape=jax.ShapeDtypeStruct((n_rows, d), x1.dtype),
        compiler_params=_cparams(("arbitrary",)),
        name="moe_dispatch",
    )(padded_ends, counts, dest, x1)


def _expert_kernel(first_ref, nblk_ref, nused_ref, xs_hbm, wg_ref, wu_ref, wd_ref, y_hbm,
                   xbuf, ybuf, wgb, wub, wdb, xsem, ysem):
    e = pl.program_id(0)
    n_used = nused_ref[0]
    nx, mb = xbuf.shape[0], xbuf.shape[1]

    def x_copy(g, slot):
        return pltpu.make_async_copy(xs_hbm.at[pl.ds(pl.multiple_of(g * mb, mb), mb), :], xbuf.at[slot],
                                     xsem.at[slot])

    nt = y_hbm.shape[1]
    sub = ybuf.shape[1] // mb

    def y_copy(g, slot):
        return pltpu.make_async_copy(
            ybuf.at[slot], y_hbm.at[pl.ds(pl.multiple_of(g * (mb * sub), mb * sub), mb * sub), :], ysem.at[slot])

    @pl.when(e == 0)
    def _():
        for s in range(nx - 1):
            @pl.when(s < n_used)
            def _():
                x_copy(s, s).start()

    wgb[...] = wg_ref[0, 0].astype(BF16)
    wub[...] = wu_ref[0, 0].astype(BF16)
    wdb[...] = wd_ref[0, 0].astype(BF16)

    def block(b, c):
        g = first_ref[e] + b
        slot = g % nx
        yslot = g % 2
        ahead = g + (nx - 1)

        @pl.when(ahead < n_used)
        def _():
            x_copy(ahead, ahead % nx).start()

        x_copy(g, slot).wait()

        @pl.when(g >= 2)
        def _():
            y_copy(g - 2, yslot).wait()

        xb = xbuf[slot].astype(BF16)
        hg = _dot(xb, wgb[...])
        hu = _dot(xb, wub[...])
        h = hg * _sigmoid(hg) * hu
        y = _dot(h.astype(BF16), wdb[...])
        for s in range(sub):
            ybuf[yslot, pl.ds(s, mb, stride=sub), :] = y[:, s * nt:(s + 1) * nt]
        y_copy(g, yslot).start()
        return c

    lax.fori_loop(0, nblk_ref[e], block, 0)

    @pl.when(e == pl.num_programs(0) - 1)
    def _():
        @pl.when(n_used >= 2)
        def _():
            y_copy(n_used - 2, n_used % 2).wait()

        y_copy(n_used - 1, (n_used - 1) % 2).wait()
        ybuf[0] = jnp.zeros(ybuf.shape[1:], ybuf.dtype)
        n_blocks = y_hbm.shape[0] // (mb * sub)

        def z_start(g, c):
            y_copy(g, 0).start()
            return c

        def z_wait(g, c):
            y_copy(g, 0).wait()
            return c

        lax.fori_loop(n_used, n_blocks, z_start, 0)
        lax.fori_loop(n_used, n_blocks, z_wait, 0)


def _experts(layer, first_block, n_blk, n_used, xs, w_gate_e, w_up_e, w_down_e):
    mb = MOE_ROWS
    n_rows, d = xs.shape
    _, n_exp, _, hdim = w_gate_e.shape
    wmap = lambda e, fb, nb, nu: (layer, e, 0, 0)
    grid_spec = pltpu.PrefetchScalarGridSpec(
        num_scalar_prefetch=3,
        grid=(n_exp,),
        in_specs=[pl.BlockSpec(memory_space=pl.ANY),
                  pl.BlockSpec((1, 1, d, hdim), wmap),
                  pl.BlockSpec((1, 1, d, hdim), wmap),
                  pl.BlockSpec((1, 1, hdim, d), wmap)],
        out_specs=pl.BlockSpec(memory_space=pl.ANY),
        scratch_shapes=[pltpu.VMEM((EXPERT_IN_SLOTS, mb, d), xs.dtype), pltpu.VMEM((2, mb * (d // LANES), LANES), F32),
                        pltpu.VMEM((d, hdim), BF16), pltpu.VMEM((d, hdim), BF16), pltpu.VMEM((hdim, d), BF16),
                        pltpu.SemaphoreType.DMA((EXPERT_IN_SLOTS,)), pltpu.SemaphoreType.DMA((2,))],
    )
    return pl.pallas_call(
        _expert_kernel,
        grid_spec=grid_spec,
        out_shape=jax.ShapeDtypeStruct((n_rows * (d // LANES), LANES), F32),
        compiler_params=_cparams(("arbitrary",)),
        name="routed_experts",
    )(first_block, n_blk, n_used, xs, w_gate_e, w_up_e, w_down_e)


def _final_kernel(alpha, dcur_ref, dnext_ref, x_ref, wt_ref, y_hbm, wsg_ref, wsu_ref, wsd_ref, g_ref, b_ref,
                  o_ref, ybuf, sems):
    i = pl.program_id(0)
    n_steps = pl.num_programs(0)
    tm = x_ref.shape[0]
    slot = i % 2

    sub = ybuf.shape[2] // tm
    nt = ybuf.shape[3]

    def gather(dref, s):
        def issue(t, c):
            dst_rows = pl.ds(pl.multiple_of(t * sub, sub), sub)
            for k in range(TOP_K):
                src_rows = pl.ds(pl.multiple_of(dref[k, t] * sub, sub), sub)
                pltpu.make_async_copy(y_hbm.at[src_rows, :], ybuf.at[s, k, dst_rows, :], sems.at[s]).start()
            return c

        lax.fori_loop(0, tm, issue, 0, unroll=2)

    @pl.when(i == 0)
    def _():
        gather(dcur_ref, 0)

    @pl.when(i + 1 < n_steps)
    def _():
        gather(dnext_ref, 1 - slot)

    x = x_ref[...]
    xb = x.astype(BF16)
    hg = _dot(xb, wsg_ref[...])
    hu = _dot(xb, wsu_ref[...])
    f = _dot((hg * _sigmoid(hg) * hu).astype(BF16), wsd_ref[...])
    for k in range(TOP_K):
        _rows_wait(y_hbm, ybuf.at[slot, k], sems.at[slot], tm * sub)
    wt = wt_ref[...]
    cols = []
    for s in range(sub):
        fs = f[:, s * nt:(s + 1) * nt]
        for k in range(TOP_K):
            fs = fs + wt[:, k:k + 1] * ybuf[slot, k, pl.ds(s, tm, stride=sub), :]
        cols.append(fs)
    f = jnp.concatenate(cols, axis=1)
    o_ref[...] = _layer_norm(alpha * x + f, g_ref[...], b_ref[...])


def _final(alpha, dest, x1, wt_tok, y_sorted, wsg, wsu, wsd, ln_g, ln_b, tm=256):
    t, d = x1.shape
    n = t // tm
    row = lambda i: (i, 0)
    full = lambda a: pl.BlockSpec(a.shape, lambda i: (0, 0))
    return pl.pallas_call(
        functools.partial(_final_kernel, alpha),
        grid=(n,),
        in_specs=[pl.BlockSpec((TOP_K, tm), lambda i: (0, i), memory_space=pltpu.SMEM),
                  pl.BlockSpec((TOP_K, tm), lambda i: (0, jnp.minimum(i + 1, n - 1)), memory_space=pltpu.SMEM),
                  pl.BlockSpec((tm, d), row), pl.BlockSpec((tm, TOP_K), row),
                  pl.BlockSpec(memory_space=pl.ANY), full(wsg), full(wsu), full(wsd), full(ln_g), full(ln_b)],
        out_specs=pl.BlockSpec((tm, d), row),
        out_shape=jax.ShapeDtypeStruct((t, d), F32),
        scratch_shapes=[pltpu.VMEM((2, TOP_K, tm * (d // LANES), LANES), F32), pltpu.SemaphoreType.DMA((2,))],
        compiler_params=_cparams(("arbitrary",)),
        name="combine_shared_ln2",
    )(dest, dest, x1, wt_tok, y_sorted, wsg, wsu, wsd, ln_g, ln_b)


def _block_plan(counts, n_tok):
    mb = MOE_ROWS
    n_blocks = -(-(n_tok * TOP_K + N_EXPERTS * (mb - 1)) // mb)
    padded = (counts + mb - 1) // mb * mb
    padded_ends = jnp.cumsum(padded).astype(I32)
    padded_starts = padded_ends - padded
    n_used = (padded_ends[-1] // mb).astype(I32).reshape(1)
    return padded_starts, padded_ends, padded_starts // mb, padded // mb, n_used, n_blocks * mb


def kernel(x, positions, w_in, attn_up, lam_re, lam_im, log_dt, b_re, b_im, c_re, c_im, d_skip, w_glu, b_glu,
           s5_up, w_o, ln1_g, ln1_b, w_router, router_bias, w_gate_e, w_up_e, w_down_e, ws_gate, ws_up, ws_down,
           ln2_g, ln2_b):
    bsz, seq, d = x.shape
    depth = w_in.shape[0]
    t = bsz * seq
    alpha = (2 * depth) ** 0.25
    assert seq % MOBA_BLOCK == 0 and seq % S5_CHUNK == 0 and t % 512 == 0

    inv_freq = ROPE_THETA ** (-jnp.arange(0, HEAD_DIM, 2, dtype=F32) / HEAD_DIM)
    ang = positions.astype(F32).reshape(t, 1) * inv_freq
    cos, sin = jnp.cos(ang), jnp.sin(ang)
    cos_t = jnp.concatenate([cos, cos, cos, cos], axis=1)
    sin_t = jnp.concatenate([-sin, sin, -sin, sin], axis=1)

    xc = x.reshape(t, d)
    row2 = lambda a: a.reshape(1, -1).astype(F32)
    for l in range(depth):
        w_v_t = w_in[l][:, 2 * ATTN_WIDTH:3 * ATTN_WIDTH].T.astype(BF16)
        q, k, v_t, u, g_a, g_s = _proj(xc, w_in[l].astype(BF16), w_v_t, cos_t, sin_t)
        attn = _attention(q, k, v_t, bsz, seq)
        tables = _s5_tables(lam_re[l], lam_im[l], log_dt[l], b_re[l], b_im[l], c_re[l], c_im[l])
        y5 = _s5_branch(u, tables, bsz, seq)
        x1 = _merge(alpha, attn, y5, u, g_a, g_s, xc, attn_up[l].astype(BF16), w_glu[l].astype(BF16),
                    s5_up[l].astype(BF16), w_o[l].astype(BF16), row2(d_skip[l]), row2(b_glu[l]),
                    row2(ln1_g[l]), row2(ln1_b[l]))
        wr_t = w_router[l].astype(F32).T
        wr_hi = wr_t.astype(BF16)
        wr_lo = (wr_t - wr_hi.astype(F32)).astype(BF16)
        bias_b = jnp.broadcast_to(router_bias[l].astype(F32)[:, None], (N_EXPERTS, 256))
        top_idx, top_w, rank, counts = _router(x1, wr_hi, wr_lo, bias_b)
        counts = counts[:, 0].astype(I32)
        pstart, pend, first_block, n_blk, n_used, n_rows = _block_plan(counts, t)
        dest = _dest_rows(top_idx, rank, jnp.broadcast_to(pstart.astype(F32)[:, None], (N_EXPERTS, 512)))
        xs = _dispatch(pend, counts, dest, x1, n_rows)
        y_sorted = _experts(l, first_block, n_blk, n_used, xs, w_gate_e, w_up_e, w_down_e)
        xc = _final(alpha, dest, x1, top_w.T, y_sorted, ws_gate[l].astype(BF16), ws_up[l].astype(BF16),
                    ws_down[l].astype(BF16), row2(ln2_g[l]), row2(ln2_b[l]))
    return xc.reshape(bsz, seq, d)
```

```python
import functools
import math

import jax
import jax.numpy as jnp
from jax import lax
from jax.experimental import pallas as pl
from jax.experimental.pallas import tpu as pltpu

F32 = jnp.float32
BF16 = jnp.bfloat16
I32 = jnp.int32

N_HEADS = 8
HEAD_DIM = 64
ATTN_WIDTH = N_HEADS * HEAD_DIM
MOBA_BLOCK = 256
MOBA_TOPK = 3
ROPE_THETA = 10000.0
S5_GROUP = 16
S5_GROUPS = 32
S5_WIDTH = S5_GROUP * S5_GROUPS
S5_STATE = 64
S5_CHUNK = 16
N_EXPERTS = 256
TOP_K = 8
N_EXPERT_GROUPS = 8
TOPK_GROUPS = 4
GROUP_SIZE = N_EXPERTS // N_EXPERT_GROUPS
ROUTED_SCALE = 2.5
LN_EPS = 1e-5
NEG_INF = -1e30
REMOVED = -3e38
QK_SCALE = HEAD_DIM ** -0.5 * math.log2(math.e)
V_ROWS = HEAD_DIM + 16

LANES = 128
MOE_ROWS = 256
EXPERT_IN_SLOTS = 4
VMEM_LIMIT = 56 * 1024 * 1024


def _cparams(sem):
    return pltpu.CompilerParams(dimension_semantics=sem, vmem_limit_bytes=VMEM_LIMIT)


def _sigmoid(x):
    return 1.0 / (1.0 + jnp.exp(-x))


def _layer_norm(r, g, b):
    mu = jnp.mean(r, axis=-1, keepdims=True)
    rc = r - mu
    var = jnp.mean(rc * rc, axis=-1, keepdims=True)
    return rc * lax.rsqrt(var + LN_EPS) * g + b


def _dot(a, b):
    return jnp.dot(a, b, preferred_element_type=F32)


def _dot_nt(a, b):
    return lax.dot_general(a, b, (((1,), (1,)), ((), ())), preferred_element_type=F32)


def _proj_kernel(x_ref, w_ref, wvt_ref, cos_ref, sin_ref, q_ref, k_ref, v_ref, u_ref, ga_ref, gs_ref):
    xb = x_ref[...].astype(BF16)
    cos = cos_ref[...]
    sin = sin_ref[...]
    lane = lax.broadcasted_iota(I32, cos.shape, 1)
    first_half = (lane % HEAD_DIM) < (HEAD_DIM // 2)

    def rope(t):
        rot = jnp.where(first_half, pltpu.roll(t, LANES - HEAD_DIM // 2, axis=1),
                        pltpu.roll(t, HEAD_DIM // 2, axis=1))
        return t * cos + rot * sin

    aw = ATTN_WIDTH
    for c in range(aw // LANES):
        sl = slice(c * LANES, (c + 1) * LANES)
        qc = _dot(xb, w_ref[:, c * LANES:(c + 1) * LANES])
        q_ref[:, sl] = (rope(qc) * QK_SCALE).astype(BF16)
        kc = _dot(xb, w_ref[:, aw + c * LANES:aw + (c + 1) * LANES])
        k_ref[:, sl] = rope(kc).astype(BF16)
    vt = _dot_nt(wvt_ref[...], xb).astype(BF16)
    ones = jnp.ones((V_ROWS - HEAD_DIM, MOBA_BLOCK), BF16)
    for p in range(v_ref.shape[0]):
        for c in range(v_ref.shape[1]):
            cols = slice(c * MOBA_BLOCK, (c + 1) * MOBA_BLOCK)
            heads = [vt[(2 * p + hh) * HEAD_DIM:(2 * p + hh + 1) * HEAD_DIM, cols] for hh in range(2)]
            v_ref[p, c] = jnp.concatenate([heads[0], ones, heads[1], ones], axis=0)
    u_ref[...] = _dot(xb, w_ref[:, 3 * aw:3 * aw + S5_WIDTH])
    d = ga_ref.shape[1]
    off = 3 * aw + S5_WIDTH
    ga_ref[...] = _dot(xb, w_ref[:, off:off + d])
    gs_ref[...] = _dot(xb, w_ref[:, off + d:off + 2 * d])


def _proj(x2, w_bf, wvt_bf, cos_t, sin_t, tm=512):
    t, d = x2.shape
    n = w_bf.shape[1]
    aw = ATTN_WIDTH
    row = lambda i: (i, 0)
    vshape = (aw // LANES, t // MOBA_BLOCK, 2 * V_ROWS, MOBA_BLOCK)
    return pl.pallas_call(
        _proj_kernel,
        grid=(t // tm,),
        in_specs=[pl.BlockSpec((tm, d), row),
                  pl.BlockSpec((d, n), lambda i: (0, 0)),
                  pl.BlockSpec((aw, d), lambda i: (0, 0)),
                  pl.BlockSpec((tm, LANES), row),
                  pl.BlockSpec((tm, LANES), row)],
        out_specs=[pl.BlockSpec((tm, aw), row), pl.BlockSpec((tm, aw), row),
                   pl.BlockSpec((vshape[0], tm // MOBA_BLOCK) + vshape[2:], lambda i: (0, i, 0, 0)),
                   pl.BlockSpec((tm, S5_WIDTH), row), pl.BlockSpec((tm, d), row), pl.BlockSpec((tm, d), row)],
        out_shape=[jax.ShapeDtypeStruct((t, aw), BF16), jax.ShapeDtypeStruct((t, aw), BF16),
                   jax.ShapeDtypeStruct(vshape, BF16), jax.ShapeDtypeStruct((t, S5_WIDTH), F32),
                   jax.ShapeDtypeStruct((t, d), F32), jax.ShapeDtypeStruct((t, d), F32)],
        compiler_params=_cparams(("parallel",)),
        name="proj_rope",
    )(x2, w_bf, wvt_bf, cos_t, sin_t)


def _attn_kernel(q_ref, k_ref, v_ref, o_ref, kmean_ref):
    i = pl.program_id(1)
    blk = MOBA_BLOCK
    n_blocks = k_ref.shape[0] // blk
    n_pairs = q_ref.shape[1] // LANES

    @pl.when(i == 0)
    def _():
        kmean_ref[...] = jnp.zeros(kmean_ref.shape, kmean_ref.dtype)
        for j in range(n_blocks):
            kj = k_ref[j * blk:(j + 1) * blk, :].astype(F32)
            kmean_ref[j:j + 1, :] = jnp.sum(kj, axis=0, keepdims=True) * (1.0 / blk)

    q_t = q_ref[...].astype(F32).T.astype(BF16)
    sub = lax.broadcasted_iota(I32, (LANES, blk), 0)
    key = lax.broadcasted_iota(I32, (blk, blk), 0)
    qry = lax.broadcasted_iota(I32, (blk, blk), 1)
    causal = key <= qry
    own = pl.ds(pl.multiple_of(i * blk, blk), blk)

    heads = [(pr, hh) for pr in range(n_pairs) for hh in range(2)]
    blk_row = lax.broadcasted_iota(I32, (HEAD_DIM, blk), 0)
    q_heads, gates, own_scores = [], [], []
    for pr, hh in heads:
        cols = slice(pr * LANES, (pr + 1) * LANES)
        q_pair = q_t[pr * LANES:(pr + 1) * LANES]
        in_head = (sub >= hh * HEAD_DIM) & (sub < (hh + 1) * HEAD_DIM)
        qh = jnp.where(in_head, q_pair, jnp.zeros_like(q_pair))
        q_heads.append(qh)
        gates.append(_dot(kmean_ref[:HEAD_DIM, cols].astype(BF16), qh))
        own_scores.append(_dot(k_ref[own, cols], qh))
    q_aug, own_probs = [], []
    for n, (pr, hh) in enumerate(heads):
        g = jnp.where(blk_row < i, gates[n], NEG_INF)
        selected = jnp.zeros(g.shape, jnp.bool_)
        for _ in range(MOBA_TOPK):
            mx = jnp.max(g, axis=0, keepdims=True)
            first = jnp.min(jnp.where(g == mx, blk_row, HEAD_DIM), axis=0, keepdims=True)
            pick = blk_row == first
            selected = selected | (pick & (blk_row < i))
            g = jnp.where(pick, REMOVED, g)
        sel_neg = jnp.where(selected, 0.0, NEG_INF).astype(BF16)
        q_own = q_heads[n][hh * HEAD_DIM:(hh + 1) * HEAD_DIM]
        q_aug.append(jnp.concatenate([q_own, sel_neg] if hh == 0 else [sel_neg, q_own], axis=0))
        s = jnp.where(causal, own_scores[n], NEG_INF)
        mx = jnp.max(s, axis=0, keepdims=True)
        own_probs.append((mx, jnp.exp2(s - mx).astype(BF16)))
    init = []
    for n, (pr, hh) in enumerate(heads):
        mx, p = own_probs[n]
        init.append((mx, _dot(v_ref[pr, i, hh * V_ROWS:(hh + 1) * V_ROWS, :], p)))

    lane2 = lax.broadcasted_iota(I32, (2 * blk, LANES), 1)
    second = (lax.broadcasted_iota(I32, (2 * blk, LANES), 0) >= blk).astype(I32)

    def body(jj, carry):
        rows = pl.ds(pl.multiple_of(jj * (2 * blk), 2 * blk), 2 * blk)
        blk_id = 2 * jj + second
        scores = []
        for n, (pr, hh) in enumerate(heads):
            k_pair = k_ref[rows, pr * LANES:(pr + 1) * LANES]
            other = (1 - hh) * HEAD_DIM
            in_other = (lane2 >= other) & (lane2 < other + HEAD_DIM)
            k_aug = jnp.where(in_other, (lane2 == other + blk_id).astype(BF16), k_pair)
            scores.append(_dot(k_aug, q_aug[n]))
        probs = []
        for n in range(len(heads)):
            m = carry[n][0]
            m_new = jnp.maximum(m, jnp.max(scores[n], axis=0, keepdims=True))
            probs.append((m_new, jnp.exp2(m - m_new), jnp.exp2(scores[n] - m_new).astype(BF16)))
        out = []
        for n, (pr, hh) in enumerate(heads):
            m_new, alpha, p = probs[n]
            vrows = slice(hh * V_ROWS, (hh + 1) * V_ROWS)
            pv = _dot(v_ref[pr, 2 * jj, vrows, :], p[:blk]) + _dot(v_ref[pr, 2 * jj + 1, vrows, :], p[blk:])
            out.append((m_new, alpha * carry[n][1] + pv))
        return tuple(out)

    res = lax.fori_loop(0, (i + 1) // 2, body, tuple(init))
    o_t = jnp.concatenate([acc[:HEAD_DIM] / acc[HEAD_DIM:HEAD_DIM + 1] for _, acc in res], axis=0)
    o_ref[...] = o_t.T.astype(BF16)


def _attention(q, k, v_t, bsz, seq):
    t, aw = q.shape
    blk = MOBA_BLOCK
    nb = seq // blk
    assert nb <= HEAD_DIM
    assert nb % 2 == 0
    qmap = lambda b, i: (b * nb + i, 0)
    return pl.pallas_call(
        _attn_kernel,
        grid=(bsz, nb),
        in_specs=[pl.BlockSpec((blk, aw), qmap),
                  pl.BlockSpec((seq, aw), lambda b, i: (b, 0)),
                  pl.BlockSpec((aw // LANES, nb, 2 * V_ROWS, blk), lambda b, i: (0, b, 0, 0))],
        out_specs=pl.BlockSpec((blk, aw), qmap),
        out_shape=jax.ShapeDtypeStruct((t, aw), BF16),
        scratch_shapes=[pltpu.VMEM((LANES, aw), F32)],
        compiler_params=_cparams(("parallel", "arbitrary")),
        name="moba_attention",
    )(q, k, v_t)


def _s5_tables(lam_re, lam_im, log_dt, b_re, b_im, c_re, c_im):
    hp = lax.Precision.HIGHEST
    L = S5_CHUNK
    lr, li = lam_re.astype(F32), lam_im.astype(F32)
    dt = jnp.exp(log_dt.astype(F32))[:, None]
    mag = jnp.exp(lr * dt)
    a_re = mag * jnp.cos(li * dt)
    a_im = mag * jnp.sin(li * dt)
    den = lr * lr + li * li
    nr = a_re - 1.0
    coef_re = (nr * lr + a_im * li) / den
    coef_im = (a_im * lr - nr * li) / den
    br, bi = b_re.astype(F32), b_im.astype(F32)
    bbar_re = coef_re[..., None] * br - coef_im[..., None] * bi
    bbar_im = coef_re[..., None] * bi + coef_im[..., None] * br
    tau = jnp.arange(L + 1, dtype=F32)[:, None, None]
    pmag = jnp.exp(tau * (lr * dt))
    p_re = pmag * jnp.cos(tau * (li * dt))
    p_im = pmag * jnp.sin(tau * (li * dt))
    cr, ci = c_re.astype(F32), c_im.astype(F32)
    ca_re = cr[None] * p_re[:, :, None, :] - ci[None] * p_im[:, :, None, :]
    ca_im = cr[None] * p_im[:, :, None, :] + ci[None] * p_re[:, :, None, :]
    kmat = (jnp.einsum('tgon,gni->tgio', ca_re[:L], bbar_re, precision=hp)
            - jnp.einsum('tgon,gni->tgio', ca_im[:L], bbar_im, precision=hp))
    s_ix = jnp.arange(L)[:, None]
    t_ix = jnp.arange(L)[None, :]
    lag = jnp.clip(t_ix - s_ix, 0, L - 1)
    m = kmat[lag]
    m = jnp.where((t_ix >= s_ix)[:, :, None, None, None], m, 0.0)
    m_intra = m.transpose(2, 0, 3, 1, 4).reshape(S5_GROUPS, L * S5_GROUP, L * S5_GROUP)
    rev = p_re[L - 1 - jnp.arange(L)], p_im[L - 1 - jnp.arange(L)]
    w_re = rev[0][..., None] * bbar_re[None] - rev[1][..., None] * bbar_im[None]
    w_im = rev[0][..., None] * bbar_im[None] + rev[1][..., None] * bbar_re[None]
    w_re = w_re.transpose(1, 0, 3, 2).reshape(S5_GROUPS, L * S5_GROUP, S5_STATE)
    w_im = w_im.transpose(1, 0, 3, 2).reshape(S5_GROUPS, L * S5_GROUP, S5_STATE)
    w_state = jnp.concatenate([w_re, w_im, w_im, w_re], axis=-1)
    cx_re = ca_re[1:].transpose(1, 3, 0, 2).reshape(S5_GROUPS, S5_STATE, L * S5_GROUP)
    cx_im = ca_im[1:].transpose(1, 3, 0, 2).reshape(S5_GROUPS, S5_STATE, L * S5_GROUP)
    cx = jnp.concatenate([cx_re, -cx_im], axis=1)
    al_re, al_im = p_re[L], p_im[L]
    a1 = jnp.concatenate([al_re, al_re], axis=-1)
    a2 = jnp.concatenate([-al_im, al_im], axis=-1)
    return m_intra.astype(BF16), w_state.astype(BF16), cx.astype(BF16), a1, a2


def _s5_group_kernel(n_seq, u_ref, w_ref, m_ref, cx_ref, a1_ref, a2_ref, y_ref, e_ref, h_ref):
    u = u_ref[0]
    e_ref[...] = _dot(u, w_ref[0])
    a1 = a1_ref[0]
    a2 = a2_ref[0]
    nk = e_ref.shape[0] // n_seq
    half = h_ref.shape[1]

    def step(k, carry):
        out = []
        for b in range(n_seq):
            hp, hq = carry[b]
            row = pl.ds(b * nk + k, 1)
            h_ref[row, :] = hp
            e = e_ref[row, :]
            out.append((a1 * hp + a2 * hq + e[:, :half], a1 * hq - a2 * hp + e[:, half:]))
        return tuple(out)

    zero = jnp.zeros((1, half), F32)
    lax.fori_loop(0, nk, step, tuple((zero, zero) for _ in range(n_seq)), unroll=4)
    y_ref[0] = _dot(u, m_ref[0]) + _dot(h_ref[...].astype(BF16), cx_ref[0])


def _s5_groups(ug, tables, n_seq):
    m_intra, w_state, cx, a1, a2 = tables
    g, r, c = ug.shape
    n2 = cx.shape[1]
    spec = lambda shape: pl.BlockSpec(shape, lambda i: (i, 0, 0))
    return pl.pallas_call(
        functools.partial(_s5_group_kernel, n_seq),
        grid=(g,),
        in_specs=[spec((1, r, c)), spec((1, c, c)), spec((1, c, c)), spec((1, n2, c)),
                  spec((1, 1, n2)), spec((1, 1, n2))],
        out_specs=spec((1, r, c)),
        out_shape=jax.ShapeDtypeStruct((g, r, c), F32),
        scratch_shapes=[pltpu.VMEM((r, c), F32), pltpu.VMEM((r, n2), F32)],
        compiler_params=_cparams(("parallel",)),
        name="s5_groups",
    )(ug, w_state, m_intra, cx, a1[:, None, :], a2[:, None, :])


def _s5_branch(u, tables, bsz, seq):
    L, G, C = S5_CHUNK, S5_GROUPS, S5_GROUP
    nk = seq // L
    ug = u.reshape(bsz, nk, L, G, C).transpose(3, 0, 1, 2, 4).reshape(G, bsz * nk, L * C).astype(BF16)
    y = _s5_groups(ug, tables, bsz)
    return y.reshape(G, bsz, nk, L, C).transpose(1, 2, 3, 0, 4).reshape(bsz * seq, G * C)


def _merge_kernel(alpha, attn_ref, y_ref, u_ref, ga_ref, gs_ref, x_ref, aup_ref, wglu_ref, sup_ref, wo_ref,
                  dskip_ref, bglu_ref, g_ref, b_ref, o_ref, ot_ref):
    a = _dot(attn_ref[...], aup_ref[...])
    y = y_ref[...] + dskip_ref[...] * u_ref[...]
    z = 0.5 * y * (1.0 + jnp.tanh(math.sqrt(2.0 / math.pi) * (y + 0.044715 * (y * y * y))))
    gate = _dot(z.astype(BF16), wglu_ref[...]) + bglu_ref[...]
    s = _dot((z * _sigmoid(gate)).astype(BF16), sup_ref[...])
    mixed = _sigmoid(ga_ref[...]) * a + _sigmoid(gs_ref[...]) * s
    r = alpha * x_ref[...] + _dot(mixed.astype(BF16), wo_ref[...])
    x1 = _layer_norm(r, g_ref[...], b_ref[...])
    o_ref[...] = x1
    tm, nt = x1.shape[0], ot_ref.shape[1]
    sub = x1.shape[1] // nt
    for s in range(sub):
        ot_ref[pl.ds(s, tm, stride=sub), :] = x1[:, s * nt:(s + 1) * nt]


def _merge(alpha, attn, y5, u, g_a, g_s, x2, aup, wglu, sup, wo, dskip, bglu, ln_g, ln_b, tm=256):
    t, d = x2.shape
    row = lambda i: (i, 0)
    full = lambda a: pl.BlockSpec(a.shape, lambda i: (0, 0))
    return pl.pallas_call(
        functools.partial(_merge_kernel, alpha),
        grid=(t // tm,),
        in_specs=[pl.BlockSpec((tm, ATTN_WIDTH), row), pl.BlockSpec((tm, S5_WIDTH), row),
                  pl.BlockSpec((tm, S5_WIDTH), row), pl.BlockSpec((tm, d), row), pl.BlockSpec((tm, d), row),
                  pl.BlockSpec((tm, d), row), full(aup), full(wglu), full(sup), full(wo),
                  full(dskip), full(bglu), full(ln_g), full(ln_b)],
        out_specs=[pl.BlockSpec((tm, d), row), pl.BlockSpec((tm * (d // LANES), LANES), row)],
        out_shape=[jax.ShapeDtypeStruct((t, d), F32), jax.ShapeDtypeStruct((t * (d // LANES), LANES), F32)],
        compiler_params=_cparams(("parallel",)),
        name="merge_ln1",
    )(attn, y5, u, g_a, g_s, x2, aup, wglu, sup, wo, dskip, bglu, ln_g, ln_b)


def _router_kernel(x_ref, wh_ref, wl_ref, bias_ref, idx_ref, wt_ref, rank_ref, cnt_ref, carry_ref):
    tm = x_ref.shape[0]

    @pl.when(pl.program_id(0) == 0)
    def _():
        carry_ref[...] = jnp.zeros(carry_ref.shape, F32)

    x = x_ref[...]
    xh = x.astype(BF16)
    xl = (x - xh.astype(F32)).astype(BF16)
    wh = wh_ref[...]
    logits = _dot_nt(wh, xh) + _dot_nt(wh, xl) + _dot_nt(wl_ref[...], xh)
    scores = _sigmoid(logits)
    biased = scores + bias_ref[...]
    gs_rows = GROUP_SIZE
    row_g = lax.broadcasted_iota(I32, (gs_rows, tm), 0)
    group_score = []
    for g in range(N_EXPERT_GROUPS):
        vg = biased[g * gs_rows:(g + 1) * gs_rows, :]
        m1 = jnp.max(vg, axis=0, keepdims=True)
        i1 = jnp.min(jnp.where(vg == m1, row_g, gs_rows), axis=0, keepdims=True)
        m2 = jnp.max(jnp.where(row_g == i1, REMOVED, vg), axis=0, keepdims=True)
        group_score.append(m1 + m2)
    pieces = []
    for g in range(N_EXPERT_GROUPS):
        beats = jnp.zeros((1, tm), I32)
        for g2 in range(N_EXPERT_GROUPS):
            if g2 == g:
                continue
            better = group_score[g2] > group_score[g]
            if g2 < g:
                better = better | (group_score[g2] == group_score[g])
            beats = beats + better.astype(I32)
        keep = beats < TOPK_GROUPS
        pieces.append(jnp.where(keep, biased[g * gs_rows:(g + 1) * gs_rows, :], NEG_INF))
    v = jnp.concatenate(pieces, axis=0)
    row = lax.broadcasted_iota(I32, (N_EXPERTS, tm), 0)
    idxs, wts = [], []
    multi_hot = jnp.zeros((N_EXPERTS, tm), F32)
    for _ in range(TOP_K):
        m = jnp.max(v, axis=0, keepdims=True)
        ix = jnp.min(jnp.where(v == m, row, N_EXPERTS), axis=0, keepdims=True)
        hit = row == ix
        idxs.append(ix)
        wts.append(jnp.sum(jnp.where(hit, scores, 0.0), axis=0, keepdims=True))
        multi_hot = multi_hot + hit.astype(F32)
        v = jnp.where(hit, REMOVED, v)
    wsum = wts[0]
    for k in range(1, TOP_K):
        wsum = wsum + wts[k]
    s_ix = lax.broadcasted_iota(I32, (tm, tm), 0)
    t_ix = lax.broadcasted_iota(I32, (tm, tm), 1)
    mh = multi_hot.astype(BF16)
    before = _dot(mh, (s_ix < t_ix).astype(BF16)) + carry_ref[...]
    carry_ref[...] = carry_ref[...] + _dot(mh, jnp.ones((tm, tm), BF16))
    for k in range(TOP_K):
        idx_ref[k:k + 1, :] = idxs[k]
        wt_ref[k:k + 1, :] = wts[k] / wsum * ROUTED_SCALE
        rank_ref[k:k + 1, :] = jnp.sum(jnp.where(row == idxs[k], before, 0.0), axis=0, keepdims=True).astype(I32)
    cnt_ref[...] = carry_ref[:, :LANES]


def _router(x1, wr_hi, wr_lo, bias_b, tm=256):
    t, d = x1.shape
    e = wr_hi.shape[0]
    tok = lambda i: (0, i)
    const = lambda i: (0, 0)
    return pl.pallas_call(
        _router_kernel,
        grid=(t // tm,),
        in_specs=[pl.BlockSpec((tm, d), lambda i: (i, 0)), pl.BlockSpec((e, d), const),
                  pl.BlockSpec((e, d), const), pl.BlockSpec((e, tm), const)],
        out_specs=[pl.BlockSpec((TOP_K, tm), tok), pl.BlockSpec((TOP_K, tm), tok),
                   pl.BlockSpec((TOP_K, tm), tok), pl.BlockSpec((e, LANES), const)],
        out_shape=[jax.ShapeDtypeStruct((TOP_K, t), I32), jax.ShapeDtypeStruct((TOP_K, t), F32),
                   jax.ShapeDtypeStruct((TOP_K, t), I32), jax.ShapeDtypeStruct((e, LANES), F32)],
        scratch_shapes=[pltpu.VMEM((e, tm), F32)],
        compiler_params=_cparams(("arbitrary",)),
        name="router_topk_rank",
    )(x1, wr_hi, wr_lo, bias_b)


def _dest_kernel(idx_ref, rank_ref, pstart_ref, dest_ref):
    pstart = pstart_ref[...]
    row = lax.broadcasted_iota(I32, pstart.shape, 0)
    for k in range(TOP_K):
        start = jnp.sum(jnp.where(row == idx_ref[k:k + 1, :], pstart, 0.0), axis=0, keepdims=True)
        dest_ref[k:k + 1, :] = start.astype(I32) + rank_ref[k:k + 1, :]


def _dest_rows(top_idx, rank, pstart_b):
    tm = pstart_b.shape[1]
    tok = lambda i: (0, i)
    return pl.pallas_call(
        _dest_kernel,
        grid=(top_idx.shape[1] // tm,),
        in_specs=[pl.BlockSpec((TOP_K, tm), tok), pl.BlockSpec((TOP_K, tm), tok),
                  pl.BlockSpec(pstart_b.shape, lambda i: (0, 0))],
        out_specs=pl.BlockSpec((TOP_K, tm), tok),
        out_shape=jax.ShapeDtypeStruct(top_idx.shape, I32),
        compiler_params=_cparams(("parallel",)),
        name="moe_dest_rows",
    )(top_idx, rank, pstart_b)


def _rows_wait(src_hbm, dst, sem, n_rows):
    pltpu.make_async_copy(src_hbm.at[pl.ds(0, n_rows), :], dst, sem).wait()


def _dispatch_kernel(pend_ref, cnt_ref, dest_ref, x_ref, xs_hbm, zbuf, zsem, sem):
    i = pl.program_id(0)
    tm = dest_ref.shape[1]
    sub = x_ref.shape[0] // tm
    mbp = zbuf.shape[0]
    mb = mbp // sub

    @pl.when(i == 0)
    def _():
        zbuf[...] = jnp.zeros(zbuf.shape, zbuf.dtype)

        def zero_copy(e):
            start = pl.multiple_of((pend_ref[e] - mb) * sub, mbp)
            return pltpu.make_async_copy(zbuf, xs_hbm.at[pl.ds(start, mbp), :], zsem)

        def z_start(e, c):
            @pl.when(cnt_ref[e] > 0)
            def _():
                zero_copy(e).start()
            return c

        def z_wait(e, c):
            @pl.when(cnt_ref[e] > 0)
            def _():
                zero_copy(e).wait()
            return c

        def tail_copy(b):
            return pltpu.make_async_copy(zbuf, xs_hbm.at[pl.ds(pl.multiple_of(b * mbp, mbp), mbp), :], zsem)

        def t_start(b, c):
            tail_copy(b).start()
            return c

        def t_wait(b, c):
            tail_copy(b).wait()
            return c

        n_used = pend_ref[N_EXPERTS - 1] // mb
        n_blocks = xs_hbm.shape[0] // mbp
        lax.fori_loop(0, N_EXPERTS, z_start, 0)
        lax.fori_loop(n_used, n_blocks, t_start, 0)
        lax.fori_loop(0, N_EXPERTS, z_wait, 0)
        lax.fori_loop(n_used, n_blocks, t_wait, 0)

    def issue(t, c):
        src = x_ref.at[pl.ds(pl.multiple_of(t * sub, sub), sub), :]
        for k in range(TOP_K):
            dst_rows = pl.ds(pl.multiple_of(dest_ref[k, t] * sub, sub), sub)
            pltpu.make_async_copy(src, xs_hbm.at[dst_rows, :], sem).start()
        return c

    lax.fori_loop(0, tm, issue, 0, unroll=2)
    for _ in range(TOP_K):
        pltpu.make_async_copy(x_ref, xs_hbm.at[pl.ds(0, tm * sub), :], sem).wait()


def _dispatch(padded_ends, counts, dest, x1_tiled, n_rows, tm=1024):
    rows, lanes = x1_tiled.shape
    t = dest.shape[1]
    sub = rows // t
    grid_spec = pltpu.PrefetchScalarGridSpec(
        num_scalar_prefetch=2,
        grid=(t // tm,),
        in_specs=[pl.BlockSpec((TOP_K, tm), lambda i, pe, cn: (0, i), memory_space=pltpu.SMEM),
                  pl.BlockSpec((tm * sub, lanes), lambda i, pe, cn: (i, 0))],
        out_specs=pl.BlockSpec(memory_space=pl.ANY),
        scratch_shapes=[pltpu.VMEM((MOE_ROWS * sub, lanes), x1_tiled.dtype), pltpu.SemaphoreType.DMA(()),
                        pltpu.SemaphoreType.DMA(())],
    )
    return pl.pallas_call(
        _dispatch_kernel,
        grid_spec=grid_spec,
        out_shape=jax.ShapeDtypeStruct((n_rows * sub, lanes), x1_tiled.dtype),
        compiler_params=_cparams(("arbitrary",)),
        name="moe_dispatch",
    )(padded_ends, counts, dest, x1_tiled)


def _expert_kernel(first_ref, nblk_ref, nused_ref, xs_hbm, wg_ref, wu_ref, wd_ref, y_hbm,
                   xbuf, ybuf, wgb, wub, wdb, xsem, ysem):
    e = pl.program_id(0)
    n_used = nused_ref[0]
    nt = y_hbm.shape[1]
    sub = wgb.shape[0] // nt
    nx, mb = xbuf.shape[0], xbuf.shape[1] // sub

    def x_copy(g, slot):
        return pltpu.make_async_copy(
            xs_hbm.at[pl.ds(pl.multiple_of(g * (mb * sub), mb * sub), mb * sub), :], xbuf.at[slot], xsem.at[slot])

    def y_copy(g, slot):
        return pltpu.make_async_copy(
            ybuf.at[slot], y_hbm.at[pl.ds(pl.multiple_of(g * (mb * sub), mb * sub), mb * sub), :], ysem.at[slot])

    @pl.when(e == 0)
    def _():
        for s in range(nx - 1):
            @pl.when(s < n_used)
            def _():
                x_copy(s, s).start()

    wgb[...] = wg_ref[0, 0].astype(BF16)
    wub[...] = wu_ref[0, 0].astype(BF16)
    wdb[...] = wd_ref[0, 0].astype(BF16)

    def block(b, c):
        g = first_ref[e] + b
        slot = g % nx
        yslot = g % 2
        ahead = g + (nx - 1)

        @pl.when(ahead < n_used)
        def _():
            x_copy(ahead, ahead % nx).start()

        x_copy(g, slot).wait()

        @pl.when(g >= 2)
        def _():
            y_copy(g - 2, yslot).wait()

        xb = jnp.concatenate([xbuf[slot, pl.ds(s, mb, stride=sub), :] for s in range(sub)], axis=1).astype(BF16)
        hg = _dot(xb, wgb[...])
        hu = _dot(xb, wub[...])
        h = hg * _sigmoid(hg) * hu
        y = _dot(h.astype(BF16), wdb[...])
        for s in range(sub):
            ybuf[yslot, pl.ds(s, mb, stride=sub), :] = y[:, s * nt:(s + 1) * nt]
        y_copy(g, yslot).start()
        return c

    lax.fori_loop(0, nblk_ref[e], block, 0)

    @pl.when(e == pl.num_programs(0) - 1)
    def _():
        @pl.when(n_used >= 2)
        def _():
            y_copy(n_used - 2, n_used % 2).wait()

        y_copy(n_used - 1, (n_used - 1) % 2).wait()
        ybuf[0] = jnp.zeros(ybuf.shape[1:], ybuf.dtype)
        n_blocks = y_hbm.shape[0] // (mb * sub)

        def z_start(g, c):
            y_copy(g, 0).start()
            return c

        def z_wait(g, c):
            y_copy(g, 0).wait()
            return c

        lax.fori_loop(n_used, n_blocks, z_start, 0)
        lax.fori_loop(n_used, n_blocks, z_wait, 0)


def _experts(layer, first_block, n_blk, n_used, xs, w_gate_e, w_up_e, w_down_e):
    mb = MOE_ROWS
    _, n_exp, d, hdim = w_gate_e.shape
    n_rows = xs.shape[0] // (d // LANES)
    wmap = lambda e, fb, nb, nu: (layer, e, 0, 0)
    grid_spec = pltpu.PrefetchScalarGridSpec(
        num_scalar_prefetch=3,
        grid=(n_exp,),
        in_specs=[pl.BlockSpec(memory_space=pl.ANY),
                  pl.BlockSpec((1, 1, d, hdim), wmap),
                  pl.BlockSpec((1, 1, d, hdim), wmap),
                  pl.BlockSpec((1, 1, hdim, d), wmap)],
        out_specs=pl.BlockSpec(memory_space=pl.ANY),
        scratch_shapes=[pltpu.VMEM((EXPERT_IN_SLOTS, mb * (d // LANES), LANES), xs.dtype),
                        pltpu.VMEM((2, mb * (d // LANES), LANES), F32),
                        pltpu.VMEM((d, hdim), BF16), pltpu.VMEM((d, hdim), BF16), pltpu.VMEM((hdim, d), BF16),
                        pltpu.SemaphoreType.DMA((EXPERT_IN_SLOTS,)), pltpu.SemaphoreType.DMA((2,))],
    )
    return pl.pallas_call(
        _expert_kernel,
        grid_spec=grid_spec,
        out_shape=jax.ShapeDtypeStruct((n_rows * (d // LANES), LANES), F32),
        compiler_params=_cparams(("arbitrary",)),
        name="routed_experts",
    )(first_block, n_blk, n_used, xs, w_gate_e, w_up_e, w_down_e)


def _final_kernel(alpha, dcur_ref, dnext_ref, x_ref, wt_ref, y_hbm, wsg_ref, wsu_ref, wsd_ref, g_ref, b_ref,
                  o_ref, ybuf, sems):
    i = pl.program_id(0)
    n_steps = pl.num_programs(0)
    tm = x_ref.shape[0]
    slot = i % 2

    sub = ybuf.shape[2] // tm
    nt = ybuf.shape[3]

    def gather(dref, s):
        def issue(t, c):
            dst_rows = pl.ds(pl.multiple_of(t * sub, sub), sub)
            for k in range(TOP_K):
                src_rows = pl.ds(pl.multiple_of(dref[k, t] * sub, sub), sub)
                pltpu.make_async_copy(y_hbm.at[src_rows, :], ybuf.at[s, k, dst_rows, :], sems.at[s]).start()
            return c

        lax.fori_loop(0, tm, issue, 0, unroll=2)

    @pl.when(i == 0)
    def _():
        gather(dcur_ref, 0)

    @pl.when(i + 1 < n_steps)
    def _():
        gather(dnext_ref, 1 - slot)

    x = x_ref[...]
    xb = x.astype(BF16)
    hg = _dot(xb, wsg_ref[...])
    hu = _dot(xb, wsu_ref[...])
    f = _dot((hg * _sigmoid(hg) * hu).astype(BF16), wsd_ref[...])
    for k in range(TOP_K):
        _rows_wait(y_hbm, ybuf.at[slot, k], sems.at[slot], tm * sub)
    wt = wt_ref[...]
    cols = []
    for s in range(sub):
        fs = f[:, s * nt:(s + 1) * nt]
        for k in range(TOP_K):
            fs = fs + wt[:, k:k + 1] * ybuf[slot, k, pl.ds(s, tm, stride=sub), :]
        cols.append(fs)
    f = jnp.concatenate(cols, axis=1)
    o_ref[...] = _layer_norm(alpha * x + f, g_ref[...], b_ref[...])


def _final(alpha, dest, x1, wt_tok, y_sorted, wsg, wsu, wsd, ln_g, ln_b, tm=256):
    t, d = x1.shape
    n = t // tm
    row = lambda i: (i, 0)
    full = lambda a: pl.BlockSpec(a.shape, lambda i: (0, 0))
    return pl.pallas_call(
        functools.partial(_final_kernel, alpha),
        grid=(n,),
        in_specs=[pl.BlockSpec((TOP_K, tm), lambda i: (0, i), memory_space=pltpu.SMEM),
                  pl.BlockSpec((TOP_K, tm), lambda i: (0, jnp.minimum(i + 1, n - 1)), memory_space=pltpu.SMEM),
                  pl.BlockSpec((tm, d), row), pl.BlockSpec((tm, TOP_K), row),
                  pl.BlockSpec(memory_space=pl.ANY), full(wsg), full(wsu), full(wsd), full(ln_g), full(ln_b)],
        out_specs=pl.BlockSpec((tm, d), row),
        out_shape=jax.ShapeDtypeStruct((t, d), F32),
        scratch_shapes=[pltpu.VMEM((2, TOP_K, tm * (d // LANES), LANES), F32), pltpu.SemaphoreType.DMA((2,))],
        compiler_params=_cparams(("arbitrary",)),
        name="combine_shared_ln2",
    )(dest, dest, x1, wt_tok, y_sorted, wsg, wsu, wsd, ln_g, ln_b)


def _block_plan(counts, n_tok):
    mb = MOE_ROWS
    n_blocks = -(-(n_tok * TOP_K + N_EXPERTS * (mb - 1)) // mb)
    padded = (counts + mb - 1) // mb * mb
    padded_ends = jnp.cumsum(padded).astype(I32)
    padded_starts = padded_ends - padded
    n_used = (padded_ends[-1] // mb).astype(I32).reshape(1)
    return padded_starts, padded_ends, padded_starts // mb, padded // mb, n_used, n_blocks * mb


def kernel(x, positions, w_in, attn_up, lam_re, lam_im, log_dt, b_re, b_im, c_re, c_im, d_skip, w_glu, b_glu,
           s5_up, w_o, ln1_g, ln1_b, w_router, router_bias, w_gate_e, w_up_e, w_down_e, ws_gate, ws_up, ws_down,
           ln2_g, ln2_b):
    bsz, seq, d = x.shape
    depth = w_in.shape[0]
    t = bsz * seq
    alpha = (2 * depth) ** 0.25
    assert seq % MOBA_BLOCK == 0 and seq % S5_CHUNK == 0 and t % 512 == 0

    inv_freq = ROPE_THETA ** (-jnp.arange(0, HEAD_DIM, 2, dtype=F32) / HEAD_DIM)
    ang = positions.astype(F32).reshape(t, 1) * inv_freq
    cos, sin = jnp.cos(ang), jnp.sin(ang)
    cos_t = jnp.concatenate([cos, cos, cos, cos], axis=1)
    sin_t = jnp.concatenate([-sin, sin, -sin, sin], axis=1)

    xc = x.reshape(t, d)
    row2 = lambda a: a.reshape(1, -1).astype(F32)
    for l in range(depth):
        w_v_t = w_in[l][:, 2 * ATTN_WIDTH:3 * ATTN_WIDTH].T.astype(BF16)
        q, k, v_t, u, g_a, g_s = _proj(xc, w_in[l].astype(BF16), w_v_t, cos_t, sin_t)
        attn = _attention(q, k, v_t, bsz, seq)
        tables = _s5_tables(lam_re[l], lam_im[l], log_dt[l], b_re[l], b_im[l], c_re[l], c_im[l])
        y5 = _s5_branch(u, tables, bsz, seq)
        x1, x1_tiled = _merge(alpha, attn, y5, u, g_a, g_s, xc, attn_up[l].astype(BF16), w_glu[l].astype(BF16),
                              s5_up[l].astype(BF16), w_o[l].astype(BF16), row2(d_skip[l]), row2(b_glu[l]),
                              row2(ln1_g[l]), row2(ln1_b[l]))
        wr_t = w_router[l].astype(F32).T
        wr_hi = wr_t.astype(BF16)
        wr_lo = (wr_t - wr_hi.astype(F32)).astype(BF16)
        bias_b = jnp.broadcast_to(router_bias[l].astype(F32)[:, None], (N_EXPERTS, 256))
        top_idx, top_w, rank, counts = _router(x1, wr_hi, wr_lo, bias_b)
        counts = counts[:, 0].astype(I32)
        pstart, pend, first_block, n_blk, n_used, n_rows = _block_plan(counts, t)
        dest = _dest_rows(top_idx, rank, jnp.broadcast_to(pstart.astype(F32)[:, None], (N_EXPERTS, 512)))
        xs = _dispatch(pend, counts, dest, x1_tiled, n_rows)
        y_sorted = _experts(l, first_block, n_blk, n_used, xs, w_gate_e, w_up_e, w_down_e)
        xc = _final(alpha, dest, x1, top_w.T, y_sorted, ws_gate[l].astype(BF16), ws_up[l].astype(BF16),
                    ws_down[l].astype(BF16), row2(ln2_g[l]), row2(ln2_b[l]))
    return xc.reshape(bsz, seq, d)
```

```python
import functools
import math

import jax
import jax.numpy as jnp
from jax import lax
from jax.experimental import pallas as pl
from jax.experimental.pallas import tpu as pltpu

F32 = jnp.float32
BF16 = jnp.bfloat16
I32 = jnp.int32

N_HEADS = 8
HEAD_DIM = 64
ATTN_WIDTH = N_HEADS * HEAD_DIM
MOBA_BLOCK = 256
MOBA_TOPK = 3
ROPE_THETA = 10000.0
S5_GROUP = 16
S5_GROUPS = 32
S5_WIDTH = S5_GROUP * S5_GROUPS
S5_STATE = 64
S5_CHUNK = 16
N_EXPERTS = 256
TOP_K = 8
N_EXPERT_GROUPS = 8
TOPK_GROUPS = 4
GROUP_SIZE = N_EXPERTS // N_EXPERT_GROUPS
ROUTED_SCALE = 2.5
LN_EPS = 1e-5
NEG_INF = -1e30
REMOVED = -3e38
QK_SCALE = HEAD_DIM ** -0.5 * math.log2(math.e)
V_ROWS = HEAD_DIM + 16

LANES = 128
MOE_ROWS = 256
EXPERT_IN_SLOTS = 4
VMEM_LIMIT = 56 * 1024 * 1024


def _cparams(sem):
    return pltpu.CompilerParams(dimension_semantics=sem, vmem_limit_bytes=VMEM_LIMIT)


def _sigmoid(x):
    return 1.0 / (1.0 + jnp.exp(-x))


def _layer_norm(r, g, b):
    mu = jnp.mean(r, axis=-1, keepdims=True)
    rc = r - mu
    var = jnp.mean(rc * rc, axis=-1, keepdims=True)
    return rc * lax.rsqrt(var + LN_EPS) * g + b


def _dot(a, b):
    return jnp.dot(a, b, preferred_element_type=F32)


def _dot_nt(a, b):
    return lax.dot_general(a, b, (((1,), (1,)), ((), ())), preferred_element_type=F32)


def _proj_kernel(x_ref, w_ref, wvt_ref, cos_ref, sin_ref, q_ref, k_ref, v_ref, u_ref, ga_ref, gs_ref):
    xb = x_ref[...].astype(BF16)
    cos = cos_ref[...]
    sin = sin_ref[...]
    lane = lax.broadcasted_iota(I32, cos.shape, 1)
    first_half = (lane % HEAD_DIM) < (HEAD_DIM // 2)

    def rope(t):
        rot = jnp.where(first_half, pltpu.roll(t, LANES - HEAD_DIM // 2, axis=1),
                        pltpu.roll(t, HEAD_DIM // 2, axis=1))
        return t * cos + rot * sin

    aw = ATTN_WIDTH
    for c in range(aw // LANES):
        sl = slice(c * LANES, (c + 1) * LANES)
        qc = _dot(xb, w_ref[:, c * LANES:(c + 1) * LANES])
        q_ref[:, sl] = (rope(qc) * QK_SCALE).astype(BF16)
        kc = _dot(xb, w_ref[:, aw + c * LANES:aw + (c + 1) * LANES])
        k_ref[:, sl] = rope(kc).astype(BF16)
    vt = _dot_nt(wvt_ref[...], xb).astype(BF16)
    ones = jnp.ones((V_ROWS - HEAD_DIM, MOBA_BLOCK), BF16)
    for p in range(v_ref.shape[0]):
        for c in range(v_ref.shape[1]):
            cols = slice(c * MOBA_BLOCK, (c + 1) * MOBA_BLOCK)
            heads = [vt[(2 * p + hh) * HEAD_DIM:(2 * p + hh + 1) * HEAD_DIM, cols] for hh in range(2)]
            v_ref[p, c] = jnp.concatenate([heads[0], ones, heads[1], ones], axis=0)
    u_ref[...] = _dot(xb, w_ref[:, 3 * aw:3 * aw + S5_WIDTH])
    d = ga_ref.shape[1]
    off = 3 * aw + S5_WIDTH
    ga_ref[...] = _dot(xb, w_ref[:, off:off + d])
    gs_ref[...] = _dot(xb, w_ref[:, off + d:off + 2 * d])


def _proj(x2, w_bf, wvt_bf, cos_t, sin_t, tm=512):
    t, d = x2.shape
    n = w_bf.shape[1]
    aw = ATTN_WIDTH
    row = lambda i: (i, 0)
    vshape = (aw // LANES, t // MOBA_BLOCK, 2 * V_ROWS, MOBA_BLOCK)
    return pl.pallas_call(
        _proj_kernel,
        grid=(t // tm,),
        in_specs=[pl.BlockSpec((tm, d), row),
                  pl.BlockSpec((d, n), lambda i: (0, 0)),
                  pl.BlockSpec((aw, d), lambda i: (0, 0)),
                  pl.BlockSpec((tm, LANES), row),
                  pl.BlockSpec((tm, LANES), row)],
        out_specs=[pl.BlockSpec((tm, aw), row), pl.BlockSpec((tm, aw), row),
                   pl.BlockSpec((vshape[0], tm // MOBA_BLOCK) + vshape[2:], lambda i: (0, i, 0, 0)),
                   pl.BlockSpec((tm, S5_WIDTH), row), pl.BlockSpec((tm, d), row), pl.BlockSpec((tm, d), row)],
        out_shape=[jax.ShapeDtypeStruct((t, aw), BF16), jax.ShapeDtypeStruct((t, aw), BF16),
                   jax.ShapeDtypeStruct(vshape, BF16), jax.ShapeDtypeStruct((t, S5_WIDTH), F32),
                   jax.ShapeDtypeStruct((t, d), F32), jax.ShapeDtypeStruct((t, d), F32)],
        compiler_params=_cparams(("parallel",)),
        name="proj_rope",
    )(x2, w_bf, wvt_bf, cos_t, sin_t)


def _attn_kernel(q_ref, k_ref, v_ref, o_ref, kmean_ref):
    i = pl.program_id(1)
    blk = MOBA_BLOCK
    n_blocks = k_ref.shape[0] // blk
    n_pairs = q_ref.shape[1] // LANES

    @pl.when(i == 0)
    def _():
        kmean_ref[...] = jnp.zeros(kmean_ref.shape, kmean_ref.dtype)
        for j in range(n_blocks):
            kj = k_ref[j * blk:(j + 1) * blk, :].astype(F32)
            kmean_ref[j:j + 1, :] = jnp.sum(kj, axis=0, keepdims=True) * (1.0 / blk)

    q_t = q_ref[...].astype(F32).T.astype(BF16)
    sub = lax.broadcasted_iota(I32, (LANES, blk), 0)
    key = lax.broadcasted_iota(I32, (blk, blk), 0)
    qry = lax.broadcasted_iota(I32, (blk, blk), 1)
    causal = key <= qry
    own = pl.ds(pl.multiple_of(i * blk, blk), blk)

    heads = [(pr, hh) for pr in range(n_pairs) for hh in range(2)]
    blk_row = lax.broadcasted_iota(I32, (HEAD_DIM, blk), 0)
    q_heads, gates, own_scores = [], [], []
    for pr, hh in heads:
        cols = slice(pr * LANES, (pr + 1) * LANES)
        q_pair = q_t[pr * LANES:(pr + 1) * LANES]
        in_head = (sub >= hh * HEAD_DIM) & (sub < (hh + 1) * HEAD_DIM)
        qh = jnp.where(in_head, q_pair, jnp.zeros_like(q_pair))
        q_heads.append(qh)
        gates.append(_dot(kmean_ref[:HEAD_DIM, cols].astype(BF16), qh))
        own_scores.append(_dot(k_ref[own, cols], qh))
    q_aug, own_probs = [], []
    for n, (pr, hh) in enumerate(heads):
        g = jnp.where(blk_row < i, gates[n], NEG_INF)
        selected = jnp.zeros(g.shape, jnp.bool_)
        for _ in range(MOBA_TOPK):
            mx = jnp.max(g, axis=0, keepdims=True)
            first = jnp.min(jnp.where(g == mx, blk_row, HEAD_DIM), axis=0, keepdims=True)
            pick = blk_row == first
            selected = selected | (pick & (blk_row < i))
            g = jnp.where(pick, REMOVED, g)
        sel_neg = jnp.where(selected, 0.0, NEG_INF).astype(BF16)
        q_own = q_heads[n][hh * HEAD_DIM:(hh + 1) * HEAD_DIM]
        q_aug.append(jnp.concatenate([q_own, sel_neg] if hh == 0 else [sel_neg, q_own], axis=0))
        s = jnp.where(causal, own_scores[n], NEG_INF)
        mx = jnp.max(s, axis=0, keepdims=True)
        own_probs.append((mx, jnp.exp2(s - mx).astype(BF16)))
    init = []
    for n, (pr, hh) in enumerate(heads):
        mx, p = own_probs[n]
        init.append((mx, _dot(v_ref[pr, i, hh * V_ROWS:(hh + 1) * V_ROWS, :], p)))

    lane2 = lax.broadcasted_iota(I32, (2 * blk, LANES), 1)
    second = (lax.broadcasted_iota(I32, (2 * blk, LANES), 0) >= blk).astype(I32)

    def body(jj, carry):
        rows = pl.ds(pl.multiple_of(jj * (2 * blk), 2 * blk), 2 * blk)
        blk_id = 2 * jj + second
        scores = []
        for n, (pr, hh) in enumerate(heads):
            k_pair = k_ref[rows, pr * LANES:(pr + 1) * LANES]
            other = (1 - hh) * HEAD_DIM
            in_other = (lane2 >= other) & (lane2 < other + HEAD_DIM)
            k_aug = jnp.where(in_other, (lane2 == other + blk_id).astype(BF16), k_pair)
            scores.append(_dot(k_aug, q_aug[n]))
        probs = []
        for n in range(len(heads)):
            m = carry[n][0]
            m_new = jnp.maximum(m, jnp.max(scores[n], axis=0, keepdims=True))
            probs.append((m_new, jnp.exp2(m - m_new), jnp.exp2(scores[n] - m_new).astype(BF16)))
        out = []
        for n, (pr, hh) in enumerate(heads):
            m_new, alpha, p = probs[n]
            vrows = slice(hh * V_ROWS, (hh + 1) * V_ROWS)
            pv = _dot(v_ref[pr, 2 * jj, vrows, :], p[:blk]) + _dot(v_ref[pr, 2 * jj + 1, vrows, :], p[blk:])
            out.append((m_new, alpha * carry[n][1] + pv))
        return tuple(out)

    res = lax.fori_loop(0, (i + 1) // 2, body, tuple(init))
    o_t = jnp.concatenate([acc[:HEAD_DIM] / acc[HEAD_DIM:HEAD_DIM + 1] for _, acc in res], axis=0)
    o_ref[...] = o_t.T.astype(BF16)


def _attention(q, k, v_t, bsz, seq):
    t, aw = q.shape
    blk = MOBA_BLOCK
    nb = seq // blk
    assert nb <= HEAD_DIM
    assert nb % 2 == 0
    qmap = lambda b, i: (b * nb + i, 0)
    return pl.pallas_call(
        _attn_kernel,
        grid=(bsz, nb),
        in_specs=[pl.BlockSpec((blk, aw), qmap),
                  pl.BlockSpec((seq, aw), lambda b, i: (b, 0)),
                  pl.BlockSpec((aw // LANES, nb, 2 * V_ROWS, blk), lambda b, i: (0, b, 0, 0))],
        out_specs=pl.BlockSpec((blk, aw), qmap),
        out_shape=jax.ShapeDtypeStruct((t, aw), BF16),
        scratch_shapes=[pltpu.VMEM((LANES, aw), F32)],
        compiler_params=_cparams(("parallel", "arbitrary")),
        name="moba_attention",
    )(q, k, v_t)


def _s5_tables(lam_re, lam_im, log_dt, b_re, b_im, c_re, c_im):
    hp = lax.Precision.HIGHEST
    L = S5_CHUNK
    lr, li = lam_re.astype(F32), lam_im.astype(F32)
    dt = jnp.exp(log_dt.astype(F32))[:, None]
    mag = jnp.exp(lr * dt)
    a_re = mag * jnp.cos(li * dt)
    a_im = mag * jnp.sin(li * dt)
    den = lr * lr + li * li
    nr = a_re - 1.0
    coef_re = (nr * lr + a_im * li) / den
    coef_im = (a_im * lr - nr * li) / den
    br, bi = b_re.astype(F32), b_im.astype(F32)
    bbar_re = coef_re[..., None] * br - coef_im[..., None] * bi
    bbar_im = coef_re[..., None] * bi + coef_im[..., None] * br
    tau = jnp.arange(L + 1, dtype=F32)[:, None, None]
    pmag = jnp.exp(tau * (lr * dt))
    p_re = pmag * jnp.cos(tau * (li * dt))
    p_im = pmag * jnp.sin(tau * (li * dt))
    cr, ci = c_re.astype(F32), c_im.astype(F32)
    ca_re = cr[None] * p_re[:, :, None, :] - ci[None] * p_im[:, :, None, :]
    ca_im = cr[None] * p_im[:, :, None, :] + ci[None] * p_re[:, :, None, :]
    kmat = (jnp.einsum('tgon,gni->tgio', ca_re[:L], bbar_re, precision=hp)
            - jnp.einsum('tgon,gni->tgio', ca_im[:L], bbar_im, precision=hp))
    s_ix = jnp.arange(L)[:, None]
    t_ix = jnp.arange(L)[None, :]
    lag = jnp.clip(t_ix - s_ix, 0, L - 1)
    m = kmat[lag]
    m = jnp.where((t_ix >= s_ix)[:, :, None, None, None], m, 0.0)
    m_intra = m.transpose(2, 0, 3, 1, 4).reshape(S5_GROUPS, L * S5_GROUP, L * S5_GROUP)
    rev = p_re[L - 1 - jnp.arange(L)], p_im[L - 1 - jnp.arange(L)]
    w_re = rev[0][..., None] * bbar_re[None] - rev[1][..., None] * bbar_im[None]
    w_im = rev[0][..., None] * bbar_im[None] + rev[1][..., None] * bbar_re[None]
    w_re = w_re.transpose(1, 0, 3, 2).reshape(S5_GROUPS, L * S5_GROUP, S5_STATE)
    w_im = w_im.transpose(1, 0, 3, 2).reshape(S5_GROUPS, L * S5_GROUP, S5_STATE)
    w_state = jnp.concatenate([w_re, w_im, w_im, w_re], axis=-1)
    cx_re = ca_re[1:].transpose(1, 3, 0, 2).reshape(S5_GROUPS, S5_STATE, L * S5_GROUP)
    cx_im = ca_im[1:].transpose(1, 3, 0, 2).reshape(S5_GROUPS, S5_STATE, L * S5_GROUP)
    cx = jnp.concatenate([cx_re, -cx_im], axis=1)
    al_re, al_im = p_re[L], p_im[L]
    a1 = jnp.concatenate([al_re, al_re], axis=-1)
    a2 = jnp.concatenate([-al_im, al_im], axis=-1)
    return m_intra.astype(BF16), w_state.astype(BF16), cx.astype(BF16), a1, a2


def _s5_group_kernel(n_seq, u_ref, p_ref, w_ref, m_ref, cx_ref, a1_ref, a2_ref, y_ref, e_ref, h_ref):
    r = e_ref.shape[0]
    chunk = u_ref.shape[0] // r
    ucat = jnp.concatenate([u_ref[pl.ds(s, r, stride=chunk), :] for s in range(chunk)], axis=1)
    u = _dot(ucat.astype(BF16), p_ref[0]).astype(BF16)
    e_ref[...] = _dot(u, w_ref[0])
    a1 = a1_ref[0]
    a2 = a2_ref[0]
    nk = e_ref.shape[0] // n_seq
    half = h_ref.shape[1]

    def step(k, carry):
        out = []
        for b in range(n_seq):
            hp, hq = carry[b]
            row = pl.ds(b * nk + k, 1)
            h_ref[row, :] = hp
            e = e_ref[row, :]
            out.append((a1 * hp + a2 * hq + e[:, :half], a1 * hq - a2 * hp + e[:, half:]))
        return tuple(out)

    zero = jnp.zeros((1, half), F32)
    lax.fori_loop(0, nk, step, tuple((zero, zero) for _ in range(n_seq)), unroll=4)
    y_ref[0] = _dot(u, m_ref[0]) + _dot(h_ref[...].astype(BF16), cx_ref[0])


def _s5_groups(u, tables, n_seq):
    m_intra, w_state, cx, a1, a2 = tables
    t = u.shape[0]
    g, n2, c = cx.shape
    r = t // S5_CHUNK
    per_tile = LANES // S5_GROUP
    row = jnp.arange(S5_CHUNK * LANES)
    col = jnp.arange(c)
    same_token = (row // LANES)[:, None] == (col // S5_GROUP)[None, :]
    lane_hit = (row % LANES)[None, :, None] == (jnp.arange(per_tile)[:, None, None] * S5_GROUP
                                                + (col % S5_GROUP)[None, None, :])
    placement = (same_token[None] & lane_hit).astype(BF16)
    spec = lambda shape: pl.BlockSpec(shape, lambda i: (i, 0, 0))
    return pl.pallas_call(
        functools.partial(_s5_group_kernel, n_seq),
        grid=(g,),
        in_specs=[pl.BlockSpec((t, LANES), lambda i: (0, i // per_tile)),
                  pl.BlockSpec((1, S5_CHUNK * LANES, c), lambda i: (i % per_tile, 0, 0)),
                  spec((1, c, c)), spec((1, c, c)), spec((1, n2, c)),
                  spec((1, 1, n2)), spec((1, 1, n2))],
        out_specs=spec((1, r, c)),
        out_shape=jax.ShapeDtypeStruct((g, r, c), F32),
        scratch_shapes=[pltpu.VMEM((r, c), F32), pltpu.VMEM((r, n2), F32)],
        compiler_params=_cparams(("parallel",)),
        name="s5_groups",
    )(u, placement, w_state, m_intra, cx, a1[:, None, :], a2[:, None, :])


def _s5_branch(u, tables, bsz, seq):
    L, G, C = S5_CHUNK, S5_GROUPS, S5_GROUP
    nk = seq // L
    y = _s5_groups(u, tables, bsz)
    return y.reshape(G, bsz, nk, L, C).transpose(1, 2, 3, 0, 4).reshape(bsz * seq, G * C)


def _merge_kernel(alpha, attn_ref, y_ref, u_ref, ga_ref, gs_ref, x_ref, aup_ref, wglu_ref, sup_ref, wo_ref,
                  dskip_ref, bglu_ref, g_ref, b_ref, o_ref):
    a = _dot(attn_ref[...], aup_ref[...])
    y = y_ref[...] + dskip_ref[...] * u_ref[...]
    z = 0.5 * y * (1.0 + jnp.tanh(math.sqrt(2.0 / math.pi) * (y + 0.044715 * (y * y * y))))
    gate = _dot(z.astype(BF16), wglu_ref[...]) + bglu_ref[...]
    s = _dot((z * _sigmoid(gate)).astype(BF16), sup_ref[...])
    mixed = _sigmoid(ga_ref[...]) * a + _sigmoid(gs_ref[...]) * s
    r = alpha * x_ref[...] + _dot(mixed.astype(BF16), wo_ref[...])
    o_ref[...] = _layer_norm(r, g_ref[...], b_ref[...])


def _merge(alpha, attn, y5, u, g_a, g_s, x2, aup, wglu, sup, wo, dskip, bglu, ln_g, ln_b, tm=256):
    t, d = x2.shape
    row = lambda i: (i, 0)
    full = lambda a: pl.BlockSpec(a.shape, lambda i: (0, 0))
    return pl.pallas_call(
        functools.partial(_merge_kernel, alpha),
        grid=(t // tm,),
        in_specs=[pl.BlockSpec((tm, ATTN_WIDTH), row), pl.BlockSpec((tm, S5_WIDTH), row),
                  pl.BlockSpec((tm, S5_WIDTH), row), pl.BlockSpec((tm, d), row), pl.BlockSpec((tm, d), row),
                  pl.BlockSpec((tm, d), row), full(aup), full(wglu), full(sup), full(wo),
                  full(dskip), full(bglu), full(ln_g), full(ln_b)],
        out_specs=pl.BlockSpec((tm, d), row),
        out_shape=jax.ShapeDtypeStruct((t, d), F32),
        compiler_params=_cparams(("parallel",)),
        name="merge_ln1",
    )(attn, y5, u, g_a, g_s, x2, aup, wglu, sup, wo, dskip, bglu, ln_g, ln_b)


def _router_kernel(x_ref, wh_ref, wl_ref, bias_ref, idx_ref, wt_ref, rank_ref, cnt_ref, carry_ref):
    tm = x_ref.shape[0]

    @pl.when(pl.program_id(0) == 0)
    def _():
        carry_ref[...] = jnp.zeros(carry_ref.shape, F32)

    x = x_ref[...]
    xh = x.astype(BF16)
    xl = (x - xh.astype(F32)).astype(BF16)
    wh = wh_ref[...]
    logits = _dot_nt(wh, xh) + _dot_nt(wh, xl) + _dot_nt(wl_ref[...], xh)
    scores = _sigmoid(logits)
    biased = scores + bias_ref[...]
    gs_rows = GROUP_SIZE
    row_g = lax.broadcasted_iota(I32, (gs_rows, tm), 0)
    group_score = []
    for g in range(N_EXPERT_GROUPS):
        vg = biased[g * gs_rows:(g + 1) * gs_rows, :]
        m1 = jnp.max(vg, axis=0, keepdims=True)
        i1 = jnp.min(jnp.where(vg == m1, row_g, gs_rows), axis=0, keepdims=True)
        m2 = jnp.max(jnp.where(row_g == i1, REMOVED, vg), axis=0, keepdims=True)
        group_score.append(m1 + m2)
    pieces = []
    for g in range(N_EXPERT_GROUPS):
        beats = jnp.zeros((1, tm), I32)
        for g2 in range(N_EXPERT_GROUPS):
            if g2 == g:
                continue
            better = group_score[g2] > group_score[g]
            if g2 < g:
                better = better | (group_score[g2] == group_score[g])
            beats = beats + better.astype(I32)
        keep = beats < TOPK_GROUPS
        pieces.append(jnp.where(keep, biased[g * gs_rows:(g + 1) * gs_rows, :], NEG_INF))
    v = jnp.concatenate(pieces, axis=0)
    row = lax.broadcasted_iota(I32, (N_EXPERTS, tm), 0)
    idxs, wts = [], []
    multi_hot = jnp.zeros((N_EXPERTS, tm), F32)
    for _ in range(TOP_K):
        m = jnp.max(v, axis=0, keepdims=True)
        ix = jnp.min(jnp.where(v == m, row, N_EXPERTS), axis=0, keepdims=True)
        hit = row == ix
        idxs.append(ix)
        wts.append(jnp.sum(jnp.where(hit, scores, 0.0), axis=0, keepdims=True))
        multi_hot = multi_hot + hit.astype(F32)
        v = jnp.where(hit, REMOVED, v)
    wsum = wts[0]
    for k in range(1, TOP_K):
        wsum = wsum + wts[k]
    s_ix = lax.broadcasted_iota(I32, (tm, tm), 0)
    t_ix = lax.broadcasted_iota(I32, (tm, tm), 1)
    mh = multi_hot.astype(BF16)
    before = _dot(mh, (s_ix < t_ix).astype(BF16)) + carry_ref[...]
    carry_ref[...] = carry_ref[...] + _dot(mh, jnp.ones((tm, tm), BF16))
    for k in range(TOP_K):
        idx_ref[k:k + 1, :] = idxs[k]
        wt_ref[k:k + 1, :] = wts[k] / wsum * ROUTED_SCALE
        rank_ref[k:k + 1, :] = jnp.sum(jnp.where(row == idxs[k], before, 0.0), axis=0, keepdims=True).astype(I32)
    cnt_ref[...] = carry_ref[:, :LANES]


def _router(x1, wr_hi, wr_lo, bias_b, tm=256):
    t, d = x1.shape
    e = wr_hi.shape[0]
    tok = lambda i: (0, i)
    const = lambda i: (0, 0)
    return pl.pallas_call(
        _router_kernel,
        grid=(t // tm,),
        in_specs=[pl.BlockSpec((tm, d), lambda i: (i, 0)), pl.BlockSpec((e, d), const),
                  pl.BlockSpec((e, d), const), pl.BlockSpec((e, tm), const)],
        out_specs=[pl.BlockSpec((TOP_K, tm), tok), pl.BlockSpec((TOP_K, tm), tok),
                   pl.BlockSpec((TOP_K, tm), tok), pl.BlockSpec((e, LANES), const)],
        out_shape=[jax.ShapeDtypeStruct((TOP_K, t), I32), jax.ShapeDtypeStruct((TOP_K, t), F32),
                   jax.ShapeDtypeStruct((TOP_K, t), I32), jax.ShapeDtypeStruct((e, LANES), F32)],
        scratch_shapes=[pltpu.VMEM((e, tm), F32)],
        compiler_params=_cparams(("arbitrary",)),
        name="router_topk_rank",
    )(x1, wr_hi, wr_lo, bias_b)


def _dest_kernel(idx_ref, rank_ref, pstart_ref, dest_ref):
    pstart = pstart_ref[...]
    row = lax.broadcasted_iota(I32, pstart.shape, 0)
    for k in range(TOP_K):
        start = jnp.sum(jnp.where(row == idx_ref[k:k + 1, :], pstart, 0.0), axis=0, keepdims=True)
        dest_ref[k:k + 1, :] = start.astype(I32) + rank_ref[k:k + 1, :]


def _dest_rows(top_idx, rank, pstart_b):
    tm = pstart_b.shape[1]
    tok = lambda i: (0, i)
    return pl.pallas_call(
        _dest_kernel,
        grid=(top_idx.shape[1] // tm,),
        in_specs=[pl.BlockSpec((TOP_K, tm), tok), pl.BlockSpec((TOP_K, tm), tok),
                  pl.BlockSpec(pstart_b.shape, lambda i: (0, 0))],
        out_specs=pl.BlockSpec((TOP_K, tm), tok),
        out_shape=jax.ShapeDtypeStruct(top_idx.shape, I32),
        compiler_params=_cparams(("parallel",)),
        name="moe_dest_rows",
    )(top_idx, rank, pstart_b)


def _rows_wait(src_hbm, dst, sem, n_rows):
    pltpu.make_async_copy(src_hbm.at[pl.ds(0, n_rows), :], dst, sem).wait()


def _dispatch_kernel(pend_ref, cnt_ref, dest_ref, x_ref, xs_hbm, zbuf, zsem, sem):
    i = pl.program_id(0)
    tm = dest_ref.shape[1]
    mb = zbuf.shape[0]

    @pl.when(i == 0)
    def _():
        zbuf[...] = jnp.zeros(zbuf.shape, zbuf.dtype)

        def zero_copy(e):
            start = pl.multiple_of(pend_ref[e] - mb, mb)
            return pltpu.make_async_copy(zbuf, xs_hbm.at[pl.ds(start, mb), :], zsem)

        def z_start(e, c):
            @pl.when(cnt_ref[e] > 0)
            def _():
                zero_copy(e).start()
            return c

        def z_wait(e, c):
            @pl.when(cnt_ref[e] > 0)
            def _():
                zero_copy(e).wait()
            return c

        def tail_copy(b):
            return pltpu.make_async_copy(zbuf, xs_hbm.at[pl.ds(pl.multiple_of(b * mb, mb), mb), :], zsem)

        def t_start(b, c):
            tail_copy(b).start()
            return c

        def t_wait(b, c):
            tail_copy(b).wait()
            return c

        n_used = pend_ref[N_EXPERTS - 1] // mb
        n_blocks = xs_hbm.shape[0] // mb
        lax.fori_loop(0, N_EXPERTS, z_start, 0)
        lax.fori_loop(n_used, n_blocks, t_start, 0)
        lax.fori_loop(0, N_EXPERTS, z_wait, 0)
        lax.fori_loop(n_used, n_blocks, t_wait, 0)

    def issue(t, c):
        src = x_ref.at[pl.ds(t, 1), :]
        for k in range(TOP_K):
            pltpu.make_async_copy(src, xs_hbm.at[pl.ds(dest_ref[k, t], 1), :], sem).start()
        return c

    lax.fori_loop(0, tm, issue, 0, unroll=2)
    for _ in range(TOP_K):
        pltpu.make_async_copy(x_ref, xs_hbm.at[pl.ds(0, tm), :], sem).wait()


def _dispatch(padded_ends, counts, dest, x1, n_rows, tm=1024):
    t, d = x1.shape
    grid_spec = pltpu.PrefetchScalarGridSpec(
        num_scalar_prefetch=2,
        grid=(t // tm,),
        in_specs=[pl.BlockSpec((TOP_K, tm), lambda i, pe, cn: (0, i), memory_space=pltpu.SMEM),
                  pl.BlockSpec((tm, d), lambda i, pe, cn: (i, 0))],
        out_specs=pl.BlockSpec(memory_space=pl.ANY),
        scratch_shapes=[pltpu.VMEM((MOE_ROWS, d), x1.dtype), pltpu.SemaphoreType.DMA(()),
                        pltpu.SemaphoreType.DMA(())],
    )
    return pl.pallas_call(
        _dispatch_kernel,
        grid_spec=grid_spec,
        out_shape=jax.ShapeDtypeStruct((n_rows, d), x1.dtype),
        compiler_params=_cparams(("arbitrary",)),
        name="moe_dispatch",
    )(padded_ends, counts, dest, x1)


def _expert_kernel(first_ref, nblk_ref, nused_ref, xs_hbm, wg_ref, wu_ref, wd_ref, y_hbm,
                   xbuf, ybuf, wgb, wub, wdb, xsem, ysem):
    e = pl.program_id(0)
    n_used = nused_ref[0]
    nx, mb = xbuf.shape[0], xbuf.shape[1]

    def x_copy(g, slot):
        return pltpu.make_async_copy(xs_hbm.at[pl.ds(pl.multiple_of(g * mb, mb), mb), :], xbuf.at[slot],
                                     xsem.at[slot])

    nt = y_hbm.shape[1]
    sub = ybuf.shape[1] // mb

    def y_copy(g, slot):
        return pltpu.make_async_copy(
            ybuf.at[slot], y_hbm.at[pl.ds(pl.multiple_of(g * (mb * sub), mb * sub), mb * sub), :], ysem.at[slot])

    @pl.when(e == 0)
    def _():
        for s in range(nx - 1):
            @pl.when(s < n_used)
            def _():
                x_copy(s, s).start()

    wgb[...] = wg_ref[0, 0].astype(BF16)
    wub[...] = wu_ref[0, 0].astype(BF16)
    wdb[...] = wd_ref[0, 0].astype(BF16)

    def block(b, c):
        g = first_ref[e] + b
        slot = g % nx
        yslot = g % 2
        ahead = g + (nx - 1)

        @pl.when(ahead < n_used)
        def _():
            x_copy(ahead, ahead % nx).start()

        x_copy(g, slot).wait()

        @pl.when(g >= 2)
        def _():
            y_copy(g - 2, yslot).wait()

        xb = xbuf[slot].astype(BF16)
        hg = _dot(xb, wgb[...])
        hu = _dot(xb, wub[...])
        h = hg * _sigmoid(hg) * hu
        y = _dot(h.astype(BF16), wdb[...])
        for s in range(sub):
            ybuf[yslot, pl.ds(s, mb, stride=sub), :] = y[:, s * nt:(s + 1) * nt]
        y_copy(g, yslot).start()
        return c

    lax.fori_loop(0, nblk_ref[e], block, 0)

    @pl.when(e == pl.num_programs(0) - 1)
    def _():
        @pl.when(n_used >= 2)
        def _():
            y_copy(n_used - 2, n_used % 2).wait()

        y_copy(n_used - 1, (n_used - 1) % 2).wait()
        ybuf[0] = jnp.zeros(ybuf.shape[1:], ybuf.dtype)
        n_blocks = y_hbm.shape[0] // (mb * sub)

        def z_start(g, c):
            y_copy(g, 0).start()
            return c

        def z_wait(g, c):
            y_copy(g, 0).wait()
            return c

        lax.fori_loop(n_used, n_blocks, z_start, 0)
        lax.fori_loop(n_used, n_blocks, z_wait, 0)


def _experts(layer, first_block, n_blk, n_used, xs, w_gate_e, w_up_e, w_down_e):
    mb = MOE_ROWS
    n_rows, d = xs.shape
    _, n_exp, _, hdim = w_gate_e.shape
    wmap = lambda e, fb, nb, nu: (layer, e, 0, 0)
    grid_spec = pltpu.PrefetchScalarGridSpec(
        num_scalar_prefetch=3,
        grid=(n_exp,),
        in_specs=[pl.BlockSpec(memory_space=pl.ANY),
                  pl.BlockSpec((1, 1, d, hdim), wmap),
                  pl.BlockSpec((1, 1, d, hdim), wmap),
                  pl.BlockSpec((1, 1, hdim, d), wmap)],
        out_specs=pl.BlockSpec(memory_space=pl.ANY),
        scratch_shapes=[pltpu.VMEM((EXPERT_IN_SLOTS, mb, d), xs.dtype), pltpu.VMEM((2, mb * (d // LANES), LANES), F32),
                        pltpu.VMEM((d, hdim), BF16), pltpu.VMEM((d, hdim), BF16), pltpu.VMEM((hdim, d), BF16),
                        pltpu.SemaphoreType.DMA((EXPERT_IN_SLOTS,)), pltpu.SemaphoreType.DMA((2,))],
    )
    return pl.pallas_call(
        _expert_kernel,
        grid_spec=grid_spec,
        out_shape=jax.ShapeDtypeStruct((n_rows * (d // LANES), LANES), F32),
        compiler_params=_cparams(("arbitrary",)),
        name="routed_experts",
    )(first_block, n_blk, n_used, xs, w_gate_e, w_up_e, w_down_e)


def _final_kernel(alpha, dcur_ref, dnext_ref, x_ref, wt_ref, y_hbm, wsg_ref, wsu_ref, wsd_ref, g_ref, b_ref,
                  o_ref, ybuf, sems):
    i = pl.program_id(0)
    n_steps = pl.num_programs(0)
    tm = x_ref.shape[0]
    slot = i % 2

    sub = ybuf.shape[2] // tm
    nt = ybuf.shape[3]

    def gather(dref, s):
        def issue(t, c):
            dst_rows = pl.ds(pl.multiple_of(t * sub, sub), sub)
            for k in range(TOP_K):
                src_rows = pl.ds(pl.multiple_of(dref[k, t] * sub, sub), sub)
                pltpu.make_async_copy(y_hbm.at[src_rows, :], ybuf.at[s, k, dst_rows, :], sems.at[s]).start()
            return c

        lax.fori_loop(0, tm, issue, 0, unroll=2)

    @pl.when(i == 0)
    def _():
        gather(dcur_ref, 0)

    @pl.when(i + 1 < n_steps)
    def _():
        gather(dnext_ref, 1 - slot)

    x = x_ref[...]
    xb = x.astype(BF16)
    hg = _dot(xb, wsg_ref[...])
    hu = _dot(xb, wsu_ref[...])
    f = _dot((hg * _sigmoid(hg) * hu).astype(BF16), wsd_ref[...])
    for k in range(TOP_K):
        _rows_wait(y_hbm, ybuf.at[slot, k], sems.at[slot], tm * sub)
    wt = wt_ref[...]
    cols = []
    for s in range(sub):
        fs = f[:, s * nt:(s + 1) * nt]
        for k in range(TOP_K):
            fs = fs + wt[:, k:k + 1] * ybuf[slot, k, pl.ds(s, tm, stride=sub), :]
        cols.append(fs)
    f = jnp.concatenate(cols, axis=1)
    o_ref[...] = _layer_norm(alpha * x + f, g_ref[...], b_ref[...])


def _final(alpha, dest, x1, wt_tok, y_sorted, wsg, wsu, wsd, ln_g, ln_b, tm=256):
    t, d = x1.shape
    n = t // tm
    row = lambda i: (i, 0)
    full = lambda a: pl.BlockSpec(a.shape, lambda i: (0, 0))
    return pl.pallas_call(
        functools.partial(_final_kernel, alpha),
        grid=(n,),
        in_specs=[pl.BlockSpec((TOP_K, tm), lambda i: (0, i), memory_space=pltpu.SMEM),
                  pl.BlockSpec((TOP_K, tm), lambda i: (0, jnp.minimum(i + 1, n - 1)), memory_space=pltpu.SMEM),
                  pl.BlockSpec((tm, d), row), pl.BlockSpec((tm, TOP_K), row),
                  pl.BlockSpec(memory_space=pl.ANY), full(wsg), full(wsu), full(wsd), full(ln_g), full(ln_b)],
        out_specs=pl.BlockSpec((tm, d), row),
        out_shape=jax.ShapeDtypeStruct((t, d), F32),
        scratch_shapes=[pltpu.VMEM((2, TOP_K, tm * (d // LANES), LANES), F32), pltpu.SemaphoreType.DMA((2,))],
        compiler_params=_cparams(("arbitrary",)),
        name="combine_shared_ln2",
    )(dest, dest, x1, wt_tok, y_sorted, wsg, wsu, wsd, ln_g, ln_b)


def _block_plan(counts, n_tok):
    mb = MOE_ROWS
    n_blocks = -(-(n_tok * TOP_K + N_EXPERTS * (mb - 1)) // mb)
    padded = (counts + mb - 1) // mb * mb
    padded_ends = jnp.cumsum(padded).astype(I32)
    padded_starts = padded_ends - padded
    n_used = (padded_ends[-1] // mb).astype(I32).reshape(1)
    return padded_starts, padded_ends, padded_starts // mb, padded // mb, n_used, n_blocks * mb


def kernel(x, positions, w_in, attn_up, lam_re, lam_im, log_dt, b_re, b_im, c_re, c_im, d_skip, w_glu, b_glu,
           s5_up, w_o, ln1_g, ln1_b, w_router, router_bias, w_gate_e, w_up_e, w_down_e, ws_gate, ws_up, ws_down,
           ln2_g, ln2_b):
    bsz, seq, d = x.shape
    depth = w_in.shape[0]
    t = bsz * seq
    alpha = (2 * depth) ** 0.25
    assert seq % MOBA_BLOCK == 0 and seq % S5_CHUNK == 0 and t % 512 == 0

    inv_freq = ROPE_THETA ** (-jnp.arange(0, HEAD_DIM, 2, dtype=F32) / HEAD_DIM)
    ang = positions.astype(F32).reshape(t, 1) * inv_freq
    cos, sin = jnp.cos(ang), jnp.sin(ang)
    cos_t = jnp.concatenate([cos, cos, cos, cos], axis=1)
    sin_t = jnp.concatenate([-sin, sin, -sin, sin], axis=1)

    xc = x.reshape(t, d)
    row2 = lambda a: a.reshape(1, -1).astype(F32)
    for l in range(depth):
        w_v_t = w_in[l][:, 2 * ATTN_WIDTH:3 * ATTN_WIDTH].T.astype(BF16)
        q, k, v_t, u, g_a, g_s = _proj(xc, w_in[l].astype(BF16), w_v_t, cos_t, sin_t)
        attn = _attention(q, k, v_t, bsz, seq)
        tables = _s5_tables(lam_re[l], lam_im[l], log_dt[l], b_re[l], b_im[l], c_re[l], c_im[l])
        y5 = _s5_branch(u, tables, bsz, seq)
        x1 = _merge(alpha, attn, y5, u, g_a, g_s, xc, attn_up[l].astype(BF16), w_glu[l].astype(BF16),
                    s5_up[l].astype(BF16), w_o[l].astype(BF16), row2(d_skip[l]), row2(b_glu[l]),
                    row2(ln1_g[l]), row2(ln1_b[l]))
        wr_t = w_router[l].astype(F32).T
        wr_hi = wr_t.astype(BF16)
        wr_lo = (wr_t - wr_hi.astype(F32)).astype(BF16)
        bias_b = jnp.broadcast_to(router_bias[l].astype(F32)[:, None], (N_EXPERTS, 256))
        top_idx, top_w, rank, counts = _router(x1, wr_hi, wr_lo, bias_b)
        counts = counts[:, 0].astype(I32)
        pstart, pend, first_block, n_blk, n_used, n_rows = _block_plan(counts, t)
        dest = _dest_rows(top_idx, rank, jnp.broadcast_to(pstart.astype(F32)[:, None], (N_EXPERTS, 512)))
        xs = _dispatch(pend, counts, dest, x1, n_rows)
        y_sorted = _experts(l, first_block, n_blk, n_used, xs, w_gate_e, w_up_e, w_down_e)
        xc = _final(alpha, dest, x1, top_w.T, y_sorted, ws_gate[l].astype(BF16), ws_up[l].astype(BF16),
                    ws_down[l].astype(BF16), row2(ln2_g[l]), row2(ln2_b[l]))
    return xc.reshape(bsz, seq, d)
```

```python
import functools
import math

import jax
import jax.numpy as jnp
from jax import lax
from jax.experimental import pallas as pl
from jax.experimental.pallas import tpu as pltpu

F32 = jnp.float32
BF16 = jnp.bfloat16
I32 = jnp.int32

N_HEADS = 8
HEAD_DIM = 64
ATTN_WIDTH = N_HEADS * HEAD_DIM
MOBA_BLOCK = 256
MOBA_TOPK = 3
ROPE_THETA = 10000.0
S5_GROUP = 16
S5_GROUPS = 32
S5_WIDTH = S5_GROUP * S5_GROUPS
S5_STATE = 64
S5_CHUNK = 16
N_EXPERTS = 256
TOP_K = 8
N_EXPERT_GROUPS = 8
TOPK_GROUPS = 4
GROUP_SIZE = N_EXPERTS // N_EXPERT_GROUPS
ROUTED_SCALE = 2.5
LN_EPS = 1e-5
NEG_INF = -1e30
REMOVED = -3e38
QK_SCALE = HEAD_DIM ** -0.5 * math.log2(math.e)
V_ROWS = HEAD_DIM + 16

LANES = 128
MOE_ROWS = 256
EXPERT_IN_SLOTS = 6
VMEM_LIMIT = 56 * 1024 * 1024


def _cparams(sem):
    return pltpu.CompilerParams(dimension_semantics=sem, vmem_limit_bytes=VMEM_LIMIT)


def _sigmoid(x):
    return 1.0 / (1.0 + jnp.exp(-x))


def _layer_norm(r, g, b):
    mu = jnp.mean(r, axis=-1, keepdims=True)
    rc = r - mu
    var = jnp.mean(rc * rc, axis=-1, keepdims=True)
    return rc * lax.rsqrt(var + LN_EPS) * g + b


def _dot(a, b):
    return jnp.dot(a, b, preferred_element_type=F32)


def _dot_nt(a, b):
    return lax.dot_general(a, b, (((1,), (1,)), ((), ())), preferred_element_type=F32)


def _proj_kernel(x_ref, w_ref, wvt_ref, cos_ref, sin_ref, q_ref, k_ref, v_ref, u_ref, ga_ref, gs_ref):
    xb = x_ref[...].astype(BF16)
    cos = cos_ref[...]
    sin = sin_ref[...]
    lane = lax.broadcasted_iota(I32, cos.shape, 1)
    first_half = (lane % HEAD_DIM) < (HEAD_DIM // 2)

    def rope(t):
        rot = jnp.where(first_half, pltpu.roll(t, LANES - HEAD_DIM // 2, axis=1),
                        pltpu.roll(t, HEAD_DIM // 2, axis=1))
        return t * cos + rot * sin

    aw = ATTN_WIDTH
    for c in range(aw // LANES):
        sl = slice(c * LANES, (c + 1) * LANES)
        qc = _dot(xb, w_ref[:, c * LANES:(c + 1) * LANES])
        q_ref[:, sl] = (rope(qc) * QK_SCALE).astype(BF16)
        kc = _dot(xb, w_ref[:, aw + c * LANES:aw + (c + 1) * LANES])
        k_ref[:, sl] = rope(kc).astype(BF16)
    vt = _dot_nt(wvt_ref[...], xb).astype(BF16)
    ones = jnp.ones((V_ROWS - HEAD_DIM, MOBA_BLOCK), BF16)
    for p in range(v_ref.shape[0]):
        for c in range(v_ref.shape[1]):
            cols = slice(c * MOBA_BLOCK, (c + 1) * MOBA_BLOCK)
            heads = [vt[(2 * p + hh) * HEAD_DIM:(2 * p + hh + 1) * HEAD_DIM, cols] for hh in range(2)]
            v_ref[p, c] = jnp.concatenate([heads[0], ones, heads[1], ones], axis=0)
    u_ref[...] = _dot(xb, w_ref[:, 3 * aw:3 * aw + S5_WIDTH])
    d = ga_ref.shape[1]
    off = 3 * aw + S5_WIDTH
    ga_ref[...] = _dot(xb, w_ref[:, off:off + d])
    gs_ref[...] = _dot(xb, w_ref[:, off + d:off + 2 * d])


def _proj(x2, w_bf, wvt_bf, cos_t, sin_t, tm=512):
    t, d = x2.shape
    n = w_bf.shape[1]
    aw = ATTN_WIDTH
    row = lambda i: (i, 0)
    vshape = (aw // LANES, t // MOBA_BLOCK, 2 * V_ROWS, MOBA_BLOCK)
    return pl.pallas_call(
        _proj_kernel,
        grid=(t // tm,),
        in_specs=[pl.BlockSpec((tm, d), row),
                  pl.BlockSpec((d, n), lambda i: (0, 0)),
                  pl.BlockSpec((aw, d), lambda i: (0, 0)),
                  pl.BlockSpec((tm, LANES), row),
                  pl.BlockSpec((tm, LANES), row)],
        out_specs=[pl.BlockSpec((tm, aw), row), pl.BlockSpec((tm, aw), row),
                   pl.BlockSpec((vshape[0], tm // MOBA_BLOCK) + vshape[2:], lambda i: (0, i, 0, 0)),
                   pl.BlockSpec((tm, S5_WIDTH), row), pl.BlockSpec((tm, d), row), pl.BlockSpec((tm, d), row)],
        out_shape=[jax.ShapeDtypeStruct((t, aw), BF16), jax.ShapeDtypeStruct((t, aw), BF16),
                   jax.ShapeDtypeStruct(vshape, BF16), jax.ShapeDtypeStruct((t, S5_WIDTH), F32),
                   jax.ShapeDtypeStruct((t, d), F32), jax.ShapeDtypeStruct((t, d), F32)],
        compiler_params=_cparams(("parallel",)),
        name="proj_rope",
    )(x2, w_bf, wvt_bf, cos_t, sin_t)


def _attn_kernel(q_ref, k_ref, v_ref, o_ref, kmean_ref):
    i = pl.program_id(1)
    blk = MOBA_BLOCK
    n_blocks = k_ref.shape[0] // blk
    n_pairs = q_ref.shape[1] // LANES

    @pl.when(i == 0)
    def _():
        kmean_ref[...] = jnp.zeros(kmean_ref.shape, kmean_ref.dtype)
        for j in range(n_blocks):
            kj = k_ref[j * blk:(j + 1) * blk, :].astype(F32)
            kmean_ref[j:j + 1, :] = jnp.sum(kj, axis=0, keepdims=True) * (1.0 / blk)

    q_t = q_ref[...].astype(F32).T.astype(BF16)
    sub = lax.broadcasted_iota(I32, (LANES, blk), 0)
    key = lax.broadcasted_iota(I32, (blk, blk), 0)
    qry = lax.broadcasted_iota(I32, (blk, blk), 1)
    causal = key <= qry
    own = pl.ds(pl.multiple_of(i * blk, blk), blk)

    heads = [(pr, hh) for pr in range(n_pairs) for hh in range(2)]
    blk_row = lax.broadcasted_iota(I32, (HEAD_DIM, blk), 0)
    q_heads, gates, own_scores = [], [], []
    for pr, hh in heads:
        cols = slice(pr * LANES, (pr + 1) * LANES)
        q_pair = q_t[pr * LANES:(pr + 1) * LANES]
        in_head = (sub >= hh * HEAD_DIM) & (sub < (hh + 1) * HEAD_DIM)
        qh = jnp.where(in_head, q_pair, jnp.zeros_like(q_pair))
        q_heads.append(qh)
        gates.append(_dot(kmean_ref[:HEAD_DIM, cols].astype(BF16), qh))
        own_scores.append(_dot(k_ref[own, cols], qh))
    q_aug, own_probs = [], []
    for n, (pr, hh) in enumerate(heads):
        g = jnp.where(blk_row < i, gates[n], NEG_INF)
        selected = jnp.zeros(g.shape, jnp.bool_)
        for _ in range(MOBA_TOPK):
            mx = jnp.max(g, axis=0, keepdims=True)
            first = jnp.min(jnp.where(g == mx, blk_row, HEAD_DIM), axis=0, keepdims=True)
            pick = blk_row == first
            selected = selected | (pick & (blk_row < i))
            g = jnp.where(pick, REMOVED, g)
        sel_neg = jnp.where(selected, 0.0, NEG_INF).astype(BF16)
        q_own = q_heads[n][hh * HEAD_DIM:(hh + 1) * HEAD_DIM]
        q_aug.append(jnp.concatenate([q_own, sel_neg] if hh == 0 else [sel_neg, q_own], axis=0))
        s = jnp.where(causal, own_scores[n], NEG_INF)
        mx = jnp.max(s, axis=0, keepdims=True)
        own_probs.append((mx, jnp.exp2(s - mx).astype(BF16)))
    init = []
    for n, (pr, hh) in enumerate(heads):
        mx, p = own_probs[n]
        init.append((mx, _dot(v_ref[pr, i, hh * V_ROWS:(hh + 1) * V_ROWS, :], p)))

    lane2 = lax.broadcasted_iota(I32, (2 * blk, LANES), 1)
    second = (lax.broadcasted_iota(I32, (2 * blk, LANES), 0) >= blk).astype(I32)

    def body(jj, carry):
        rows = pl.ds(pl.multiple_of(jj * (2 * blk), 2 * blk), 2 * blk)
        blk_id = 2 * jj + second
        scores = []
        for n, (pr, hh) in enumerate(heads):
            k_pair = k_ref[rows, pr * LANES:(pr + 1) * LANES]
            other = (1 - hh) * HEAD_DIM
            in_other = (lane2 >= other) & (lane2 < other + HEAD_DIM)
            k_aug = jnp.where(in_other, (lane2 == other + blk_id).astype(BF16), k_pair)
            scores.append(_dot(k_aug, q_aug[n]))
        probs = []
        for n in range(len(heads)):
            m = carry[n][0]
            m_new = jnp.maximum(m, jnp.max(scores[n], axis=0, keepdims=True))
            probs.append((m_new, jnp.exp2(m - m_new), jnp.exp2(scores[n] - m_new).astype(BF16)))
        out = []
        for n, (pr, hh) in enumerate(heads):
            m_new, alpha, p = probs[n]
            vrows = slice(hh * V_ROWS, (hh + 1) * V_ROWS)
            pv = _dot(v_ref[pr, 2 * jj, vrows, :], p[:blk]) + _dot(v_ref[pr, 2 * jj + 1, vrows, :], p[blk:])
            out.append((m_new, alpha * carry[n][1] + pv))
        return tuple(out)

    res = lax.fori_loop(0, (i + 1) // 2, body, tuple(init))
    o_t = jnp.concatenate([acc[:HEAD_DIM] / acc[HEAD_DIM:HEAD_DIM + 1] for _, acc in res], axis=0)
    o_ref[...] = o_t.T.astype(BF16)


def _attention(q, k, v_t, bsz, seq):
    t, aw = q.shape
    blk = MOBA_BLOCK
    nb = seq // blk
    assert nb <= HEAD_DIM
    assert nb % 2 == 0
    qmap = lambda b, i: (b * nb + i, 0)
    return pl.pallas_call(
        _attn_kernel,
        grid=(bsz, nb),
        in_specs=[pl.BlockSpec((blk, aw), qmap),
                  pl.BlockSpec((seq, aw), lambda b, i: (b, 0)),
                  pl.BlockSpec((aw // LANES, nb, 2 * V_ROWS, blk), lambda b, i: (0, b, 0, 0))],
        out_specs=pl.BlockSpec((blk, aw), qmap),
        out_shape=jax.ShapeDtypeStruct((t, aw), BF16),
        scratch_shapes=[pltpu.VMEM((LANES, aw), F32)],
        compiler_params=_cparams(("parallel", "arbitrary")),
        name="moba_attention",
    )(q, k, v_t)


def _s5_tables(lam_re, lam_im, log_dt, b_re, b_im, c_re, c_im):
    hp = lax.Precision.HIGHEST
    L = S5_CHUNK
    lr, li = lam_re.astype(F32), lam_im.astype(F32)
    dt = jnp.exp(log_dt.astype(F32))[:, None]
    mag = jnp.exp(lr * dt)
    a_re = mag * jnp.cos(li * dt)
    a_im = mag * jnp.sin(li * dt)
    den = lr * lr + li * li
    nr = a_re - 1.0
    coef_re = (nr * lr + a_im * li) / den
    coef_im = (a_im * lr - nr * li) / den
    br, bi = b_re.astype(F32), b_im.astype(F32)
    bbar_re = coef_re[..., None] * br - coef_im[..., None] * bi
    bbar_im = coef_re[..., None] * bi + coef_im[..., None] * br
    tau = jnp.arange(L + 1, dtype=F32)[:, None, None]
    pmag = jnp.exp(tau * (lr * dt))
    p_re = pmag * jnp.cos(tau * (li * dt))
    p_im = pmag * jnp.sin(tau * (li * dt))
    cr, ci = c_re.astype(F32), c_im.astype(F32)
    ca_re = cr[None] * p_re[:, :, None, :] - ci[None] * p_im[:, :, None, :]
    ca_im = cr[None] * p_im[:, :, None, :] + ci[None] * p_re[:, :, None, :]
    kmat = (jnp.einsum('tgon,gni->tgio', ca_re[:L], bbar_re, precision=hp)
            - jnp.einsum('tgon,gni->tgio', ca_im[:L], bbar_im, precision=hp))
    s_ix = jnp.arange(L)[:, None]
    t_ix = jnp.arange(L)[None, :]
    lag = jnp.clip(t_ix - s_ix, 0, L - 1)
    m = kmat[lag]
    m = jnp.where((t_ix >= s_ix)[:, :, None, None, None], m, 0.0)
    m_intra = m.transpose(2, 0, 3, 1, 4).reshape(S5_GROUPS, L * S5_GROUP, L * S5_GROUP)
    rev = p_re[L - 1 - jnp.arange(L)], p_im[L - 1 - jnp.arange(L)]
    w_re = rev[0][..., None] * bbar_re[None] - rev[1][..., None] * bbar_im[None]
    w_im = rev[0][..., None] * bbar_im[None] + rev[1][..., None] * bbar_re[None]
    w_re = w_re.transpose(1, 0, 3, 2).reshape(S5_GROUPS, L * S5_GROUP, S5_STATE)
    w_im = w_im.transpose(1, 0, 3, 2).reshape(S5_GROUPS, L * S5_GROUP, S5_STATE)
    w_state = jnp.concatenate([w_re, w_im, w_im, w_re], axis=-1)
    cx_re = ca_re[1:].transpose(1, 3, 0, 2).reshape(S5_GROUPS, S5_STATE, L * S5_GROUP)
    cx_im = ca_im[1:].transpose(1, 3, 0, 2).reshape(S5_GROUPS, S5_STATE, L * S5_GROUP)
    cx = jnp.concatenate([cx_re, -cx_im], axis=1)
    al_re, al_im = p_re[L], p_im[L]
    a1 = jnp.concatenate([al_re, al_re], axis=-1)
    a2 = jnp.concatenate([-al_im, al_im], axis=-1)
    return m_intra.astype(BF16), w_state.astype(BF16), cx.astype(BF16), a1, a2


def _s5_group_kernel(n_seq, u_ref, p_ref, w_ref, m_ref, cx_ref, a1_ref, a2_ref, y_ref, e_ref, h_ref):
    r = e_ref.shape[0]
    chunk = u_ref.shape[0] // r
    ucat = jnp.concatenate([u_ref[pl.ds(s, r, stride=chunk), :] for s in range(chunk)], axis=1)
    u = _dot(ucat.astype(BF16), p_ref[0]).astype(BF16)
    e_ref[...] = _dot(u, w_ref[0])
    a1 = a1_ref[0]
    a2 = a2_ref[0]
    nk = e_ref.shape[0] // n_seq
    half = h_ref.shape[1]

    def step(k, carry):
        out = []
        for b in range(n_seq):
            hp, hq = carry[b]
            row = pl.ds(b * nk + k, 1)
            h_ref[row, :] = hp
            e = e_ref[row, :]
            out.append((a1 * hp + a2 * hq + e[:, :half], a1 * hq - a2 * hp + e[:, half:]))
        return tuple(out)

    zero = jnp.zeros((1, half), F32)
    lax.fori_loop(0, nk, step, tuple((zero, zero) for _ in range(n_seq)), unroll=4)
    y_ref[0] = _dot(u, m_ref[0]) + _dot(h_ref[...].astype(BF16), cx_ref[0])


def _s5_groups(u, tables, n_seq):
    m_intra, w_state, cx, a1, a2 = tables
    t = u.shape[0]
    g, n2, c = cx.shape
    r = t // S5_CHUNK
    per_tile = LANES // S5_GROUP
    row = jnp.arange(S5_CHUNK * LANES)
    col = jnp.arange(c)
    same_token = (row // LANES)[:, None] == (col // S5_GROUP)[None, :]
    lane_hit = (row % LANES)[None, :, None] == (jnp.arange(per_tile)[:, None, None] * S5_GROUP
                                                + (col % S5_GROUP)[None, None, :])
    placement = (same_token[None] & lane_hit).astype(BF16)
    spec = lambda shape: pl.BlockSpec(shape, lambda i: (i, 0, 0))
    return pl.pallas_call(
        functools.partial(_s5_group_kernel, n_seq),
        grid=(g,),
        in_specs=[pl.BlockSpec((t, LANES), lambda i: (0, i // per_tile)),
                  pl.BlockSpec((1, S5_CHUNK * LANES, c), lambda i: (i % per_tile, 0, 0)),
                  spec((1, c, c)), spec((1, c, c)), spec((1, n2, c)),
                  spec((1, 1, n2)), spec((1, 1, n2))],
        out_specs=spec((1, r, c)),
        out_shape=jax.ShapeDtypeStruct((g, r, c), F32),
        scratch_shapes=[pltpu.VMEM((r, c), F32), pltpu.VMEM((r, n2), F32)],
        compiler_params=_cparams(("parallel",)),
        name="s5_groups",
    )(u, placement, w_state, m_intra, cx, a1[:, None, :], a2[:, None, :])


def _s5_branch(u, tables, bsz, seq):
    L, G, C = S5_CHUNK, S5_GROUPS, S5_GROUP
    nk = seq // L
    y = _s5_groups(u, tables, bsz)
    return y.reshape(G, bsz, nk, L, C).transpose(1, 2, 3, 0, 4).reshape(bsz * seq, G * C)


def _merge_kernel(alpha, attn_ref, y_ref, u_ref, ga_ref, gs_ref, x_ref, aup_ref, wglu_ref, sup_ref, wo_ref,
                  dskip_ref, bglu_ref, g_ref, b_ref, o_ref):
    a = _dot(attn_ref[...], aup_ref[...])
    y = y_ref[...] + dskip_ref[...] * u_ref[...]
    z = 0.5 * y * (1.0 + jnp.tanh(math.sqrt(2.0 / math.pi) * (y + 0.044715 * (y * y * y))))
    gate = _dot(z.astype(BF16), wglu_ref[...]) + bglu_ref[...]
    s = _dot((z * _sigmoid(gate)).astype(BF16), sup_ref[...])
    mixed = _sigmoid(ga_ref[...]) * a + _sigmoid(gs_ref[...]) * s
    r = alpha * x_ref[...] + _dot(mixed.astype(BF16), wo_ref[...])
    o_ref[...] = _layer_norm(r, g_ref[...], b_ref[...])


def _merge(alpha, attn, y5, u, g_a, g_s, x2, aup, wglu, sup, wo, dskip, bglu, ln_g, ln_b, tm=256):
    t, d = x2.shape
    row = lambda i: (i, 0)
    full = lambda a: pl.BlockSpec(a.shape, lambda i: (0, 0))
    return pl.pallas_call(
        functools.partial(_merge_kernel, alpha),
        grid=(t // tm,),
        in_specs=[pl.BlockSpec((tm, ATTN_WIDTH), row), pl.BlockSpec((tm, S5_WIDTH), row),
                  pl.BlockSpec((tm, S5_WIDTH), row), pl.BlockSpec((tm, d), row), pl.BlockSpec((tm, d), row),
                  pl.BlockSpec((tm, d), row), full(aup), full(wglu), full(sup), full(wo),
                  full(dskip), full(bglu), full(ln_g), full(ln_b)],
        out_specs=pl.BlockSpec((tm, d), row),
        out_shape=jax.ShapeDtypeStruct((t, d), F32),
        compiler_params=_cparams(("parallel",)),
        name="merge_ln1",
    )(attn, y5, u, g_a, g_s, x2, aup, wglu, sup, wo, dskip, bglu, ln_g, ln_b)


def _router_kernel(x_ref, wh_ref, wl_ref, bias_ref, idx_ref, wt_ref, rank_ref, cnt_ref, carry_ref):
    tm = x_ref.shape[0]

    @pl.when(pl.program_id(0) == 0)
    def _():
        carry_ref[...] = jnp.zeros(carry_ref.shape, F32)

    x = x_ref[...]
    xh = x.astype(BF16)
    xl = (x - xh.astype(F32)).astype(BF16)
    wh = wh_ref[...]
    logits = _dot_nt(wh, xh) + _dot_nt(wh, xl) + _dot_nt(wl_ref[...], xh)
    scores = _sigmoid(logits)
    biased = scores + bias_ref[...]
    gs_rows = GROUP_SIZE
    row_g = lax.broadcasted_iota(I32, (gs_rows, tm), 0)
    group_score = []
    for g in range(N_EXPERT_GROUPS):
        vg = biased[g * gs_rows:(g + 1) * gs_rows, :]
        m1 = jnp.max(vg, axis=0, keepdims=True)
        i1 = jnp.min(jnp.where(vg == m1, row_g, gs_rows), axis=0, keepdims=True)
        m2 = jnp.max(jnp.where(row_g == i1, REMOVED, vg), axis=0, keepdims=True)
        group_score.append(m1 + m2)
    pieces = []
    for g in range(N_EXPERT_GROUPS):
        beats = jnp.zeros((1, tm), I32)
        for g2 in range(N_EXPERT_GROUPS):
            if g2 == g:
                continue
            better = group_score[g2] > group_score[g]
            if g2 < g:
                better = better | (group_score[g2] == group_score[g])
            beats = beats + better.astype(I32)
        keep = beats < TOPK_GROUPS
        pieces.append(jnp.where(keep, biased[g * gs_rows:(g + 1) * gs_rows, :], NEG_INF))
    v = jnp.concatenate(pieces, axis=0)
    row = lax.broadcasted_iota(I32, (N_EXPERTS, tm), 0)
    idxs, wts = [], []
    multi_hot = jnp.zeros((N_EXPERTS, tm), F32)
    for _ in range(TOP_K):
        m = jnp.max(v, axis=0, keepdims=True)
        ix = jnp.min(jnp.where(v == m, row, N_EXPERTS), axis=0, keepdims=True)
        hit = row == ix
        idxs.append(ix)
        wts.append(jnp.sum(jnp.where(hit, scores, 0.0), axis=0, keepdims=True))
        multi_hot = multi_hot + hit.astype(F32)
        v = jnp.where(hit, REMOVED, v)
    wsum = wts[0]
    for k in range(1, TOP_K):
        wsum = wsum + wts[k]
    s_ix = lax.broadcasted_iota(I32, (tm, tm), 0)
    t_ix = lax.broadcasted_iota(I32, (tm, tm), 1)
    mh = multi_hot.astype(BF16)
    before = _dot(mh, (s_ix < t_ix).astype(BF16)) + carry_ref[...]
    carry_ref[...] = carry_ref[...] + _dot(mh, jnp.ones((tm, tm), BF16))
    for k in range(TOP_K):
        idx_ref[k:k + 1, :] = idxs[k]
        wt_ref[k:k + 1, :] = wts[k] / wsum * ROUTED_SCALE
        rank_ref[k:k + 1, :] = jnp.sum(jnp.where(row == idxs[k], before, 0.0), axis=0, keepdims=True).astype(I32)
    cnt_ref[...] = carry_ref[:, :LANES]


def _router(x1, wr_hi, wr_lo, bias_b, tm=256):
    t, d = x1.shape
    e = wr_hi.shape[0]
    tok = lambda i: (0, i)
    const = lambda i: (0, 0)
    return pl.pallas_call(
        _router_kernel,
        grid=(t // tm,),
        in_specs=[pl.BlockSpec((tm, d), lambda i: (i, 0)), pl.BlockSpec((e, d), const),
                  pl.BlockSpec((e, d), const), pl.BlockSpec((e, tm), const)],
        out_specs=[pl.BlockSpec((TOP_K, tm), tok), pl.BlockSpec((TOP_K, tm), tok),
                   pl.BlockSpec((TOP_K, tm), tok), pl.BlockSpec((e, LANES), const)],
        out_shape=[jax.ShapeDtypeStruct((TOP_K, t), I32), jax.ShapeDtypeStruct((TOP_K, t), F32),
                   jax.ShapeDtypeStruct((TOP_K, t), I32), jax.ShapeDtypeStruct((e, LANES), F32)],
        scratch_shapes=[pltpu.VMEM((e, tm), F32)],
        compiler_params=_cparams(("arbitrary",)),
        name="router_topk_rank",
    )(x1, wr_hi, wr_lo, bias_b)


def _dest_kernel(idx_ref, rank_ref, pstart_ref, dest_ref):
    pstart = pstart_ref[...]
    row = lax.broadcasted_iota(I32, pstart.shape, 0)
    for k in range(TOP_K):
        start = jnp.sum(jnp.where(row == idx_ref[k:k + 1, :], pstart, 0.0), axis=0, keepdims=True)
        dest_ref[k:k + 1, :] = start.astype(I32) + rank_ref[k:k + 1, :]


def _dest_rows(top_idx, rank, pstart_b):
    tm = pstart_b.shape[1]
    tok = lambda i: (0, i)
    return pl.pallas_call(
        _dest_kernel,
        grid=(top_idx.shape[1] // tm,),
        in_specs=[pl.BlockSpec((TOP_K, tm), tok), pl.BlockSpec((TOP_K, tm), tok),
                  pl.BlockSpec(pstart_b.shape, lambda i: (0, 0))],
        out_specs=pl.BlockSpec((TOP_K, tm), tok),
        out_shape=jax.ShapeDtypeStruct(top_idx.shape, I32),
        compiler_params=_cparams(("parallel",)),
        name="moe_dest_rows",
    )(top_idx, rank, pstart_b)


def _rows_wait(src_hbm, dst, sem, n_rows):
    pltpu.make_async_copy(src_hbm.at[pl.ds(0, n_rows), :], dst, sem).wait()


def _dispatch_kernel(pend_ref, cnt_ref, dest_ref, x_ref, xs_hbm, zbuf, zsem, sem):
    i = pl.program_id(0)
    tm = dest_ref.shape[1]
    mb = zbuf.shape[0]

    @pl.when(i == 0)
    def _():
        zbuf[...] = jnp.zeros(zbuf.shape, zbuf.dtype)

        def zero_copy(e):
            start = pl.multiple_of(pend_ref[e] - mb, mb)
            return pltpu.make_async_copy(zbuf, xs_hbm.at[pl.ds(start, mb), :], zsem)

        def z_start(e, c):
            @pl.when(cnt_ref[e] > 0)
            def _():
                zero_copy(e).start()
            return c

        def z_wait(e, c):
            @pl.when(cnt_ref[e] > 0)
            def _():
                zero_copy(e).wait()
            return c

        def tail_copy(b):
            return pltpu.make_async_copy(zbuf, xs_hbm.at[pl.ds(pl.multiple_of(b * mb, mb), mb), :], zsem)

        def t_start(b, c):
            tail_copy(b).start()
            return c

        def t_wait(b, c):
            tail_copy(b).wait()
            return c

        n_used = pend_ref[N_EXPERTS - 1] // mb
        n_blocks = xs_hbm.shape[0] // mb
        lax.fori_loop(0, N_EXPERTS, z_start, 0)
        lax.fori_loop(n_used, n_blocks, t_start, 0)
        lax.fori_loop(0, N_EXPERTS, z_wait, 0)
        lax.fori_loop(n_used, n_blocks, t_wait, 0)

    def issue(t, c):
        src = x_ref.at[pl.ds(t, 1), :]
        for k in range(TOP_K):
            pltpu.make_async_copy(src, xs_hbm.at[pl.ds(dest_ref[k, t], 1), :], sem).start()
        return c

    lax.fori_loop(0, tm, issue, 0, unroll=2)
    for _ in range(TOP_K):
        pltpu.make_async_copy(x_ref, xs_hbm.at[pl.ds(0, tm), :], sem).wait()


def _dispatch(padded_ends, counts, dest, x1, n_rows, tm=1024):
    t, d = x1.shape
    grid_spec = pltpu.PrefetchScalarGridSpec(
        num_scalar_prefetch=2,
        grid=(t // tm,),
        in_specs=[pl.BlockSpec((TOP_K, tm), lambda i, pe, cn: (0, i), memory_space=pltpu.SMEM),
                  pl.BlockSpec((tm, d), lambda i, pe, cn: (i, 0))],
        out_specs=pl.BlockSpec(memory_space=pl.ANY),
        scratch_shapes=[pltpu.VMEM((MOE_ROWS, d), x1.dtype), pltpu.SemaphoreType.DMA(()),
                        pltpu.SemaphoreType.DMA(())],
    )
    return pl.pallas_call(
        _dispatch_kernel,
        grid_spec=grid_spec,
        out_shape=jax.ShapeDtypeStruct((n_rows, d), x1.dtype),
        compiler_params=_cparams(("arbitrary",)),
        name="moe_dispatch",
    )(padded_ends, counts, dest, x1)


def _expert_kernel(first_ref, nblk_ref, nused_ref, xs_hbm, wg_ref, wu_ref, wd_ref, y_hbm,
                   xbuf, ybuf, wgb, wub, wdb, xsem, ysem):
    e = pl.program_id(0)
    n_used = nused_ref[0]
    nx, mb = xbuf.shape[0], xbuf.shape[1]

    def x_copy(g, slot):
        return pltpu.make_async_copy(xs_hbm.at[pl.ds(pl.multiple_of(g * mb, mb), mb), :], xbuf.at[slot],
                                     xsem.at[slot])

    nt = y_hbm.shape[1]
    sub = ybuf.shape[1] // mb

    def y_copy(g, slot):
        return pltpu.make_async_copy(
            ybuf.at[slot], y_hbm.at[pl.ds(pl.multiple_of(g * (mb * sub), mb * sub), mb * sub), :], ysem.at[slot])

    @pl.when(e == 0)
    def _():
        for s in range(nx - 1):
            @pl.when(s < n_used)
            def _():
                x_copy(s, s).start()

    wgb[...] = wg_ref[0, 0].astype(BF16)
    wub[...] = wu_ref[0, 0].astype(BF16)
    wdb[...] = wd_ref[0, 0].astype(BF16)

    def block(b, c):
        g = first_ref[e] + b
        slot = g % nx
        yslot = g % 2
        ahead = g + (nx - 1)

        @pl.when(ahead < n_used)
        def _():
            x_copy(ahead, ahead % nx).start()

        x_copy(g, slot).wait()

        @pl.when(g >= 2)
        def _():
            y_copy(g - 2, yslot).wait()

        xb = xbuf[slot].astype(BF16)
        hg = _dot(xb, wgb[...])
        hu = _dot(xb, wub[...])
        h = hg * _sigmoid(hg) * hu
        y = _dot(h.astype(BF16), wdb[...])
        for s in range(sub):
            ybuf[yslot, pl.ds(s, mb, stride=sub), :] = y[:, s * nt:(s + 1) * nt]
        y_copy(g, yslot).start()
        return c

    lax.fori_loop(0, nblk_ref[e], block, 0)

    @pl.when(e == pl.num_programs(0) - 1)
    def _():
        @pl.when(n_used >= 2)
        def _():
            y_copy(n_used - 2, n_used % 2).wait()

        y_copy(n_used - 1, (n_used - 1) % 2).wait()
        ybuf[0] = jnp.zeros(ybuf.shape[1:], ybuf.dtype)
        n_blocks = y_hbm.shape[0] // (mb * sub)

        def z_start(g, c):
            y_copy(g, 0).start()
            return c

        def z_wait(g, c):
            y_copy(g, 0).wait()
            return c

        lax.fori_loop(n_used, n_blocks, z_start, 0)
        lax.fori_loop(n_used, n_blocks, z_wait, 0)


def _experts(layer, first_block, n_blk, n_used, xs, w_gate_e, w_up_e, w_down_e):
    mb = MOE_ROWS
    n_rows, d = xs.shape
    _, n_exp, _, hdim = w_gate_e.shape
    wmap = lambda e, fb, nb, nu: (layer, e, 0, 0)
    grid_spec = pltpu.PrefetchScalarGridSpec(
        num_scalar_prefetch=3,
        grid=(n_exp,),
        in_specs=[pl.BlockSpec(memory_space=pl.ANY),
                  pl.BlockSpec((1, 1, d, hdim), wmap),
                  pl.BlockSpec((1, 1, d, hdim), wmap),
                  pl.BlockSpec((1, 1, hdim, d), wmap)],
        out_specs=pl.BlockSpec(memory_space=pl.ANY),
        scratch_shapes=[pltpu.VMEM((EXPERT_IN_SLOTS, mb, d), xs.dtype), pltpu.VMEM((2, mb * (d // LANES), LANES), F32),
                        pltpu.VMEM((d, hdim), BF16), pltpu.VMEM((d, hdim), BF16), pltpu.VMEM((hdim, d), BF16),
                        pltpu.SemaphoreType.DMA((EXPERT_IN_SLOTS,)), pltpu.SemaphoreType.DMA((2,))],
    )
    return pl.pallas_call(
        _expert_kernel,
        grid_spec=grid_spec,
        out_shape=jax.ShapeDtypeStruct((n_rows * (d // LANES), LANES), F32),
        compiler_params=_cparams(("arbitrary",)),
        name="routed_experts",
    )(first_block, n_blk, n_used, xs, w_gate_e, w_up_e, w_down_e)


def _final_kernel(alpha, dcur_ref, dnext_ref, x_ref, wt_ref, y_hbm, wsg_ref, wsu_ref, wsd_ref, g_ref, b_ref,
                  o_ref, ybuf, sems):
    i = pl.program_id(0)
    n_steps = pl.num_programs(0)
    tm = x_ref.shape[0]
    slot = i % 2

    sub = ybuf.shape[2] // tm
    nt = ybuf.shape[3]

    def gather(dref, s):
        def issue(t, c):
            dst_rows = pl.ds(pl.multiple_of(t * sub, sub), sub)
            for k in range(TOP_K):
                src_rows = pl.ds(pl.multiple_of(dref[k, t] * sub, sub), sub)
                pltpu.make_async_copy(y_hbm.at[src_rows, :], ybuf.at[s, k, dst_rows, :], sems.at[s]).start()
            return c

        lax.fori_loop(0, tm, issue, 0, unroll=2)

    @pl.when(i == 0)
    def _():
        gather(dcur_ref, 0)

    @pl.when(i + 1 < n_steps)
    def _():
        gather(dnext_ref, 1 - slot)

    x = x_ref[...]
    xb = x.astype(BF16)
    hg = _dot(xb, wsg_ref[...])
    hu = _dot(xb, wsu_ref[...])
    f = _dot((hg * _sigmoid(hg) * hu).astype(BF16), wsd_ref[...])
    for k in range(TOP_K):
        _rows_wait(y_hbm, ybuf.at[slot, k], sems.at[slot], tm * sub)
    wt = wt_ref[...]
    cols = []
    for s in range(sub):
        fs = f[:, s * nt:(s + 1) * nt]
        for k in range(TOP_K):
            fs = fs + wt[:, k:k + 1] * ybuf[slot, k, pl.ds(s, tm, stride=sub), :]
        cols.append(fs)
    f = jnp.concatenate(cols, axis=1)
    o_ref[...] = _layer_norm(alpha * x + f, g_ref[...], b_ref[...])


def _final(alpha, dest, x1, wt_tok, y_sorted, wsg, wsu, wsd, ln_g, ln_b, tm=256):
    t, d = x1.shape
    n = t // tm
    row = lambda i: (i, 0)
    full = lambda a: pl.BlockSpec(a.shape, lambda i: (0, 0))
    return pl.pallas_call(
        functools.partial(_final_kernel, alpha),
        grid=(n,),
        in_specs=[pl.BlockSpec((TOP_K, tm), lambda i: (0, i), memory_space=pltpu.SMEM),
                  pl.BlockSpec((TOP_K, tm), lambda i: (0, jnp.minimum(i + 1, n - 1)), memory_space=pltpu.SMEM),
                  pl.BlockSpec((tm, d), row), pl.BlockSpec((tm, TOP_K), row),
                  pl.BlockSpec(memory_space=pl.ANY), full(wsg), full(wsu), full(wsd), full(ln_g), full(ln_b)],
        out_specs=pl.BlockSpec((tm, d), row),
        out_shape=jax.ShapeDtypeStruct((t, d), F32),
        scratch_shapes=[pltpu.VMEM((2, TOP_K, tm * (d // LANES), LANES), F32), pltpu.SemaphoreType.DMA((2,))],
        compiler_params=_cparams(("arbitrary",)),
        name="combine_shared_ln2",
    )(dest, dest, x1, wt_tok, y_sorted, wsg, wsu, wsd, ln_g, ln_b)


def _block_plan(counts, n_tok):
    mb = MOE_ROWS
    n_blocks = -(-(n_tok * TOP_K + N_EXPERTS * (mb - 1)) // mb)
    padded = (counts + mb - 1) // mb * mb
    padded_ends = jnp.cumsum(padded).astype(I32)
    padded_starts = padded_ends - padded
    n_used = (padded_ends[-1] // mb).astype(I32).reshape(1)
    return padded_starts, padded_ends, padded_starts // mb, padded // mb, n_used, n_blocks * mb


def kernel(x, positions, w_in, attn_up, lam_re, lam_im, log_dt, b_re, b_im, c_re, c_im, d_skip, w_glu, b_glu,
           s5_up, w_o, ln1_g, ln1_b, w_router, router_bias, w_gate_e, w_up_e, w_down_e, ws_gate, ws_up, ws_down,
           ln2_g, ln2_b):
    bsz, seq, d = x.shape
    depth = w_in.shape[0]
    t = bsz * seq
    alpha = (2 * depth) ** 0.25
    assert seq % MOBA_BLOCK == 0 and seq % S5_CHUNK == 0 and t % 512 == 0

    inv_freq = ROPE_THETA ** (-jnp.arange(0, HEAD_DIM, 2, dtype=F32) / HEAD_DIM)
    ang = positions.astype(F32).reshape(t, 1) * inv_freq
    cos, sin = jnp.cos(ang), jnp.sin(ang)
    cos_t = jnp.concatenate([cos, cos, cos, cos], axis=1)
    sin_t = jnp.concatenate([-sin, sin, -sin, sin], axis=1)

    xc = x.reshape(t, d)
    row2 = lambda a: a.reshape(1, -1).astype(F32)
    for l in range(depth):
        w_v_t = w_in[l][:, 2 * ATTN_WIDTH:3 * ATTN_WIDTH].T.astype(BF16)
        q, k, v_t, u, g_a, g_s = _proj(xc, w_in[l].astype(BF16), w_v_t, cos_t, sin_t)
        attn = _attention(q, k, v_t, bsz, seq)
        tables = _s5_tables(lam_re[l], lam_im[l], log_dt[l], b_re[l], b_im[l], c_re[l], c_im[l])
        y5 = _s5_branch(u, tables, bsz, seq)
        x1 = _merge(alpha, attn, y5, u, g_a, g_s, xc, attn_up[l].astype(BF16), w_glu[l].astype(BF16),
                    s5_up[l].astype(BF16), w_o[l].astype(BF16), row2(d_skip[l]), row2(b_glu[l]),
                    row2(ln1_g[l]), row2(ln1_b[l]))
        wr_t = w_router[l].astype(F32).T
        wr_hi = wr_t.astype(BF16)
        wr_lo = (wr_t - wr_hi.astype(F32)).astype(BF16)
        bias_b = jnp.broadcast_to(router_bias[l].astype(F32)[:, None], (N_EXPERTS, 256))
        top_idx, top_w, rank, counts = _router(x1, wr_hi, wr_lo, bias_b)
        counts = counts[:, 0].astype(I32)
        pstart, pend, first_block, n_blk, n_used, n_rows = _block_plan(counts, t)
        dest = _dest_rows(top_idx, rank, jnp.broadcast_to(pstart.astype(F32)[:, None], (N_EXPERTS, 512)))
        xs = _dispatch(pend, counts, dest, x1, n_rows)
        y_sorted = _experts(l, first_block, n_blk, n_used, xs, w_gate_e, w_up_e, w_down_e)
        xc = _final(alpha, dest, x1, top_w.T, y_sorted, ws_gate[l].astype(BF16), ws_up[l].astype(BF16),
                    ws_down[l].astype(BF16), row2(ln2_g[l]), row2(ln2_b[l]))
    return xc.reshape(bsz, seq, d)
```
